```python
import math
import jax, jax.numpy as jnp
from jax import lax
import numpy as np

D_MODEL = 1024
BATCH = 8
SEQ = 4096
DEPTH = 4

GRID_W = 64
CTX_LEN = 256
N_MIXERS = 3
N_MOD = 6
EPS = 1e-6
S5_GROUP = 16
S5_GROUPS = D_MODEL // S5_GROUP
S5_STATE = 64
N_DIRS = 2
DT_MIN = 1e-3
DT_MAX = 1e-1
POOL_WINDOWS = (2, 4, 8, 16)
POOL_GROUPS = len(POOL_WINDOWS)
POOL_WIDTH = D_MODEL // POOL_GROUPS
FNET_GROUPS = 4
FNET_WIDTH = D_MODEL // FNET_GROUPS
D_FF = 2816
N_EXPERTS = 8
TOP_K = 2
D_FF_EXPERT = 3584
N_S5_LAYERS = (DEPTH - 1) // N_MIXERS + 1
N_POOL_LAYERS = (DEPTH - 2) // N_MIXERS + 1
N_FNET_LAYERS = (DEPTH - 3) // N_MIXERS + 1
N_DENSE_LAYERS = (DEPTH + 1) // 2
N_MOE_LAYERS = DEPTH // 2

kernel_name = "hybrid_s5_pool_fourier_moe_dit"


def rms_norm(h, gain):
    hf = h.astype(jnp.float32)
    inv = lax.rsqrt(jnp.mean(hf * hf, axis=-1, keepdims=True) + EPS)
    return (hf * inv * gain.astype(jnp.float32)).astype(h.dtype)


def ada_params(cond, w, b):
    m = jax.nn.silu(cond) @ w + b
    return jnp.split(m[..., None, :], N_MOD, axis=-1)


def modulate(h, shift, scale):
    return h * (1 + scale) + shift


def s5_discretise(lam_re, lam_im, log_step, b_re, b_im):
    lam = lax.complex(lam_re.astype(jnp.float32), lam_im.astype(jnp.float32))
    step = jnp.exp(log_step.astype(jnp.float32))[:, None]
    lam_bar = jnp.exp(lam * step)
    b = lax.complex(b_re.astype(jnp.float32), b_im.astype(jnp.float32))
    b_bar = ((lam_bar - 1.0) / lam)[..., None] * b
    return lam_bar, b_bar


def linear_recurrence(bu, lam_bar, reverse):
    decay = jnp.broadcast_to(lam_bar, bu.shape[1:])[None]

    def combine(left, right):
        a_l, b_l = left
        a_r, b_r = right
        return a_r * a_l, a_r * b_l + b_r

    _, h = lax.associative_scan(combine, (decay, bu), reverse=reverse, axis=1)
    return h


def s5_glu(y, glu_w, glu_b):
    z = jax.nn.gelu(y)
    return z * jax.nn.sigmoid(z @ glu_w.astype(jnp.float32) + glu_b.astype(jnp.float32))


def s5_mixer(u_lat, u_ctx, lam_re, lam_im, log_step, b_re, b_im, c_re, c_im, d, glu_w, glu_b, need_ctx):
    bsz, n_lat, _ = u_lat.shape
    n_ctx = u_ctx.shape[1]
    ul = u_lat.astype(jnp.float32).reshape(bsz, n_lat, S5_GROUPS, S5_GROUP).astype(jnp.complex64)
    uc = u_ctx.astype(jnp.float32).reshape(bsz, n_ctx, S5_GROUPS, S5_GROUP).astype(jnp.complex64)
    d32 = d.astype(jnp.float32)
    y_lat = d32 * u_lat.astype(jnp.float32)
    y_ctx = d32 * u_ctx.astype(jnp.float32) if need_ctx else None
    for direction in range(N_DIRS):
        reverse = direction == 1
        lam_bar, b_bar = s5_discretise(lam_re[direction], lam_im[direction], log_step[direction],
                                       b_re[direction], b_im[direction])
        c_mat = lax.complex(c_re[direction].astype(jnp.float32), c_im[direction].astype(jnp.float32))
        h_ctx = linear_recurrence(jnp.einsum('bngh,gph->bngp', uc, b_bar), lam_bar, reverse)
        h_start = h_ctx[:, 0] if reverse else h_ctx[:, -1]
        bu_lat = jnp.einsum('bngh,gph->bngp', ul, b_bar)
        bu_lat = bu_lat.at[:, -1 if reverse else 0].add(lam_bar * h_start)
        h_lat = linear_recurrence(bu_lat, lam_bar, reverse)
        y_lat = y_lat + jnp.einsum('bngp,ghp->bngh', h_lat, c_mat).real.reshape(bsz, n_lat, D_MODEL)
        if need_ctx:
            y_ctx = y_ctx + jnp.einsum('bngp,ghp->bngh', h_ctx, c_mat).real.reshape(bsz, n_ctx, D_MODEL)
    out_lat = s5_glu(y_lat, glu_w, glu_b).astype(u_lat.dtype)
    out_ctx = s5_glu(y_ctx, glu_w, glu_b).astype(u_ctx.dtype) if need_ctx else None
    return out_lat, out_ctx


def centred_pool_residual(h, window):
    width = h.shape[2]
    csum = jnp.concatenate([jnp.zeros_like(h[:, :, :1]), jnp.cumsum(h, axis=2)], axis=2)
    pos = jnp.arange(width)
    lo = jnp.clip(pos - window // 2, 0, width)
    hi = jnp.clip(pos + window // 2, 0, width)
    count = (hi - lo).astype(h.dtype)[:, None]
    return (jnp.take(csum, hi, axis=2) - jnp.take(csum, lo, axis=2)) / count - h


def pool_mixer(u, rows, width, w, b, scale):
    bsz, n, _ = u.shape
    hf = u.astype(jnp.float32).reshape(bsz, rows, width, D_MODEL)
    res = jnp.stack([centred_pool_residual(hf[..., g * POOL_WIDTH:(g + 1) * POOL_WIDTH], win)
                     for g, win in enumerate(POOL_WINDOWS)], axis=3)
    mixed = jnp.einsum('brwgc,gcd->brwgd', res, w.astype(jnp.float32)) + b.astype(jnp.float32).reshape(POOL_GROUPS, POOL_WIDTH)
    return (mixed.reshape(bsz, n, D_MODEL) * scale.astype(jnp.float32)).astype(u.dtype)


def fourier_mixer(u, w, b):
    bsz, n, _ = u.shape
    hg = u.astype(jnp.float32).reshape(bsz, n, FNET_GROUPS, FNET_WIDTH)
    mixed = jnp.fft.fftn(hg, axes=(1, 3), norm="ortho").real.reshape(bsz, n, D_MODEL)
    return (mixed @ w.astype(jnp.float32) + b.astype(jnp.float32)).astype(u.dtype)


def swiglu(h, w_gate, w_up, w_down):
    return (jax.nn.silu(h @ w_gate) * (h @ w_up)) @ w_down


def moe_swiglu(h, router, w_gate, w_up, w_down):
    logits = h.astype(jnp.float32) @ router.astype(jnp.float32)
    top_vals, top_idx = lax.top_k(logits, TOP_K)
    weights = jax.nn.softmax(top_vals, axis=-1)
    combine = jnp.sum(jax.nn.one_hot(top_idx, N_EXPERTS, dtype=jnp.float32) * weights[..., None], axis=-2)
    combine = combine.astype(h.dtype)
    out = jnp.zeros_like(h)
    for e in range(N_EXPERTS):
        out = out + combine[..., e:e + 1] * swiglu(h, w_gate[e], w_up[e], w_down[e])
    return out


def setup_inputs(seed: int = 0) -> dict:
    key = jax.random.key(seed)
    ks = iter(jax.random.split(key, 40))
    f32 = jnp.float32

    def nrm(shape, scale):
        return jax.random.normal(next(ks), shape, f32) * scale

    d = D_MODEL
    lam_im_base = jnp.pi * jnp.arange(S5_STATE, dtype=f32)
    s5_shape = (N_S5_LAYERS, N_DIRS, S5_GROUPS, S5_STATE)
    return {
        "x": nrm((BATCH, SEQ, d), 1.0),
        "c": nrm((BATCH, d), 1.0),
        "ctx": nrm((BATCH, CTX_LEN, d), 1.0),
        "c_ctx": nrm((d,), 1.0),
        "ada_w": nrm((DEPTH, d, N_MOD * d), 0.5 * d ** -0.5),
        "ada_b": nrm((DEPTH, N_MOD * d), 0.02),
        "norm_mix": 1.0 + nrm((DEPTH, d), 0.1),
        "norm_ffn": 1.0 + nrm((DEPTH, d), 0.1),
        "norm_final": 1.0 + nrm((d,), 0.1),
        "s5_lambda_re": -0.5 + nrm(s5_shape, 0.01),
        "s5_lambda_im": lam_im_base + nrm(s5_shape, 0.01),
        "s5_log_step": jax.random.uniform(next(ks), (N_S5_LAYERS, N_DIRS, S5_GROUPS), f32,
                                           math.log(DT_MIN), math.log(DT_MAX)),
        "s5_b_re": nrm(s5_shape + (S5_GROUP,), (2 * S5_GROUP) ** -0.5),
        "s5_b_im": nrm(s5_shape + (S5_GROUP,), (2 * S5_GROUP) ** -0.5),
        "s5_c_re": nrm((N_S5_LAYERS, N_DIRS, S5_GROUPS, S5_GROUP, S5_STATE), (2 * S5_STATE) ** -0.5),
        "s5_c_im": nrm((N_S5_LAYERS, N_DIRS, S5_GROUPS, S5_GROUP, S5_STATE), (2 * S5_STATE) ** -0.5),
        "s5_d": nrm((N_S5_LAYERS, d), 1.0),
        "s5_glu_w": nrm((N_S5_LAYERS, d, d), d ** -0.5),
        "s5_glu_b": nrm((N_S5_LAYERS, d), 0.02),
        "pool_w": nrm((N_POOL_LAYERS, POOL_GROUPS, POOL_WIDTH, POOL_WIDTH), POOL_WIDTH ** -0.5),
        "pool_b": nrm((N_POOL_LAYERS, d), 0.02),
        "pool_scale": 1.0 + nrm((N_POOL_LAYERS, d), 0.1),
        "fnet_w": nrm((N_FNET_LAYERS, d, d), d ** -0.5),
        "fnet_b": nrm((N_FNET_LAYERS, d), 0.02),
        "ffn_w_gate": nrm((N_DENSE_LAYERS, d, D_FF), d ** -0.5),
        "ffn_w_up": nrm((N_DENSE_LAYERS, d, D_FF), d ** -0.5),
        "ffn_w_down": nrm((N_DENSE_LAYERS, D_FF, d), D_FF ** -0.5),
        "moe_router": nrm((N_MOE_LAYERS, d, N_EXPERTS), d ** -0.5),
        "moe_w_gate": nrm((N_MOE_LAYERS, N_EXPERTS, d, D_FF_EXPERT), d ** -0.5),
        "moe_w_up": nrm((N_MOE_LAYERS, N_EXPERTS, d, D_FF_EXPERT), d ** -0.5),
        "moe_w_down": nrm((N_MOE_LAYERS, N_EXPERTS, D_FF_EXPERT, d), D_FF_EXPERT ** -0.5),
    }


def reference(x, c, ctx, c_ctx, ada_w, ada_b, norm_mix, norm_ffn, norm_final,
              s5_lambda_re, s5_lambda_im, s5_log_step, s5_b_re, s5_b_im, s5_c_re, s5_c_im,
              s5_d, s5_glu_w, s5_glu_b, pool_w, pool_b, pool_scale, fnet_w, fnet_b,
              ffn_w_gate, ffn_w_up, ffn_w_down, moe_router, moe_w_gate, moe_w_up, moe_w_down):
    rows = x.shape[1] // GRID_W
    h_lat, h_ctx = x, ctx
    for i in range(DEPTH):
        last = i == DEPTH - 1
        kind, j = i % N_MIXERS, i // N_MIXERS
        sh_m, sc_m, g_m, sh_f, sc_f, g_f = ada_params(c, ada_w[i], ada_b[i])
        csh_m, csc_m, cg_m, csh_f, csc_f, cg_f = ada_params(c_ctx, ada_w[i], ada_b[i])

        u_lat = modulate(rms_norm(h_lat, norm_mix[i]), sh_m, sc_m)
        u_ctx = modulate(rms_norm(h_ctx, norm_mix[i]), csh_m, csc_m)
        if kind == 0:
            y_lat, y_ctx = s5_mixer(u_lat, u_ctx, s5_lambda_re[j], s5_lambda_im[j], s5_log_step[j],
                                    s5_b_re[j], s5_b_im[j], s5_c_re[j], s5_c_im[j],
                                    s5_d[j], s5_glu_w[j], s5_glu_b[j], need_ctx=not last)
        elif kind == 1:
            y_lat = pool_mixer(u_lat, rows, GRID_W, pool_w[j], pool_b[j], pool_scale[j])
            y_ctx = None if last else pool_mixer(u_ctx, 1, u_ctx.shape[1], pool_w[j], pool_b[j], pool_scale[j])
        else:
            y_lat = fourier_mixer(u_lat, fnet_w[j], fnet_b[j])
            y_ctx = None if last else fourier_mixer(u_ctx, fnet_w[j], fnet_b[j])
        h_lat = h_lat + g_m * y_lat
        if not last:
            h_ctx = h_ctx + cg_m * y_ctx

        k = i // 2
        if i % 2 == 0:
            ffn = lambda h: swiglu(h, ffn_w_gate[k], ffn_w_up[k], ffn_w_down[k])
        else:
            ffn = lambda h: moe_swiglu(h, moe_router[k], moe_w_gate[k], moe_w_up[k], moe_w_down[k])
        h_lat = h_lat + g_f * ffn(modulate(rms_norm(h_lat, norm_ffn[i]), sh_f, sc_f))
        if not last:
            h_ctx = h_ctx + cg_f * ffn(modulate(rms_norm(h_ctx, norm_ffn[i]), csh_f, csc_f))
    return rms_norm(h_lat, norm_final)
```

```python
import functools
import math

import numpy as np
import jax
import jax.numpy as jnp
from jax import lax
from jax.experimental import pallas as pl
from jax.experimental.pallas import tpu as pltpu

F32 = jnp.float32
BF16 = jnp.bfloat16

EPS = 1e-6
GRID_W = 64
N_MOD = 6
S5_GROUP = 16
S5_STATE = 64
S5_CHUNK = 16
POOL_WINDOWS = (2, 4, 8, 16)
FNET_GROUPS = 4
N_EXPERTS = 8

LANES = 128
TILE = 256
ROW_BLOCK = 1024
VMEM_LIMIT = 56 * 1024 * 1024


def _cparams(sem, vmem=VMEM_LIMIT):
    return pltpu.CompilerParams(dimension_semantics=sem, vmem_limit_bytes=vmem)


def _dot(a, b):
    return jnp.dot(a, b, preferred_element_type=F32)


def _split(a):
    hi = a.astype(BF16)
    lo = (a - hi.astype(F32)).astype(BF16)
    return hi, lo


def _dot3(a, b):
    a_hi, a_lo = _split(a)
    b_hi, b_lo = _split(b)
    return _dot(a_hi, b_hi) + (_dot(a_lo, b_hi) + _dot(a_hi, b_lo))


def _normmod(h, gs, sh):
    inv = lax.rsqrt(jnp.mean(h * h, axis=-1, keepdims=True) + EPS)
    return h * inv * gs + sh


GS_M, SH_M, G_M, GS_F, SH_F, G_F = range(6)
MOD_ROWS = 8


def _ada_kernel(c_ref, w_ref, b_ref, o_ref):
    c = c_ref[...]
    s = c * (1.0 / (1.0 + jnp.exp(-c)))
    o_ref[0] = _dot3(s, w_ref[0]) + b_ref[0]


def _ada_all(cond, ada_w, ada_b):
    depth, d, n = ada_w.shape
    rows = cond.shape[0]
    tn = 1536
    return pl.pallas_call(
        _ada_kernel,
        out_shape=jax.ShapeDtypeStruct((depth, rows, n), F32),
        grid=(depth, n // tn),
        in_specs=[
            pl.BlockSpec((rows, d), lambda i, j: (0, 0)),
            pl.BlockSpec((1, d, tn), lambda i, j: (i, 0, j)),
            pl.BlockSpec((1, 1, tn), lambda i, j: (i, 0, j)),
        ],
        out_specs=pl.BlockSpec((1, rows, tn), lambda i, j: (i, 0, j)),
        compiler_params=_cparams(("arbitrary", "arbitrary")),
        name="ada_params",
    )(cond, ada_w, ada_b.reshape(depth, 1, n))


def _s5_pre_kernel(h_ref, mod_ref, u_ref):
    mod = mod_ref[0]
    u_ref[...] = _normmod(h_ref[...], mod[GS_M:GS_M + 1], mod[SH_M:SH_M + 1])


def _s5_pre(h, mod):
    r, d = h.shape
    return pl.pallas_call(
        _s5_pre_kernel,
        out_shape=jax.ShapeDtypeStruct((r, d), F32),
        grid=(r // TILE,),
        in_specs=[pl.BlockSpec((TILE, d), lambda i: (i, 0)),
                  pl.BlockSpec((1, MOD_ROWS, d), lambda i: (i, 0, 0))],
        out_specs=pl.BlockSpec((TILE, d), lambda i: (i, 0)),
        compiler_params=_cparams(("arbitrary",)),
        name="s5_pre",
    )(h, mod)


def _s5_weights(lam_re, lam_im, log_step, b_re, b_im, c_re, c_im):
    hp = lax.Precision.HIGHEST
    L = S5_CHUNK
    n_dir, g, p = lam_re.shape
    gb = LANES // S5_GROUP
    nb = g // gb
    step = jnp.exp(log_step)[..., None]
    ar, ai = lam_re * step, lam_im * step
    k = jnp.arange(L + 1, dtype=F32)[None, :, None, None]
    mag = jnp.exp(ar[:, None] * k)
    ang = ai[:, None] * k
    pr, pi = mag * jnp.cos(ang), mag * jnp.sin(ang)
    lbr, lbi = pr[:, 1] - 1.0, pi[:, 1]
    den = lam_re * lam_re + lam_im * lam_im
    qr = (lbr * lam_re + lbi * lam_im) / den
    qi = (lbi * lam_re - lbr * lam_im) / den
    bbr = qr[..., None] * b_re - qi[..., None] * b_im
    bbi = qr[..., None] * b_im + qi[..., None] * b_re
    er = pr[..., None] * bbr[:, None] - pi[..., None] * bbi[:, None]
    ei = pr[..., None] * bbi[:, None] + pi[..., None] * bbr[:, None]
    kmat = (jnp.einsum('dgop,dkgpi->dkgoi', c_re, er[:, :L], precision=hp)
            - jnp.einsum('dgop,dkgpi->dkgoi', c_im, ei[:, :L], precision=hp))
    clr = c_re[:, None] * pr[:, :, :, None, :] - c_im[:, None] * pi[:, :, :, None, :]
    cli = c_re[:, None] * pi[:, :, :, None, :] + c_im[:, None] * pr[:, :, :, None, :]

    eye = jnp.eye(gb, dtype=F32)
    s_idx = np.arange(L)[:, None]
    t_idx = np.arange(L)[None, :]
    w_ts, w_y = [], []
    for d in range(n_dir):
        lag = (t_idx - s_idx) if d == 0 else (s_idx - t_idx)
        mask = jnp.asarray(lag >= 0, F32)[:, :, None, None, None]
        kt = kmat[d][np.clip(lag, 0, L - 1)] * mask
        kt = kt.reshape(L, L, nb, gb, S5_GROUP, S5_GROUP)
        wt = jnp.einsum('stjgoi,gh->jsgitho', kt, eye).reshape(nb, L * LANES, L * LANES)
        ks = (L - 1 - np.arange(L)) if d == 0 else np.arange(L)
        parts = []
        for e in (er, ei):
            es = e[d][ks].reshape(L, nb, gb, p, S5_GROUP)
            parts.append(jnp.einsum('sjgpi,gh->jsgihp', es, eye))
        ws = jnp.stack(parts, axis=4).reshape(nb, L * LANES, 2 * gb * p)
        w_ts.append(jnp.concatenate([wt, ws], axis=-1).astype(BF16))
        ky = (np.arange(L) + 1) if d == 0 else (L - np.arange(L))
        parts = []
        for cl, sign in ((clr, 1.0), (cli, -1.0)):
            cy = sign * cl[d][ky].reshape(L, nb, gb, S5_GROUP, p)
            parts.append(jnp.einsum('tjgop,gh->jgptho', cy, eye))
        w_y.append(jnp.stack(parts, axis=1).reshape(nb, 2 * gb * p, L * LANES).astype(BF16))
    a = jnp.stack([pr[:, L].reshape(n_dir, nb, 1, gb * p), pi[:, L].reshape(n_dir, nb, 1, gb * p)], axis=3)
    return jnp.stack(w_ts), jnp.stack(w_y), a.reshape(n_dir, nb, 1, 2 * gb * p)


def _s5_scan_kernel(ul_ref, uc_ref, wts_ref, wy_ref, a_ref, yl_ref, yc_ref, h_scr, s_scr, hs_scr):
    d = pl.program_id(1)
    t = pl.program_id(2)
    L = S5_CHUNK
    half = a_ref.shape[-1] // 2
    nt = L * LANES

    def process(x_ref, y_ref):
        bsz, tok, _ = x_ref.shape
        nc = tok // L
        m = bsz * nc
        u = jnp.concatenate(
            [x_ref[:, pl.ds(s, nc, stride=L), :].reshape(m, LANES) for s in range(L)], axis=1).astype(BF16)
        s_all = _dot(u, wts_ref[0, 0, :, nt:])
        nq = s_all.shape[1] // LANES
        for q in range(nq):
            s_scr[q, 0:m, :] = s_all[:, q * LANES:(q + 1) * LANES]
        a = a_ref[0, 0]
        a_re = [jnp.broadcast_to(a[:, q * LANES:(q + 1) * LANES], (bsz, LANES)) for q in range(nq // 2)]
        a_im = [jnp.broadcast_to(a[:, half + q * LANES:half + (q + 1) * LANES], (bsz, LANES)) for q in range(nq // 2)]

        def scan(order):
            h_re = [h_scr[q] for q in range(nq // 2)]
            h_im = [h_scr[nq // 2 + q] for q in range(nq // 2)]
            for c in order:
                rows = pl.ds(c, bsz, stride=nc)
                for q in range(nq // 2):
                    hs_scr.at[q][rows, :] = h_re[q]
                    hs_scr.at[nq // 2 + q][rows, :] = h_im[q]
                    s_re = s_scr.at[q][rows, :]
                    s_im = s_scr.at[nq // 2 + q][rows, :]
                    h_re[q], h_im[q] = (a_re[q] * h_re[q] - a_im[q] * h_im[q] + s_re,
                                        a_re[q] * h_im[q] + a_im[q] * h_re[q] + s_im)
            for q in range(nq // 2):
                h_scr[q] = h_re[q]
                h_scr[nq // 2 + q] = h_im[q]

        @pl.when(d == 0)
        def _():
            scan(range(nc))

        @pl.when(d == 1)
        def _():
            scan(range(nc - 1, -1, -1))

        hs = jnp.concatenate([hs_scr[q, 0:m, :] for q in range(nq)], axis=1).astype(BF16)
        y = _dot(u, wts_ref[0, 0, :, :nt]) + _dot(hs, wy_ref[0, 0])
        for tt in range(L):
            y_ref[:, pl.ds(tt, nc, stride=L), :] = y[:, tt * LANES:(tt + 1) * LANES].reshape(bsz, nc, LANES)

    @pl.when(t == 0)
    def _():
        h_scr[...] = jnp.zeros_like(h_scr)
        process(uc_ref, yc_ref)

    @pl.when(t > 0)
    def _():
        process(ul_ref, yl_ref)


def _s5_scan(u3, w_ts, w_y, a, seq, ctx_len):
    bsz, _, d = u3.shape
    tl = 1024
    ntl = seq // tl
    nb = d // LANES
    n_dir = 2
    ctx_blk = seq // ctx_len
    nstate = a.shape[-1]

    def lat_idx(dd, t):
        i = jnp.maximum(t - 1, 0)
        return jnp.where(dd == 0, i, ntl - 1 - i)

    return pl.pallas_call(
        _s5_scan_kernel,
        out_shape=(jax.ShapeDtypeStruct((n_dir, bsz, seq, d), F32),
                   jax.ShapeDtypeStruct((n_dir, bsz, ctx_len, d), F32)),
        grid=(nb, n_dir, ntl + 1),
        in_specs=[
            pl.BlockSpec((bsz, tl, LANES), lambda j, dd, t: (0, lat_idx(dd, t), j)),
            pl.BlockSpec((bsz, ctx_len, LANES), lambda j, dd, t: (0, ctx_blk, j)),
            pl.BlockSpec((1, 1) + w_ts.shape[2:], lambda j, dd, t: (dd, j, 0, 0), pipeline_mode=pl.Buffered(1)),
            pl.BlockSpec((1, 1) + w_y.shape[2:], lambda j, dd, t: (dd, j, 0, 0), pipeline_mode=pl.Buffered(1)),
            pl.BlockSpec((1, 1, 1, nstate), lambda j, dd, t: (dd, j, 0, 0)),
        ],
        out_specs=(
            pl.BlockSpec((None, bsz, tl, LANES), lambda j, dd, t: (dd, 0, lat_idx(dd, t), j)),
            pl.BlockSpec((None, bsz, ctx_len, LANES), lambda j, dd, t: (dd, 0, 0, j)),
        ),
        scratch_shapes=[
            pltpu.VMEM((nstate // LANES, bsz, LANES), F32),
            pltpu.VMEM((nstate // LANES, bsz * tl // S5_CHUNK, LANES), F32),
            pltpu.VMEM((nstate // LANES, bsz * tl // S5_CHUNK, LANES), F32),
        ],
        compiler_params=_cparams(("arbitrary", "arbitrary", "arbitrary")),
        name="s5_scan",
    )(u3, u3, w_ts, w_y, a)


def _gelu_tanh(x):
    return 0.5 * x * (1.0 + jnp.tanh(math.sqrt(2.0 / math.pi) * (x + 0.044715 * (x * x * x))))


def _s5_glu_kernel(tiles_per_seq, h_ref, u_ref, ylf_ref, ylr_ref, ycf_ref, ycr_ref, mod_ref, dvec_ref,
                   w_ref, b_ref, o_ref):
    is_ctx = (pl.program_id(0) % tiles_per_seq) == tiles_per_seq - 1
    y_scan = jnp.where(is_ctx, ycf_ref[0, 0] + ycr_ref[0, 0], ylf_ref[0, 0] + ylr_ref[0, 0])
    y = dvec_ref[...] * u_ref[...] + y_scan
    z = _gelu_tanh(y)
    gate = 1.0 / (1.0 + jnp.exp(-(_dot(z.astype(BF16), w_ref[...]) + b_ref[...])))
    o_ref[...] = h_ref[...] + mod_ref[0, G_M:G_M + 1] * (z * gate)


def _s5_glu(h, u, y_lat, y_ctx, mod, dvec, glu_w, glu_b, seq):
    r, d = h.shape
    lat_tiles = seq // TILE
    tps = lat_tiles + 1

    def lat_map(dd):
        return lambda i: (dd, i // tps, jnp.minimum(i % tps, lat_tiles - 1), 0)

    def ctx_map(dd):
        return lambda i: (dd, i // tps, 0, 0)

    row = pl.BlockSpec((TILE, d), lambda i: (i, 0))
    vec = pl.BlockSpec((1, d), lambda i: (0, 0))
    return pl.pallas_call(
        functools.partial(_s5_glu_kernel, tps),
        out_shape=jax.ShapeDtypeStruct((r, d), F32),
        grid=(r // TILE,),
        in_specs=[row, row,
                  pl.BlockSpec((1, 1, TILE, d), lat_map(0)), pl.BlockSpec((1, 1, TILE, d), lat_map(1)),
                  pl.BlockSpec((1, 1, TILE, d), ctx_map(0)), pl.BlockSpec((1, 1, TILE, d), ctx_map(1)),
                  pl.BlockSpec((1, MOD_ROWS, d), lambda i: (i, 0, 0)),
                  vec, pl.BlockSpec((d, d), lambda i: (0, 0)), vec],
        out_specs=row,
        compiler_params=_cparams(("arbitrary",)),
        name="s5_glu",
    )(h, u, y_lat, y_lat, y_ctx, y_ctx, mod, dvec.reshape(1, d), glu_w.astype(BF16), glu_b.reshape(1, d))


def _pool_matrices(width, tile):
    mats = np.zeros((len(POOL_WINDOWS), tile, tile), np.float32)
    pos = np.arange(width)
    for g, win in enumerate(POOL_WINDOWS):
        lo = np.clip(pos - win // 2, 0, width)
        hi = np.clip(pos + win // 2, 0, width)
        for base in range(0, tile, width):
            for j in range(width):
                mats[g, base + j, base + lo[j]:base + hi[j]] = 1.0 / float(hi[j] - lo[j])
    return mats


def _pool_kernel(h_ref, mod_ref, a_ref, w_ref, b_ref, sc_ref, o_ref):
    mod = mod_ref[0]
    h = h_ref[...]
    u = _normmod(h, mod[GS_M:GS_M + 1], mod[SH_M:SH_M + 1])
    ub = u.astype(BF16)
    gw = w_ref.shape[-1]
    for g in range(w_ref.shape[0]):
        cols = slice(g * gw, (g + 1) * gw)
        res = _dot(a_ref[0, g], ub[:, cols]) - u[:, cols]
        mixed = _dot(res.astype(BF16), w_ref[g]) + b_ref[:, cols]
        o_ref[:, cols] = h[:, cols] + mod[G_M:G_M + 1, cols] * (mixed * sc_ref[:, cols])


def _pool(h, mod, pool_w, pool_b, pool_scale, seq):
    r, d = h.shape
    tps = seq // TILE + 1
    ng, gw, _ = pool_w.shape
    mats = jnp.asarray(np.stack([_pool_matrices(GRID_W, TILE), _pool_matrices(TILE, TILE)]), BF16)
    row = pl.BlockSpec((TILE, d), lambda i: (i, 0))
    vec = pl.BlockSpec((1, d), lambda i: (0, 0))
    return pl.pallas_call(
        _pool_kernel,
        out_shape=jax.ShapeDtypeStruct((r, d), F32),
        grid=(r // TILE,),
        in_specs=[row, pl.BlockSpec((1, MOD_ROWS, d), lambda i: (i, 0, 0)),
                  pl.BlockSpec((1, ng, TILE, TILE), lambda i: ((i % tps) // (tps - 1), 0, 0, 0)),
                  pl.BlockSpec((ng, gw, gw), lambda i: (0, 0, 0)), vec, vec],
        out_specs=row,
        compiler_params=_cparams(("arbitrary",)),
        name="pool_mixer",
    )(h, mod, mats, pool_w.astype(BF16), pool_b.reshape(1, d), pool_scale.reshape(1, d))


def _dft_tables(n):
    idx = np.arange(n)
    ang = 2.0 * np.pi * ((idx[:, None] * idx[None, :]) % n) / n
    scale = 1.0 / math.sqrt(n)
    return (np.cos(ang) * scale).astype(np.float32), (np.sin(ang) * scale).astype(np.float32)


def _fnet_fold_kernel(cc_ref, sc_ref, w_ref, o_ref):
    w = w_ref[...]
    d = w.shape[-1]
    o_ref[:, :d] = _dot3(cc_ref[...], w).astype(o_ref.dtype)
    o_ref[:, d:] = (-_dot3(sc_ref[...], w)).astype(o_ref.dtype)


def _fnet_fold(fnet_w):
    d = fnet_w.shape[0]
    gw = d // FNET_GROUPS
    cc, sc = _dft_tables(gw)
    sq = pl.BlockSpec((gw, gw), lambda i: (0, 0))
    return pl.pallas_call(
        _fnet_fold_kernel,
        out_shape=jax.ShapeDtypeStruct((d, 2 * d), BF16),
        grid=(FNET_GROUPS,),
        in_specs=[sq, sq, pl.BlockSpec((gw, d), lambda i: (i, 0))],
        out_specs=pl.BlockSpec((gw, 2 * d), lambda i: (i, 0)),
        compiler_params=_cparams(("arbitrary",)),
        name="fnet_fold",
    )(jnp.asarray(cc), jnp.asarray(sc), fnet_w)


def _fnet_proj_kernel(h_ref, mod_ref, g_ref, v_ref):
    for j in range(h_ref.shape[0] // TILE):
        rows = slice(j * TILE, (j + 1) * TILE)
        mod = mod_ref[j]
        u = _normmod(h_ref[rows, :], mod[GS_M:GS_M + 1], mod[SH_M:SH_M + 1])
        v_ref[rows, :] = _dot(u.astype(BF16), g_ref[...]).astype(v_ref.dtype)


def _fnet_proj(h, mod, g):
    r, d = h.shape
    sub = ROW_BLOCK // TILE
    return pl.pallas_call(
        _fnet_proj_kernel,
        out_shape=jax.ShapeDtypeStruct((r, 2 * d), BF16),
        grid=(r // ROW_BLOCK,),
        in_specs=[pl.BlockSpec((ROW_BLOCK, d), lambda i: (i, 0)),
                  pl.BlockSpec((sub, MOD_ROWS, d), lambda i: (i, 0, 0)),
                  pl.BlockSpec((d, 2 * d), lambda i: (0, 0))],
        out_specs=pl.BlockSpec((ROW_BLOCK, 2 * d), lambda i: (i, 0)),
        compiler_params=_cparams(("arbitrary",)),
        name="fnet_proj",
    )(h, mod, g)


def _dft_matrix(n):
    n1 = int(round(math.sqrt(n)))
    assert n1 * n1 == n
    hi = jnp.arange(n1, dtype=jnp.int32)[:, None]
    k = jnp.arange(n, dtype=jnp.int32)[None, :]
    ang_a = (2.0 * math.pi / n1) * ((hi * k) % n1).astype(F32)
    ang_b = (2.0 * math.pi / n) * ((hi * k) % n).astype(F32)
    ca, sa, cb, sb = jnp.cos(ang_a), jnp.sin(ang_a), jnp.cos(ang_b), jnp.sin(ang_b)
    scale = 1.0 / math.sqrt(n)
    c = (ca[:, None] * cb[None] - sa[:, None] * sb[None]).reshape(n, n) * scale
    s = (sa[:, None] * cb[None] + ca[:, None] * sb[None]).reshape(n, n) * scale
    return jnp.concatenate([c, s], axis=1).astype(BF16)


def _fnet_lat_kernel(f_ref, v_ref, h_ref, mod_ref, b_ref, o_ref, acc_ref):
    k = pl.program_id(2)

    @pl.when(k == 0)
    def _():
        acc_ref[...] = jnp.zeros_like(acc_ref)

    acc_ref[...] += _dot(f_ref[...], v_ref[0])

    @pl.when(k == pl.num_programs(2) - 1)
    def _():
        for j in range(h_ref.shape[1] // TILE):
            rows = slice(j * TILE, (j + 1) * TILE)
            o_ref[0, rows, :] = h_ref[0, rows, :] + mod_ref[0, j, G_M:G_M + 1] * (acc_ref[rows, :] + b_ref[...])


def _fnet_lat(dft, v3, h3, mod4, fnet_b, seq):
    bsz, tot, d = h3.shape
    tm, tk = ROW_BLOCK, min(2048, seq)
    kh = seq // tk
    sub = tm // TILE
    return pl.pallas_call(
        _fnet_lat_kernel,
        out_shape=jax.ShapeDtypeStruct((bsz, tot, d), F32),
        grid=(bsz, seq // tm, 2 * kh),
        in_specs=[pl.BlockSpec((tm, tk), lambda b, m, k: (m, k)),
                  pl.BlockSpec((1, tk, d), lambda b, m, k: (b, k % kh, k // kh)),
                  pl.BlockSpec((1, tm, d), lambda b, m, k: (b, m, 0)),
                  pl.BlockSpec((1, sub, MOD_ROWS, d), lambda b, m, k: (b, m, 0, 0)),
                  pl.BlockSpec((1, d), lambda b, m, k: (0, 0))],
        out_specs=pl.BlockSpec((1, tm, d), lambda b, m, k: (b, m, 0)),
        scratch_shapes=[pltpu.VMEM((tm, d), F32)],
        compiler_params=_cparams(("arbitrary", "arbitrary", "arbitrary")),
        name="fnet_dft_latent",
    )(dft, v3, h3, mod4, fnet_b.reshape(1, d))


def _fnet_ctx_kernel(f_ref, v_ref, h_ref, mod_ref, b_ref, prev_ref, o_ref):
    del prev_ref
    n = f_ref.shape[0]
    d = h_ref.shape[-1]
    acc = _dot(f_ref[:, :n], v_ref[0, :, :d]) + _dot(f_ref[:, n:], v_ref[0, :, d:])
    o_ref[0] = h_ref[0] + mod_ref[0, 0, G_M:G_M + 1] * (acc + b_ref[...])


def _fnet_ctx(dftc, v3, h3, mod4, fnet_b, partial, seq):
    bsz, tot, d = h3.shape
    n = tot - seq
    blk = seq // n
    return pl.pallas_call(
        _fnet_ctx_kernel,
        out_shape=jax.ShapeDtypeStruct((bsz, tot, d), F32),
        grid=(bsz,),
        in_specs=[pl.BlockSpec((n, 2 * n), lambda b: (0, 0)),
                  pl.BlockSpec((1, n, 2 * d), lambda b: (b, blk, 0)),
                  pl.BlockSpec((1, n, d), lambda b: (b, blk, 0)),
                  pl.BlockSpec((1, 1, MOD_ROWS, d), lambda b: (b, blk, 0, 0)),
                  pl.BlockSpec((1, d), lambda b: (0, 0)),
                  pl.BlockSpec(memory_space=pl.ANY)],
        out_specs=pl.BlockSpec((1, n, d), lambda b: (b, blk, 0)),
        input_output_aliases={5: 0},
        compiler_params=_cparams(("arbitrary",)),
        name="fnet_dft_context",
    )(dftc, v3, h3, mod4, fnet_b.reshape(1, d), partial)


def _silu(x):
    return x * (1.0 / (1.0 + jnp.exp(-x)))


def _ffn_kernel(h_ref, mod_ref, wg_ref, wu_ref, wd_ref, o_ref, u_scr, acc_ref):
    f = pl.program_id(1)
    sub = h_ref.shape[0] // TILE

    @pl.when(f == 0)
    def _():
        for j in range(sub):
            rows = slice(j * TILE, (j + 1) * TILE)
            mod = mod_ref[j]
            u_scr[rows, :] = _normmod(h_ref[rows, :], mod[GS_F:GS_F + 1], mod[SH_F:SH_F + 1]).astype(BF16)
        acc_ref[...] = jnp.zeros_like(acc_ref)

    u = u_scr[...]
    a = _silu(_dot(u, wg_ref[...])) * _dot(u, wu_ref[...])
    acc_ref[...] += _dot(a.astype(BF16), wd_ref[...])

    @pl.when(f == pl.num_programs(1) - 1)
    def _():
        for j in range(sub):
            rows = slice(j * TILE, (j + 1) * TILE)
            o_ref[rows, :] = h_ref[rows, :] + mod_ref[j, G_F:G_F + 1] * acc_ref[rows, :]


def _ffn(h, mod, w_gate, w_up, w_down):
    r, d = h.shape
    ff = w_gate.shape[1]
    tm, tf = ROW_BLOCK, 256
    sub = tm // TILE
    return pl.pallas_call(
        _ffn_kernel,
        out_shape=jax.ShapeDtypeStruct((r, d), F32),
        grid=(r // tm, ff // tf),
        in_specs=[pl.BlockSpec((tm, d), lambda i, f: (i, 0)),
                  pl.BlockSpec((sub, MOD_ROWS, d), lambda i, f: (i, 0, 0)),
                  pl.BlockSpec((d, tf), lambda i, f: (0, f)),
                  pl.BlockSpec((d, tf), lambda i, f: (0, f)),
                  pl.BlockSpec((tf, d), lambda i, f: (f, 0))],
        out_specs=pl.BlockSpec((tm, d), lambda i, f: (i, 0)),
        scratch_shapes=[pltpu.VMEM((tm, d), BF16), pltpu.VMEM((tm, d), F32)],
        compiler_params=_cparams(("arbitrary", "arbitrary")),
        name="ffn_swiglu",
    )(h, mod, w_gate.astype(BF16), w_up.astype(BF16), w_down.astype(BF16))


def _router_kernel(h_ref, mod_ref, r_ref, u_ref, cw_ref):
    mod = mod_ref[0]
    u = _normmod(h_ref[...], mod[GS_F:GS_F + 1], mod[SH_F:SH_F + 1])
    u_ref[...] = u.astype(BF16)
    logits = _dot3(u, r_ref[...])
    lane = lax.broadcasted_iota(jnp.int32, logits.shape, 1).astype(F32)
    neg = jnp.float32(-jnp.inf)
    logits = jnp.where(lane < N_EXPERTS, logits, neg)
    m1 = jnp.max(logits, axis=-1, keepdims=True)
    i1 = jnp.min(jnp.where(logits == m1, lane, float(LANES)), axis=-1, keepdims=True)
    rest = jnp.where(lane == i1, neg, logits)
    m2 = jnp.max(rest, axis=-1, keepdims=True)
    i2 = jnp.min(jnp.where(rest == m2, lane, float(LANES)), axis=-1, keepdims=True)
    e2 = jnp.exp(m2 - m1)
    w1 = 1.0 / (1.0 + e2)
    w2 = e2 / (1.0 + e2)
    cw_ref[...] = jnp.where(lane == i1, w1, jnp.where(lane == i2, w2, 0.0))


def _router(h, mod, router):
    r, d = h.shape
    rp = jnp.zeros((d, LANES), F32).at[:, :N_EXPERTS].set(router)
    return pl.pallas_call(
        _router_kernel,
        out_shape=(jax.ShapeDtypeStruct((r, d), BF16), jax.ShapeDtypeStruct((r, LANES), F32)),
        grid=(r // TILE,),
        in_specs=[pl.BlockSpec((TILE, d), lambda i: (i, 0)),
                  pl.BlockSpec((1, MOD_ROWS, d), lambda i: (i, 0, 0)),
                  pl.BlockSpec((d, LANES), lambda i: (0, 0))],
        out_specs=(pl.BlockSpec((TILE, d), lambda i: (i, 0)), pl.BlockSpec((TILE, LANES), lambda i: (i, 0))),
        compiler_params=_cparams(("arbitrary",)),
        name="moe_router",
    )(h, mod, rp)


def _moe_kernel(h_ref, u_ref, cw_ref, mod_ref, wg_ref, wu_ref, wd_ref, o_ref, acc_ref):
    e = pl.program_id(1)
    f = pl.program_id(2)

    @pl.when((e == 0) & (f == 0))
    def _():
        acc_ref[...] = jnp.zeros_like(acc_ref)

    cw = cw_ref[...]
    lane = lax.broadcasted_iota(jnp.int32, cw.shape, 1)
    cw_e = jnp.sum(jnp.where(lane == e, cw, 0.0), axis=-1, keepdims=True)
    u = u_ref[...]
    a = _silu(_dot(u, wg_ref[0])) * _dot(u, wu_ref[0]) * cw_e
    acc_ref[...] += _dot(a.astype(BF16), wd_ref[0])

    @pl.when((e == pl.num_programs(1) - 1) & (f == pl.num_programs(2) - 1))
    def _():
        for j in range(h_ref.shape[0] // TILE):
            rows = slice(j * TILE, (j + 1) * TILE)
            o_ref[rows, :] = h_ref[rows, :] + mod_ref[j, G_F:G_F + 1] * acc_ref[rows, :]


def _moe(h, u, cw, mod, w_gate, w_up, w_down):
    r, d = h.shape
    ne, _, ff = w_gate.shape
    tm, tf = ROW_BLOCK, 512
    sub = tm // TILE
    return pl.pallas_call(
        _moe_kernel,
        out_shape=jax.ShapeDtypeStruct((r, d), F32),
        grid=(r // tm, ne, ff // tf),
        in_specs=[pl.BlockSpec((tm, d), lambda i, e, f: (i, 0)),
                  pl.BlockSpec((tm, d), lambda i, e, f: (i, 0)),
                  pl.BlockSpec((tm, LANES), lambda i, e, f: (i, 0)),
                  pl.BlockSpec((sub, MOD_ROWS, d), lambda i, e, f: (i, 0, 0)),
                  pl.BlockSpec((1, d, tf), lambda i, e, f: (e, 0, f)),
                  pl.BlockSpec((1, d, tf), lambda i, e, f: (e, 0, f)),
                  pl.BlockSpec((1, tf, d), lambda i, e, f: (e, f, 0))],
        out_specs=pl.BlockSpec((tm, d), lambda i, e, f: (i, 0)),
        scratch_shapes=[pltpu.VMEM((tm, d), F32)],
        compiler_params=_cparams(("arbitrary", "arbitrary", "arbitrary")),
        name="moe_experts",
    )(h, u, cw, mod, w_gate.astype(BF16), w_up.astype(BF16), w_down.astype(BF16))


def _final_kernel(h_ref, g_ref, o_ref):
    h = h_ref[0]
    inv = lax.rsqrt(jnp.mean(h * h, axis=-1, keepdims=True) + EPS)
    o_ref[0] = h * inv * g_ref[...]


def _final_norm(h3, gain, seq):
    bsz, _, d = h3.shape
    return pl.pallas_call(
        _final_kernel,
        out_shape=jax.ShapeDtypeStruct((bsz, seq, d), F32),
        grid=(bsz, seq // TILE),
        in_specs=[pl.BlockSpec((1, TILE, d), lambda b, t: (b, t, 0)),
                  pl.BlockSpec((1, d), lambda b, t: (0, 0))],
        out_specs=pl.BlockSpec((1, TILE, d), lambda b, t: (b, t, 0)),
        compiler_params=_cparams(("arbitrary", "arbitrary")),
        name="final_norm",
    )(h3, gain.reshape(1, d))


def _tile_mods(mods, gain_mix, gain_ffn, bsz, lat_tiles):
    sh_m, sc_m, g_m, sh_f, sc_f, g_f = (mods[:, i] for i in range(N_MOD))
    rows = jnp.stack([gain_mix * (1.0 + sc_m), sh_m, g_m, gain_ffn * (1.0 + sc_f), sh_f, g_f,
                      jnp.zeros_like(g_f), jnp.zeros_like(g_f)], axis=1)
    lat = jnp.broadcast_to(rows[:bsz, None], (bsz, lat_tiles) + rows.shape[1:])
    ctx = jnp.broadcast_to(rows[bsz:, None], (bsz, 1) + rows.shape[1:])
    return jnp.concatenate([lat, ctx], axis=1).reshape(bsz * (lat_tiles + 1), MOD_ROWS, rows.shape[-1])


def kernel(x, c, ctx, c_ctx, ada_w, ada_b, norm_mix, norm_ffn, norm_final, s5_lambda_re, s5_lambda_im, s5_log_step, s5_b_re, s5_b_im, s5_c_re, s5_c_im, s5_d, s5_glu_w, s5_glu_b, pool_w, pool_b, pool_scale, fnet_w, fnet_b, ffn_w_gate, ffn_w_up, ffn_w_down, moe_router, moe_w_gate, moe_w_up, moe_w_down):
    bsz, seq, d = x.shape
    ctx_len = ctx.shape[1]
    depth = ada_w.shape[0]
    assert ctx_len == TILE and seq % ROW_BLOCK == 0 and d % LANES == 0
    tot = seq + ctx_len
    lat_tiles = seq // TILE
    tps = lat_tiles + 1
    r = bsz * tot
    assert r % ROW_BLOCK == 0

    cond = jnp.zeros((16, d), F32).at[:bsz].set(c).at[bsz].set(c_ctx)
    mods_all = _ada_all(cond, ada_w, ada_b)[:, :bsz + 1].reshape(depth, bsz + 1, N_MOD, d)

    h = jnp.concatenate([x, ctx], axis=1).reshape(r, d)
    for i in range(depth):
        kind, j = i % 3, i // 3
        mod = _tile_mods(mods_all[i], norm_mix[i], norm_ffn[i], bsz, lat_tiles)
        if kind == 0:
            u = _s5_pre(h, mod)
            w_ts, w_y, a = _s5_weights(s5_lambda_re[j], s5_lambda_im[j], s5_log_step[j],
                                       s5_b_re[j], s5_b_im[j], s5_c_re[j], s5_c_im[j])
            y_lat, y_ctx = _s5_scan(u.reshape(bsz, tot, d), w_ts, w_y, a, seq, ctx_len)
            h = _s5_glu(h, u, y_lat, y_ctx, mod, s5_d[j], s5_glu_w[j], s5_glu_b[j], seq)
        elif kind == 1:
            h = _pool(h, mod, pool_w[j], pool_b[j], pool_scale[j], seq)
        else:
            g = _fnet_fold(fnet_w[j])
            v3 = _fnet_proj(h, mod, g).reshape(bsz, tot, 2 * d)
            h3 = h.reshape(bsz, tot, d)
            mod4 = mod.reshape(bsz, tps, MOD_ROWS, d)
            part = _fnet_lat(_dft_matrix(seq), v3, h3, mod4, fnet_b[j], seq)
            cc, sc = _dft_tables(ctx_len)
            dftc = jnp.asarray(np.concatenate([cc, sc], axis=1), BF16)
            h = _fnet_ctx(dftc, v3, h3, mod4, fnet_b[j], part, seq).reshape(r, d)
        kk = i // 2
        if i % 2 == 0:
            h = _ffn(h, mod, ffn_w_gate[kk], ffn_w_up[kk], ffn_w_down[kk])
        else:
            ub, cw = _router(h, mod, moe_router[kk])
            h = _moe(h, ub, cw, mod, moe_w_gate[kk], moe_w_up[kk], moe_w_down[kk])
    return _final_norm(h.reshape(bsz, tot, d), norm_final, seq)
```

```python
import functools
import math

import numpy as np
import jax
import jax.numpy as jnp
from jax import lax
from jax.experimental import pallas as pl
from jax.experimental.pallas import tpu as pltpu

F32 = jnp.float32
BF16 = jnp.bfloat16

EPS = 1e-6
GRID_W = 64
N_MOD = 6
S5_GROUP = 16
S5_STATE = 64
S5_CHUNK = 16
POOL_WINDOWS = (2, 4, 8, 16)
FNET_GROUPS = 4
N_EXPERTS = 8

LANES = 128
TILE = 256
ROW_BLOCK = 1024
VMEM_LIMIT = 56 * 1024 * 1024


def _cparams(sem, vmem=VMEM_LIMIT):
    return pltpu.CompilerParams(dimension_semantics=sem, vmem_limit_bytes=vmem)


def _dot(a, b):
    return jnp.dot(a, b, preferred_element_type=F32)


def _split(a):
    hi = a.astype(BF16)
    lo = (a - hi.astype(F32)).astype(BF16)
    return hi, lo


def _dot3(a, b):
    a_hi, a_lo = _split(a)
    b_hi, b_lo = _split(b)
    return _dot(a_hi, b_hi) + (_dot(a_lo, b_hi) + _dot(a_hi, b_lo))


def _normmod(h, gs, sh):
    inv = lax.rsqrt(jnp.mean(h * h, axis=-1, keepdims=True) + EPS)
    return h * inv * gs + sh


GS_M, SH_M, G_M, GS_F, SH_F, G_F = range(6)
MOD_ROWS = 8


def _ada_kernel(c_ref, w_ref, b_ref, o_ref):
    c = c_ref[...]
    s = c * (1.0 / (1.0 + jnp.exp(-c)))
    o_ref[0] = _dot3(s, w_ref[0]) + b_ref[0]


def _ada_all(cond, ada_w, ada_b):
    depth, d, n = ada_w.shape
    rows = cond.shape[0]
    tn = 1536
    return pl.pallas_call(
        _ada_kernel,
        out_shape=jax.ShapeDtypeStruct((depth, rows, n), F32),
        grid=(depth, n // tn),
        in_specs=[
            pl.BlockSpec((rows, d), lambda i, j: (0, 0)),
            pl.BlockSpec((1, d, tn), lambda i, j: (i, 0, j)),
            pl.BlockSpec((1, 1, tn), lambda i, j: (i, 0, j)),
        ],
        out_specs=pl.BlockSpec((1, rows, tn), lambda i, j: (i, 0, j)),
        compiler_params=_cparams(("arbitrary", "arbitrary")),
        name="ada_params",
    )(cond, ada_w, ada_b.reshape(depth, 1, n))


def _s5_pre_kernel(h_ref, mod_ref, u_ref):
    mod = mod_ref[0]
    u_ref[...] = _normmod(h_ref[...], mod[GS_M:GS_M + 1], mod[SH_M:SH_M + 1])


def _s5_pre(h, mod):
    r, d = h.shape
    return pl.pallas_call(
        _s5_pre_kernel,
        out_shape=jax.ShapeDtypeStruct((r, d), F32),
        grid=(r // TILE,),
        in_specs=[pl.BlockSpec((TILE, d), lambda i: (i, 0)),
                  pl.BlockSpec((1, MOD_ROWS, d), lambda i: (i, 0, 0))],
        out_specs=pl.BlockSpec((TILE, d), lambda i: (i, 0)),
        compiler_params=_cparams(("arbitrary",)),
        name="s5_pre",
    )(h, mod)


def _s5_params(lam_re, lam_im, log_step, b_re, b_im, c_re, c_im):
    hp = lax.Precision.HIGHEST
    L = S5_CHUNK
    n_dir, g, p = lam_re.shape
    gb = LANES // S5_GROUP
    nb = g // gb
    hch = S5_GROUP
    step = jnp.exp(log_step)[..., None]
    ar, ai = lam_re * step, lam_im * step
    k = jnp.arange(L + 1, dtype=F32)[None, :, None, None]
    mag = jnp.exp(ar[:, None] * k)
    ang = ai[:, None] * k
    pr, pi = mag * jnp.cos(ang), mag * jnp.sin(ang)
    lbr, lbi = pr[:, 1] - 1.0, pi[:, 1]
    den = lam_re * lam_re + lam_im * lam_im
    qr = (lbr * lam_re + lbi * lam_im) / den
    qi = (lbi * lam_re - lbr * lam_im) / den
    bbr = qr[..., None] * b_re - qi[..., None] * b_im
    bbi = qr[..., None] * b_im + qi[..., None] * b_re
    er = pr[..., None] * bbr[:, None] - pi[..., None] * bbi[:, None]
    ei = pr[..., None] * bbi[:, None] + pi[..., None] * bbr[:, None]
    kmat = (jnp.einsum('dgop,dkgpi->dgiko', c_re, er[:, :L], precision=hp)
            - jnp.einsum('dgop,dkgpi->dgiko', c_im, ei[:, :L], precision=hp))
    kc = kmat.reshape(n_dir, nb, gb * hch, L * hch)
    es = jnp.stack([er[:, L - 1::-1], ei[:, L - 1::-1]], axis=0)
    ec = jnp.transpose(es, (1, 3, 2, 5, 0, 4)).reshape(n_dir, nb, gb, L, hch, 2, p)
    ec = jnp.transpose(ec, (0, 1, 3, 2, 4, 5, 6)).reshape(n_dir, nb, L, gb * hch, 2 * p)
    clr = c_re[:, None] * pr[:, 1:, :, None, :] - c_im[:, None] * pi[:, 1:, :, None, :]
    cli = c_re[:, None] * pi[:, 1:, :, None, :] + c_im[:, None] * pr[:, 1:, :, None, :]
    cs = jnp.stack([clr, -cli], axis=0)
    cl = jnp.transpose(cs, (1, 3, 0, 5, 2, 4)).reshape(n_dir, nb, gb, 2, p, L, hch)
    cl = jnp.transpose(cl, (0, 1, 3, 2, 4, 5, 6)).reshape(n_dir, nb, 2 * gb * p, L * hch)
    a = jnp.stack([pr[:, L].reshape(n_dir, nb, 1, gb * p), pi[:, L].reshape(n_dir, nb, 1, gb * p)], axis=3)
    return kc.astype(BF16), ec.astype(BF16), cl.astype(BF16), a.reshape(n_dir, nb, 1, 2 * gb * p)


def _replicators():
    gb = LANES // S5_GROUP
    r16 = np.zeros((S5_CHUNK, S5_GROUP, S5_CHUNK, gb, S5_GROUP), np.float32)
    for k in range(S5_CHUNK):
        for o in range(S5_GROUP):
            r16[k, o, k, :, o] = 1.0
    r64 = np.zeros((2, S5_STATE, 2, gb, S5_STATE), np.float32)
    for part in range(2):
        for q in range(S5_STATE):
            r64[part, q, part, :, q] = 1.0
    return (jnp.asarray(r16.reshape(S5_CHUNK * S5_GROUP, S5_CHUNK * LANES), BF16),
            jnp.asarray(r64.reshape(2 * S5_STATE, 2 * gb * S5_STATE), BF16))


def _s5_scan_kernel(ul_ref, uc_ref, kc_ref, ec_ref, cl_ref, r16_ref, r64_ref, a_ref, yl_ref, yc_ref,
                    h_scr, s_scr, hs_scr, wts_scr, wy_scr):
    d = pl.program_id(1)
    t = pl.program_id(2)
    L = S5_CHUNK
    nstate = a_ref.shape[-1]
    half = nstate // 2
    nt = L * LANES
    blk = lambda i: slice(i * LANES, (i + 1) * LANES)

    def build_weights():
        row_gi = lax.broadcasted_iota(jnp.int32, (LANES, LANES), 0) // S5_GROUP
        lane_go = lax.broadcasted_iota(jnp.int32, (LANES, LANES), 1) // S5_GROUP
        toep = _dot(kc_ref[0, 0], r16_ref[...])
        zero = jnp.zeros((LANES, LANES), BF16)
        for lag in range(L):
            piece = jnp.where(row_gi == lane_go, toep[:, blk(lag)], 0.0).astype(BF16)
            for s in range(L - lag):
                wts_scr[blk(s), blk(s + lag)] = piece
        for s in range(1, L):
            for tt in range(s):
                wts_scr[blk(s), blk(tt)] = zero
        row_gi_w = lax.broadcasted_iota(jnp.int32, (LANES, nstate), 0) // S5_GROUP
        lane_gp = (lax.broadcasted_iota(jnp.int32, (LANES, nstate), 1) % half) // S5_STATE
        for s in range(L):
            e = _dot(ec_ref[0, 0, s], r64_ref[...])
            wts_scr[blk(s), nt:] = jnp.where(row_gi_w == lane_gp, e, 0.0).astype(BF16)
        row_gp = (lax.broadcasted_iota(jnp.int32, (nstate, LANES), 0) % half) // S5_STATE
        lane_go_t = lax.broadcasted_iota(jnp.int32, (nstate, LANES), 1) // S5_GROUP
        for tt in range(L):
            w = _dot(cl_ref[0, 0], r16_ref[:, blk(tt)])
            wy_scr[:, blk(tt)] = jnp.where(row_gp == lane_go_t, w, 0.0).astype(BF16)

    def process(x_ref, y_ref):
        bsz, tok, _ = x_ref.shape
        nc = tok // L
        m = bsz * nc
        flip = lambda i, n: i + d * (n - 1 - 2 * i)
        u = jnp.concatenate(
            [x_ref[:, pl.ds(flip(s, L), nc, stride=L), :].reshape(m, LANES) for s in range(L)], axis=1).astype(BF16)
        s_all = _dot(u, wts_scr[:, nt:])
        nq = nstate // LANES
        nh = nq // 2
        for q in range(nq):
            s_scr[q, 0:m, :] = s_all[:, blk(q)]
        a = a_ref[0, 0]
        a_re = [jnp.broadcast_to(a[:, blk(q)], (bsz, LANES)) for q in range(nh)]
        a_im = [jnp.broadcast_to(a[:, blk(nh + q)], (bsz, LANES)) for q in range(nh)]
        h_re = [h_scr[q] for q in range(nh)]
        h_im = [h_scr[nh + q] for q in range(nh)]
        for c in range(nc):
            rows = pl.ds(flip(c, nc), bsz, stride=nc)
            for q in range(nh):
                hs_scr.at[q][rows, :] = h_re[q]
                hs_scr.at[nh + q][rows, :] = h_im[q]
                s_re = s_scr.at[q][rows, :]
                s_im = s_scr.at[nh + q][rows, :]
                h_re[q], h_im[q] = (a_re[q] * h_re[q] - a_im[q] * h_im[q] + s_re,
                                    a_re[q] * h_im[q] + a_im[q] * h_re[q] + s_im)
        for q in range(nh):
            h_scr[q] = h_re[q]
            h_scr[nh + q] = h_im[q]
        hs = jnp.concatenate([hs_scr[q, 0:m, :] for q in range(nq)], axis=1).astype(BF16)
        y = _dot(u, wts_scr[:, :nt]) + _dot(hs, wy_scr[...])
        for tt in range(L):
            y_ref[:, pl.ds(flip(tt, L), nc, stride=L), :] = y[:, blk(tt)].reshape(bsz, nc, LANES)

    @pl.when(t == 0)
    def _():
        build_weights()
        h_scr[...] = jnp.zeros_like(h_scr)
        process(uc_ref, yc_ref)

    @pl.when(t > 0)
    def _():
        process(ul_ref, yl_ref)


def _s5_scan(u3, kc, ec, cl, a, seq, ctx_len):
    bsz, _, d = u3.shape
    tl = 1024
    ntl = seq // tl
    nb = d // LANES
    n_dir = 2
    ctx_blk = seq // ctx_len
    nstate = a.shape[-1]
    nt = S5_CHUNK * LANES
    r16, r64 = _replicators()

    def lat_idx(dd, t):
        i = jnp.maximum(t - 1, 0)
        return jnp.where(dd == 0, i, ntl - 1 - i)

    per_block = lambda arr: pl.BlockSpec((1, 1) + arr.shape[2:], lambda j, dd, t: (dd, j) + (0,) * (arr.ndim - 2))
    const = lambda arr: pl.BlockSpec(arr.shape, lambda j, dd, t: (0,) * arr.ndim)
    return pl.pallas_call(
        _s5_scan_kernel,
        out_shape=(jax.ShapeDtypeStruct((n_dir, bsz, seq, d), F32),
                   jax.ShapeDtypeStruct((n_dir, bsz, ctx_len, d), F32)),
        grid=(nb, n_dir, ntl + 1),
        in_specs=[
            pl.BlockSpec((bsz, tl, LANES), lambda j, dd, t: (0, lat_idx(dd, t), j)),
            pl.BlockSpec((bsz, ctx_len, LANES), lambda j, dd, t: (0, ctx_blk, j)),
            per_block(kc), per_block(ec), per_block(cl), const(r16), const(r64), per_block(a),
        ],
        out_specs=(
            pl.BlockSpec((None, bsz, tl, LANES), lambda j, dd, t: (dd, 0, lat_idx(dd, t), j)),
            pl.BlockSpec((None, bsz, ctx_len, LANES), lambda j, dd, t: (dd, 0, 0, j)),
        ),
        scratch_shapes=[
            pltpu.VMEM((nstate // LANES, bsz, LANES), F32),
            pltpu.VMEM((nstate // LANES, bsz * tl // S5_CHUNK, LANES), F32),
            pltpu.VMEM((nstate // LANES, bsz * tl // S5_CHUNK, LANES), F32),
            pltpu.VMEM((nt, nt + nstate), BF16),
            pltpu.VMEM((nstate, nt), BF16),
        ],
        compiler_params=_cparams(("arbitrary", "arbitrary", "arbitrary")),
        name="s5_scan",
    )(u3, u3, kc, ec, cl, r16, r64, a)


def _gelu_tanh(x):
    return 0.5 * x * (1.0 + jnp.tanh(math.sqrt(2.0 / math.pi) * (x + 0.044715 * (x * x * x))))


def _s5_glu_kernel(tiles_per_seq, h_ref, u_ref, ylf_ref, ylr_ref, ycf_ref, ycr_ref, mod_ref, dvec_ref,
                   w_ref, b_ref, o_ref):
    is_ctx = (pl.program_id(0) % tiles_per_seq) == tiles_per_seq - 1
    y_scan = jnp.where(is_ctx, ycf_ref[0, 0] + ycr_ref[0, 0], ylf_ref[0, 0] + ylr_ref[0, 0])
    y = dvec_ref[...] * u_ref[...] + y_scan
    z = _gelu_tanh(y)
    gate = 1.0 / (1.0 + jnp.exp(-(_dot(z.astype(BF16), w_ref[...]) + b_ref[...])))
    o_ref[...] = h_ref[...] + mod_ref[0, G_M:G_M + 1] * (z * gate)


def _s5_glu(h, u, y_lat, y_ctx, mod, dvec, glu_w, glu_b, seq):
    r, d = h.shape
    lat_tiles = seq // TILE
    tps = lat_tiles + 1

    def lat_map(dd):
        return lambda i: (dd, i // tps, jnp.minimum(i % tps, lat_tiles - 1), 0)

    def ctx_map(dd):
        return lambda i: (dd, i // tps, 0, 0)

    row = pl.BlockSpec((TILE, d), lambda i: (i, 0))
    vec = pl.BlockSpec((1, d), lambda i: (0, 0))
    return pl.pallas_call(
        functools.partial(_s5_glu_kernel, tps),
        out_shape=jax.ShapeDtypeStruct((r, d), F32),
        grid=(r // TILE,),
        in_specs=[row, row,
                  pl.BlockSpec((1, 1, TILE, d), lat_map(0)), pl.BlockSpec((1, 1, TILE, d), lat_map(1)),
                  pl.BlockSpec((1, 1, TILE, d), ctx_map(0)), pl.BlockSpec((1, 1, TILE, d), ctx_map(1)),
                  pl.BlockSpec((1, MOD_ROWS, d), lambda i: (i, 0, 0)),
                  vec, pl.BlockSpec((d, d), lambda i: (0, 0)), vec],
        out_specs=row,
        compiler_params=_cparams(("arbitrary",)),
        name="s5_glu",
    )(h, u, y_lat, y_lat, y_ctx, y_ctx, mod, dvec.reshape(1, d), glu_w.astype(BF16), glu_b.reshape(1, d))


def _pool_matrices(width, tile):
    mats = np.zeros((len(POOL_WINDOWS), tile, tile), np.float32)
    pos = np.arange(width)
    for g, win in enumerate(POOL_WINDOWS):
        lo = np.clip(pos - win // 2, 0, width)
        hi = np.clip(pos + win // 2, 0, width)
        for base in range(0, tile, width):
            for j in range(width):
                mats[g, base + j, base + lo[j]:base + hi[j]] = 1.0 / float(hi[j] - lo[j])
    return mats


def _pool_kernel(h_ref, mod_ref, a_ref, w_ref, b_ref, sc_ref, o_ref):
    mod = mod_ref[0]
    h = h_ref[...]
    u = _normmod(h, mod[GS_M:GS_M + 1], mod[SH_M:SH_M + 1])
    ub = u.astype(BF16)
    gw = w_ref.shape[-1]
    for g in range(w_ref.shape[0]):
        cols = slice(g * gw, (g + 1) * gw)
        res = _dot(a_ref[0, g], ub[:, cols]) - u[:, cols]
        mixed = _dot(res.astype(BF16), w_ref[g]) + b_ref[:, cols]
        o_ref[:, cols] = h[:, cols] + mod[G_M:G_M + 1, cols] * (mixed * sc_ref[:, cols])


def _pool(h, mod, pool_w, pool_b, pool_scale, seq):
    r, d = h.shape
    tps = seq // TILE + 1
    ng, gw, _ = pool_w.shape
    mats = jnp.asarray(np.stack([_pool_matrices(GRID_W, TILE), _pool_matrices(TILE, TILE)]), BF16)
    row = pl.BlockSpec((TILE, d), lambda i: (i, 0))
    vec = pl.BlockSpec((1, d), lambda i: (0, 0))
    return pl.pallas_call(
        _pool_kernel,
        out_shape=jax.ShapeDtypeStruct((r, d), F32),
        grid=(r // TILE,),
        in_specs=[row, pl.BlockSpec((1, MOD_ROWS, d), lambda i: (i, 0, 0)),
                  pl.BlockSpec((1, ng, TILE, TILE), lambda i: ((i % tps) // (tps - 1), 0, 0, 0)),
                  pl.BlockSpec((ng, gw, gw), lambda i: (0, 0, 0)), vec, vec],
        out_specs=row,
        compiler_params=_cparams(("arbitrary",)),
        name="pool_mixer",
    )(h, mod, mats, pool_w.astype(BF16), pool_b.reshape(1, d), pool_scale.reshape(1, d))


def _dft_tables(n):
    idx = np.arange(n)
    ang = 2.0 * np.pi * ((idx[:, None] * idx[None, :]) % n) / n
    scale = 1.0 / math.sqrt(n)
    return (np.cos(ang) * scale).astype(np.float32), (np.sin(ang) * scale).astype(np.float32)


def _fnet_fold_kernel(cc_ref, sc_ref, w_ref, o_ref):
    w = w_ref[...]
    d = w.shape[-1]
    o_ref[:, :d] = _dot3(cc_ref[...], w).astype(o_ref.dtype)
    o_ref[:, d:] = (-_dot3(sc_ref[...], w)).astype(o_ref.dtype)


def _fnet_fold(fnet_w):
    d = fnet_w.shape[0]
    gw = d // FNET_GROUPS
    cc, sc = _dft_tables(gw)
    sq = pl.BlockSpec((gw, gw), lambda i: (0, 0))
    return pl.pallas_call(
        _fnet_fold_kernel,
        out_shape=jax.ShapeDtypeStruct((d, 2 * d), BF16),
        grid=(FNET_GROUPS,),
        in_specs=[sq, sq, pl.BlockSpec((gw, d), lambda i: (i, 0))],
        out_specs=pl.BlockSpec((gw, 2 * d), lambda i: (i, 0)),
        compiler_params=_cparams(("arbitrary",)),
        name="fnet_fold",
    )(jnp.asarray(cc), jnp.asarray(sc), fnet_w)


def _fnet_proj_kernel(h_ref, mod_ref, g_ref, v_ref):
    for j in range(h_ref.shape[0] // TILE):
        rows = slice(j * TILE, (j + 1) * TILE)
        mod = mod_ref[j]
        u = _normmod(h_ref[rows, :], mod[GS_M:GS_M + 1], mod[SH_M:SH_M + 1])
        v_ref[rows, :] = _dot(u.astype(BF16), g_ref[...]).astype(v_ref.dtype)


def _fnet_proj(h, mod, g):
    r, d = h.shape
    sub = ROW_BLOCK // TILE
    return pl.pallas_call(
        _fnet_proj_kernel,
        out_shape=jax.ShapeDtypeStruct((r, 2 * d), BF16),
        grid=(r // ROW_BLOCK,),
        in_specs=[pl.BlockSpec((ROW_BLOCK, d), lambda i: (i, 0)),
                  pl.BlockSpec((sub, MOD_ROWS, d), lambda i: (i, 0, 0)),
                  pl.BlockSpec((d, 2 * d), lambda i: (0, 0))],
        out_specs=pl.BlockSpec((ROW_BLOCK, 2 * d), lambda i: (i, 0)),
        compiler_params=_cparams(("arbitrary",)),
        name="fnet_proj",
    )(h, mod, g)


def _dft_matrix(n):
    n1 = int(round(math.sqrt(n)))
    assert n1 * n1 == n
    hi = jnp.arange(n1, dtype=jnp.int32)[:, None]
    k = jnp.arange(n, dtype=jnp.int32)[None, :]
    ang_a = (2.0 * math.pi / n1) * ((hi * k) % n1).astype(F32)
    ang_b = (2.0 * math.pi / n) * ((hi * k) % n).astype(F32)
    ca, sa, cb, sb = jnp.cos(ang_a), jnp.sin(ang_a), jnp.cos(ang_b), jnp.sin(ang_b)
    scale = 1.0 / math.sqrt(n)
    c = (ca[:, None] * cb[None] - sa[:, None] * sb[None]).reshape(n, n) * scale
    s = (sa[:, None] * cb[None] + ca[:, None] * sb[None]).reshape(n, n) * scale
    return jnp.concatenate([c, s], axis=1).astype(BF16)


def _fnet_lat_kernel(f_ref, v_ref, h_ref, mod_ref, b_ref, o_ref, acc_ref):
    k = pl.program_id(2)

    @pl.when(k == 0)
    def _():
        acc_ref[...] = jnp.zeros_like(acc_ref)

    acc_ref[...] += _dot(f_ref[...], v_ref[0])

    @pl.when(k == pl.num_programs(2) - 1)
    def _():
        for j in range(h_ref.shape[1] // TILE):
            rows = slice(j * TILE, (j + 1) * TILE)
            o_ref[0, rows, :] = h_ref[0, rows, :] + mod_ref[0, j, G_M:G_M + 1] * (acc_ref[rows, :] + b_ref[...])


def _fnet_lat(dft, v3, h3, mod4, fnet_b, seq):
    bsz, tot, d = h3.shape
    tm, tk = ROW_BLOCK, min(2048, seq)
    kh = seq // tk
    sub = tm // TILE
    return pl.pallas_call(
        _fnet_lat_kernel,
        out_shape=jax.ShapeDtypeStruct((bsz, tot, d), F32),
        grid=(bsz, seq // tm, 2 * kh),
        in_specs=[pl.BlockSpec((tm, tk), lambda b, m, k: (m, k)),
                  pl.BlockSpec((1, tk, d), lambda b, m, k: (b, k % kh, k // kh)),
                  pl.BlockSpec((1, tm, d), lambda b, m, k: (b, m, 0)),
                  pl.BlockSpec((1, sub, MOD_ROWS, d), lambda b, m, k: (b, m, 0, 0)),
                  pl.BlockSpec((1, d), lambda b, m, k: (0, 0))],
        out_specs=pl.BlockSpec((1, tm, d), lambda b, m, k: (b, m, 0)),
        scratch_shapes=[pltpu.VMEM((tm, d), F32)],
        compiler_params=_cparams(("arbitrary", "arbitrary", "arbitrary")),
        name="fnet_dft_latent",
    )(dft, v3, h3, mod4, fnet_b.reshape(1, d))


def _fnet_ctx_kernel(f_ref, v_ref, h_ref, mod_ref, b_ref, prev_ref, o_ref):
    del prev_ref
    n = f_ref.shape[0]
    d = h_ref.shape[-1]
    acc = _dot(f_ref[:, :n], v_ref[0, :, :d]) + _dot(f_ref[:, n:], v_ref[0, :, d:])
    o_ref[0] = h_ref[0] + mod_ref[0, 0, G_M:G_M + 1] * (acc + b_ref[...])


def _fnet_ctx(dftc, v3, h3, mod4, fnet_b, partial, seq):
    bsz, tot, d = h3.shape
    n = tot - seq
    blk = seq // n
    return pl.pallas_call(
        _fnet_ctx_kernel,
        out_shape=jax.ShapeDtypeStruct((bsz, tot, d), F32),
        grid=(bsz,),
        in_specs=[pl.BlockSpec((n, 2 * n), lambda b: (0, 0)),
                  pl.BlockSpec((1, n, 2 * d), lambda b: (b, blk, 0)),
                  pl.BlockSpec((1, n, d), lambda b: (b, blk, 0)),
                  pl.BlockSpec((1, 1, MOD_ROWS, d), lambda b: (b, blk, 0, 0)),
                  pl.BlockSpec((1, d), lambda b: (0, 0)),
                  pl.BlockSpec(memory_space=pl.ANY)],
        out_specs=pl.BlockSpec((1, n, d), lambda b: (b, blk, 0)),
        input_output_aliases={5: 0},
        compiler_params=_cparams(("arbitrary",)),
        name="fnet_dft_context",
    )(dftc, v3, h3, mod4, fnet_b.reshape(1, d), partial)


def _silu(x):
    return x * (1.0 / (1.0 + jnp.exp(-x)))


def _ffn_kernel(h_ref, mod_ref, wg_ref, wu_ref, wd_ref, o_ref, u_scr, acc_ref):
    f = pl.program_id(1)
    sub = h_ref.shape[0] // TILE

    @pl.when(f == 0)
    def _():
        for j in range(sub):
            rows = slice(j * TILE, (j + 1) * TILE)
            mod = mod_ref[j]
            u_scr[rows, :] = _normmod(h_ref[rows, :], mod[GS_F:GS_F + 1], mod[SH_F:SH_F + 1]).astype(BF16)
        acc_ref[...] = jnp.zeros_like(acc_ref)

    u = u_scr[...]
    a = _silu(_dot(u, wg_ref[...])) * _dot(u, wu_ref[...])
    acc_ref[...] += _dot(a.astype(BF16), wd_ref[...])

    @pl.when(f == pl.num_programs(1) - 1)
    def _():
        for j in range(sub):
            rows = slice(j * TILE, (j + 1) * TILE)
            o_ref[rows, :] = h_ref[rows, :] + mod_ref[j, G_F:G_F + 1] * acc_ref[rows, :]


def _ffn(h, mod, w_gate, w_up, w_down):
    r, d = h.shape
    ff = w_gate.shape[1]
    tm, tf = ROW_BLOCK, 256
    sub = tm // TILE
    return pl.pallas_call(
        _ffn_kernel,
        out_shape=jax.ShapeDtypeStruct((r, d), F32),
        grid=(r // tm, ff // tf),
        in_specs=[pl.BlockSpec((tm, d), lambda i, f: (i, 0)),
                  pl.BlockSpec((sub, MOD_ROWS, d), lambda i, f: (i, 0, 0)),
                  pl.BlockSpec((d, tf), lambda i, f: (0, f)),
                  pl.BlockSpec((d, tf), lambda i, f: (0, f)),
                  pl.BlockSpec((tf, d), lambda i, f: (f, 0))],
        out_specs=pl.BlockSpec((tm, d), lambda i, f: (i, 0)),
        scratch_shapes=[pltpu.VMEM((tm, d), BF16), pltpu.VMEM((tm, d), F32)],
        compiler_params=_cparams(("arbitrary", "arbitrary")),
        name="ffn_swiglu",
    )(h, mod, w_gate.astype(BF16), w_up.astype(BF16), w_down.astype(BF16))


MOE_TILE = 2048
MOE_CHUNK = 256


def _router_kernel(h_ref, mod_ref, r_ref, u_ref, cw_ref, slot_ref, slott_ref, cnt_ref):
    lane = lax.broadcasted_iota(jnp.int32, (TILE, LANES), 1).astype(F32)
    earlier = jnp.where(lax.broadcasted_iota(jnp.int32, (TILE, TILE), 1)
                        < lax.broadcasted_iota(jnp.int32, (TILE, TILE), 0), 1.0, 0.0).astype(BF16)
    neg = jnp.float32(-jnp.inf)
    count = jnp.zeros((1, LANES), F32)
    for j in range(h_ref.shape[0] // TILE):
        rows = slice(j * TILE, (j + 1) * TILE)
        mod = mod_ref[j]
        u = _normmod(h_ref[rows, :], mod[GS_F:GS_F + 1], mod[SH_F:SH_F + 1])
        u_ref[rows, :] = u.astype(BF16)
        logits = _dot3(u, r_ref[...])
        logits = jnp.where(lane < N_EXPERTS, logits, neg)
        m1 = jnp.max(logits, axis=-1, keepdims=True)
        i1 = jnp.min(jnp.where(logits == m1, lane, float(LANES)), axis=-1, keepdims=True)
        rest = jnp.where(lane == i1, neg, logits)
        m2 = jnp.max(rest, axis=-1, keepdims=True)
        i2 = jnp.min(jnp.where(rest == m2, lane, float(LANES)), axis=-1, keepdims=True)
        e2 = jnp.exp(m2 - m1)
        w1 = 1.0 / (1.0 + e2)
        w2 = e2 / (1.0 + e2)
        cw_ref[rows, :] = jnp.where(lane == i1, w1, jnp.where(lane == i2, w2, 0.0))
        sel = jnp.where(lane == i1, 1.0, jnp.where(lane == i2, 1.0, 0.0))
        slot = jnp.where(sel > 0.0, _dot(earlier, sel.astype(BF16)) + count, -1.0)
        slot_ref[rows, :] = slot
        slott_ref[:, rows] = slot.T[:N_EXPERTS, :]
        count = count + jnp.sum(sel, axis=0, keepdims=True)
    cnt_ref[0] = jnp.broadcast_to(count, (N_EXPERTS, LANES))


def _router(h, mod, router):
    r, d = h.shape
    rp = jnp.zeros((d, LANES), F32).at[:, :N_EXPERTS].set(router)
    sub = MOE_TILE // TILE
    row = lambda w: pl.BlockSpec((MOE_TILE, w), lambda i: (i, 0))
    return pl.pallas_call(
        _router_kernel,
        out_shape=(jax.ShapeDtypeStruct((r, d), BF16), jax.ShapeDtypeStruct((r, LANES), F32),
                   jax.ShapeDtypeStruct((r, LANES), F32), jax.ShapeDtypeStruct((N_EXPERTS, r), F32),
                   jax.ShapeDtypeStruct((r // MOE_TILE, N_EXPERTS, LANES), F32)),
        grid=(r // MOE_TILE,),
        in_specs=[row(d),
                  pl.BlockSpec((sub, MOD_ROWS, d), lambda i: (i, 0, 0)),
                  pl.BlockSpec((d, LANES), lambda i: (0, 0))],
        out_specs=(row(d), row(LANES), row(LANES),
                   pl.BlockSpec((N_EXPERTS, MOE_TILE), lambda i: (0, i)),
                   pl.BlockSpec((1, N_EXPERTS, LANES), lambda i: (i, 0, 0))),
        compiler_params=_cparams(("arbitrary",)),
        name="moe_router",
    )(h, mod, rp)


def _moe_kernel(nch_ref, h_ref, u_ref, cw_ref, slot_ref, slott_ref, mod_ref, wg_ref, wu_ref, wd_ref, o_ref,
                xs_scr, y_scr):
    i = pl.program_id(0)
    e = pl.program_id(1)
    f = pl.program_id(2)
    last_f = pl.num_programs(2) - 1
    tile = u_ref.shape[0]
    nch = nch_ref[i * N_EXPERTS + e]
    block = 2 * MOE_CHUNK

    @pl.when((e == 0) & (f == 0))
    def _():
        o_ref[...] = jnp.zeros_like(o_ref)

    def run_expert(k):
        m = k * MOE_CHUNK
        blocks = [(r0, min(block, m - r0)) for r0 in range(0, m, block)]

        @pl.when(f == 0)
        def _():
            slot_row = slott_ref[pl.ds(e, 1), :]
            for r0, rn in blocks:
                rid = (lax.broadcasted_iota(jnp.int32, (rn, 1), 0) + r0).astype(F32)
                pick = jnp.where(slot_row == rid, 1.0, 0.0).astype(BF16)
                xs_scr[r0:r0 + rn, :] = _dot(pick, u_ref[...]).astype(BF16)

        for r0, rn in blocks:
            xs = xs_scr[r0:r0 + rn, :]
            a = _silu(_dot(xs, wg_ref[0])) * _dot(xs, wu_ref[0])
            part = _dot(a.astype(BF16), wd_ref[0])

            @pl.when(f == 0)
            def _():
                y_scr[r0:r0 + rn, :] = part

            @pl.when(f > 0)
            def _():
                y_scr[r0:r0 + rn, :] += part

        @pl.when(f == last_f)
        def _():
            mine = lax.broadcasted_iota(jnp.int32, (tile, LANES), 1) == e
            slot_col = jnp.sum(jnp.where(mine, slot_ref[...], 0.0), axis=-1, keepdims=True)
            cw_col = jnp.sum(jnp.where(mine, cw_ref[...], 0.0), axis=-1, keepdims=True)
            acc = None
            for c in range(k):
                cid = (lax.broadcasted_iota(jnp.int32, (1, MOE_CHUNK), 1) + c * MOE_CHUNK).astype(F32)
                put = jnp.where(slot_col == cid, 1.0, 0.0).astype(BF16)
                got = _dot(put, y_scr[c * MOE_CHUNK:(c + 1) * MOE_CHUNK, :].astype(BF16))
                acc = got if acc is None else acc + got
            o_ref[...] += cw_col * acc

    for k in range(1, tile // MOE_CHUNK + 1):
        pl.when(nch == k)(functools.partial(run_expert, k))

    @pl.when((e == pl.num_programs(1) - 1) & (f == last_f))
    def _():
        for j in range(tile // TILE):
            rows = slice(j * TILE, (j + 1) * TILE)
            o_ref[rows, :] = h_ref[rows, :] + mod_ref[j, G_F:G_F + 1] * o_ref[rows, :]


def _moe(h, u, cw, slot, slott, counts, mod, w_gate, w_up, w_down):
    r, d = h.shape
    ne, _, ff = w_gate.shape
    tm, tf = MOE_TILE, 512
    sub = tm // TILE
    nch = ((counts[:, 0, :ne].astype(jnp.int32) + (MOE_CHUNK - 1)) // MOE_CHUNK).reshape(-1)
    once = pl.Buffered(1)
    row = lambda w: pl.BlockSpec((tm, w), lambda i, e, f, n: (i, 0), pipeline_mode=once)
    return pl.pallas_call(
        _moe_kernel,
        out_shape=jax.ShapeDtypeStruct((r, d), F32),
        grid_spec=pltpu.PrefetchScalarGridSpec(
            num_scalar_prefetch=1,
            grid=(r // tm, ne, ff // tf),
            in_specs=[row(d), row(d), row(LANES), row(LANES),
                      pl.BlockSpec((ne, tm), lambda i, e, f, n: (0, i), pipeline_mode=once),
                      pl.BlockSpec((sub, MOD_ROWS, d), lambda i, e, f, n: (i, 0, 0)),
                      pl.BlockSpec((1, d, tf), lambda i, e, f, n: (e, 0, f)),
                      pl.BlockSpec((1, d, tf), lambda i, e, f, n: (e, 0, f)),
                      pl.BlockSpec((1, tf, d), lambda i, e, f, n: (e, f, 0))],
            out_specs=pl.BlockSpec((tm, d), lambda i, e, f, n: (i, 0), pipeline_mode=once),
            scratch_shapes=[pltpu.VMEM((tm, d), BF16), pltpu.VMEM((tm, d), F32)]),
        compiler_params=_cparams(("arbitrary", "arbitrary", "arbitrary")),
        name="moe_experts",
    )(nch, h, u, cw, slot, slott, mod, w_gate.astype(BF16), w_up.astype(BF16), w_down.astype(BF16))


def _final_kernel(h_ref, g_ref, o_ref):
    h = h_ref[0]
    inv = lax.rsqrt(jnp.mean(h * h, axis=-1, keepdims=True) + EPS)
    o_ref[0] = h * inv * g_ref[...]


def _final_norm(h3, gain, seq):
    bsz, _, d = h3.shape
    return pl.pallas_call(
        _final_kernel,
        out_shape=jax.ShapeDtypeStruct((bsz, seq, d), F32),
        grid=(bsz, seq // TILE),
        in_specs=[pl.BlockSpec((1, TILE, d), lambda b, t: (b, t, 0)),
                  pl.BlockSpec((1, d), lambda b, t: (0, 0))],
        out_specs=pl.BlockSpec((1, TILE, d), lambda b, t: (b, t, 0)),
        compiler_params=_cparams(("arbitrary", "arbitrary")),
        name="final_norm",
    )(h3, gain.reshape(1, d))


def _tile_mods(mods, gain_mix, gain_ffn, bsz, lat_tiles):
    sh_m, sc_m, g_m, sh_f, sc_f, g_f = (mods[:, i] for i in range(N_MOD))
    rows = jnp.stack([gain_mix * (1.0 + sc_m), sh_m, g_m, gain_ffn * (1.0 + sc_f), sh_f, g_f,
                      jnp.zeros_like(g_f), jnp.zeros_like(g_f)], axis=1)
    lat = jnp.broadcast_to(rows[:bsz, None], (bsz, lat_tiles) + rows.shape[1:])
    ctx = jnp.broadcast_to(rows[bsz:, None], (bsz, 1) + rows.shape[1:])
    return jnp.concatenate([lat, ctx], axis=1).reshape(bsz * (lat_tiles + 1), MOD_ROWS, rows.shape[-1])


def kernel(x, c, ctx, c_ctx, ada_w, ada_b, norm_mix, norm_ffn, norm_final, s5_lambda_re, s5_lambda_im, s5_log_step, s5_b_re, s5_b_im, s5_c_re, s5_c_im, s5_d, s5_glu_w, s5_glu_b, pool_w, pool_b, pool_scale, fnet_w, fnet_b, ffn_w_gate, ffn_w_up, ffn_w_down, moe_router, moe_w_gate, moe_w_up, moe_w_down):
    bsz, seq, d = x.shape
    ctx_len = ctx.shape[1]
    depth = ada_w.shape[0]
    assert ctx_len == TILE and seq % ROW_BLOCK == 0 and d % LANES == 0
    tot = seq + ctx_len
    lat_tiles = seq // TILE
    tps = lat_tiles + 1
    r = bsz * tot
    assert r % ROW_BLOCK == 0

    cond = jnp.zeros((16, d), F32).at[:bsz].set(c).at[bsz].set(c_ctx)
    mods_all = _ada_all(cond, ada_w, ada_b)[:, :bsz + 1].reshape(depth, bsz + 1, N_MOD, d)

    h = jnp.concatenate([x, ctx], axis=1).reshape(r, d)
    for i in range(depth):
        kind, j = i % 3, i // 3
        mod = _tile_mods(mods_all[i], norm_mix[i], norm_ffn[i], bsz, lat_tiles)
        if kind == 0:
            u = _s5_pre(h, mod)
            kc, ec, cl, a = _s5_params(s5_lambda_re[j], s5_lambda_im[j], s5_log_step[j],
                                       s5_b_re[j], s5_b_im[j], s5_c_re[j], s5_c_im[j])
            y_lat, y_ctx = _s5_scan(u.reshape(bsz, tot, d), kc, ec, cl, a, seq, ctx_len)
            h = _s5_glu(h, u, y_lat, y_ctx, mod, s5_d[j], s5_glu_w[j], s5_glu_b[j], seq)
        elif kind == 1:
            h = _pool(h, mod, pool_w[j], pool_b[j], pool_scale[j], seq)
        else:
            g = _fnet_fold(fnet_w[j])
            v3 = _fnet_proj(h, mod, g).reshape(bsz, tot, 2 * d)
            h3 = h.reshape(bsz, tot, d)
            mod4 = mod.reshape(bsz, tps, MOD_ROWS, d)
            part = _fnet_lat(_dft_matrix(seq), v3, h3, mod4, fnet_b[j], seq)
            cc, sc = _dft_tables(ctx_len)
            dftc = jnp.asarray(np.concatenate([cc, sc], axis=1), BF16)
            h = _fnet_ctx(dftc, v3, h3, mod4, fnet_b[j], part, seq).reshape(r, d)
        kk = i // 2
        if i % 2 == 0:
            h = _ffn(h, mod, ffn_w_gate[kk], ffn_w_up[kk], ffn_w_down[kk])
        else:
            ub, cw, slot, slott, counts = _router(h, mod, moe_router[kk])
            h = _moe(h, ub, cw, slot, slott, counts, mod, moe_w_gate[kk], moe_w_up[kk], moe_w_down[kk])
    return _final_norm(h.reshape(bsz, tot, d), norm_final, seq)
```

```python
import functools
import math

import numpy as np
import jax
import jax.numpy as jnp
from jax import lax
from jax.experimental import pallas as pl
from jax.experimental.pallas import tpu as pltpu

F32 = jnp.float32
BF16 = jnp.bfloat16

EPS = 1e-6
GRID_W = 64
N_MOD = 6
S5_GROUP = 16
S5_STATE = 64
S5_CHUNK = 16
POOL_WINDOWS = (2, 4, 8, 16)
FNET_GROUPS = 4
N_EXPERTS = 8

LANES = 128
TILE = 256
ROW_BLOCK = 1024
VMEM_LIMIT = 56 * 1024 * 1024


def _cparams(sem, vmem=VMEM_LIMIT):
    return pltpu.CompilerParams(dimension_semantics=sem, vmem_limit_bytes=vmem)


def _dot(a, b):
    return jnp.dot(a, b, preferred_element_type=F32)


def _split(a):
    hi = a.astype(BF16)
    lo = (a - hi.astype(F32)).astype(BF16)
    return hi, lo


def _dot3(a, b):
    a_hi, a_lo = _split(a)
    b_hi, b_lo = _split(b)
    return _dot(a_hi, b_hi) + (_dot(a_lo, b_hi) + _dot(a_hi, b_lo))


def _normmod(h, gs, sh):
    inv = lax.rsqrt(jnp.mean(h * h, axis=-1, keepdims=True) + EPS)
    return h * inv * gs + sh


GS_M, SH_M, G_M, GS_F, SH_F, G_F = range(6)
MOD_ROWS = 8


def _ada_kernel(c_ref, w_ref, b_ref, o_ref):
    c = c_ref[...]
    s = c * (1.0 / (1.0 + jnp.exp(-c)))
    o_ref[0] = _dot3(s, w_ref[0]) + b_ref[0]


def _ada_all(cond, ada_w, ada_b):
    depth, d, n = ada_w.shape
    rows = cond.shape[0]
    tn = 1536
    return pl.pallas_call(
        _ada_kernel,
        out_shape=jax.ShapeDtypeStruct((depth, rows, n), F32),
        grid=(depth, n // tn),
        in_specs=[
            pl.BlockSpec((rows, d), lambda i, j: (0, 0)),
            pl.BlockSpec((1, d, tn), lambda i, j: (i, 0, j)),
            pl.BlockSpec((1, 1, tn), lambda i, j: (i, 0, j)),
        ],
        out_specs=pl.BlockSpec((1, rows, tn), lambda i, j: (i, 0, j)),
        compiler_params=_cparams(("arbitrary", "arbitrary")),
        name="ada_params",
    )(cond, ada_w, ada_b.reshape(depth, 1, n))


def _s5_pre_kernel(h_ref, mod_ref, u_ref):
    mod = mod_ref[0]
    u_ref[...] = _normmod(h_ref[...], mod[GS_M:GS_M + 1], mod[SH_M:SH_M + 1])


def _s5_pre(h, mod):
    r, d = h.shape
    return pl.pallas_call(
        _s5_pre_kernel,
        out_shape=jax.ShapeDtypeStruct((r, d), F32),
        grid=(r // TILE,),
        in_specs=[pl.BlockSpec((TILE, d), lambda i: (i, 0)),
                  pl.BlockSpec((1, MOD_ROWS, d), lambda i: (i, 0, 0))],
        out_specs=pl.BlockSpec((TILE, d), lambda i: (i, 0)),
        compiler_params=_cparams(("arbitrary",)),
        name="s5_pre",
    )(h, mod)


def _s5_params(lam_re, lam_im, log_step, b_re, b_im, c_re, c_im):
    hp = lax.Precision.HIGHEST
    L = S5_CHUNK
    n_dir, g, p = lam_re.shape
    gb = LANES // S5_GROUP
    nb = g // gb
    hch = S5_GROUP
    step = jnp.exp(log_step)[..., None]
    ar, ai = lam_re * step, lam_im * step
    k = jnp.arange(L + 1, dtype=F32)[None, :, None, None]
    mag = jnp.exp(ar[:, None] * k)
    ang = ai[:, None] * k
    pr, pi = mag * jnp.cos(ang), mag * jnp.sin(ang)
    lbr, lbi = pr[:, 1] - 1.0, pi[:, 1]
    den = lam_re * lam_re + lam_im * lam_im
    qr = (lbr * lam_re + lbi * lam_im) / den
    qi = (lbi * lam_re - lbr * lam_im) / den
    bbr = qr[..., None] * b_re - qi[..., None] * b_im
    bbi = qr[..., None] * b_im + qi[..., None] * b_re
    er = pr[..., None] * bbr[:, None] - pi[..., None] * bbi[:, None]
    ei = pr[..., None] * bbi[:, None] + pi[..., None] * bbr[:, None]
    kmat = (jnp.einsum('dgop,dkgpi->dgiko', c_re, er[:, :L], precision=hp)
            - jnp.einsum('dgop,dkgpi->dgiko', c_im, ei[:, :L], precision=hp))
    kc = kmat.reshape(n_dir, nb, gb * hch, L * hch)
    es = jnp.stack([er[:, L - 1::-1], ei[:, L - 1::-1]], axis=0)
    ec = jnp.transpose(es, (1, 3, 2, 5, 0, 4)).reshape(n_dir, nb, gb, L, hch, 2, p)
    ec = jnp.transpose(ec, (0, 1, 3, 2, 4, 5, 6)).reshape(n_dir, nb, L, gb * hch, 2 * p)
    clr = c_re[:, None] * pr[:, 1:, :, None, :] - c_im[:, None] * pi[:, 1:, :, None, :]
    cli = c_re[:, None] * pi[:, 1:, :, None, :] + c_im[:, None] * pr[:, 1:, :, None, :]
    cs = jnp.stack([clr, -cli], axis=0)
    cl = jnp.transpose(cs, (1, 3, 0, 5, 2, 4)).reshape(n_dir, nb, gb, 2, p, L, hch)
    cl = jnp.transpose(cl, (0, 1, 3, 2, 4, 5, 6)).reshape(n_dir, nb, 2 * gb * p, L * hch)
    a = jnp.stack([pr[:, L].reshape(n_dir, nb, 1, gb * p), pi[:, L].reshape(n_dir, nb, 1, gb * p)], axis=3)
    return kc.astype(BF16), ec.astype(BF16), cl.astype(BF16), a.reshape(n_dir, nb, 1, 2 * gb * p)


def _replicators():
    gb = LANES // S5_GROUP
    r16 = np.zeros((S5_CHUNK, S5_GROUP, S5_CHUNK, gb, S5_GROUP), np.float32)
    for k in range(S5_CHUNK):
        for o in range(S5_GROUP):
            r16[k, o, k, :, o] = 1.0
    r64 = np.zeros((2, S5_STATE, 2, gb, S5_STATE), np.float32)
    for part in range(2):
        for q in range(S5_STATE):
            r64[part, q, part, :, q] = 1.0
    return (jnp.asarray(r16.reshape(S5_CHUNK * S5_GROUP, S5_CHUNK * LANES), BF16),
            jnp.asarray(r64.reshape(2 * S5_STATE, 2 * gb * S5_STATE), BF16))


def _s5_scan_kernel(ul_ref, uc_ref, kc_ref, ec_ref, cl_ref, r16_ref, r64_ref, a_ref, yl_ref, yc_ref,
                    h_scr, s_scr, hs_scr, wts_scr, wy_scr):
    d = pl.program_id(1)
    t = pl.program_id(2)
    L = S5_CHUNK
    nstate = a_ref.shape[-1]
    half = nstate // 2
    nt = L * LANES
    blk = lambda i: slice(i * LANES, (i + 1) * LANES)

    def build_weights():
        row_gi = lax.broadcasted_iota(jnp.int32, (LANES, LANES), 0) // S5_GROUP
        lane_go = lax.broadcasted_iota(jnp.int32, (LANES, LANES), 1) // S5_GROUP
        toep = _dot(kc_ref[0, 0], r16_ref[...])
        zero = jnp.zeros((LANES, LANES), BF16)
        for lag in range(L):
            piece = jnp.where(row_gi == lane_go, toep[:, blk(lag)], 0.0).astype(BF16)
            for s in range(L - lag):
                wts_scr[blk(s), blk(s + lag)] = piece
        for s in range(1, L):
            for tt in range(s):
                wts_scr[blk(s), blk(tt)] = zero
        row_gi_w = lax.broadcasted_iota(jnp.int32, (LANES, nstate), 0) // S5_GROUP
        lane_gp = (lax.broadcasted_iota(jnp.int32, (LANES, nstate), 1) % half) // S5_STATE
        for s in range(L):
            e = _dot(ec_ref[0, 0, s], r64_ref[...])
            wts_scr[blk(s), nt:] = jnp.where(row_gi_w == lane_gp, e, 0.0).astype(BF16)
        row_gp = (lax.broadcasted_iota(jnp.int32, (nstate, LANES), 0) % half) // S5_STATE
        lane_go_t = lax.broadcasted_iota(jnp.int32, (nstate, LANES), 1) // S5_GROUP
        for tt in range(L):
            w = _dot(cl_ref[0, 0], r16_ref[:, blk(tt)])
            wy_scr[:, blk(tt)] = jnp.where(row_gp == lane_go_t, w, 0.0).astype(BF16)

    def process(x_ref, y_ref):
        bsz, tok, _ = x_ref.shape
        nc = tok // L
        m = bsz * nc
        flip = lambda i, n: i + d * (n - 1 - 2 * i)
        u = jnp.concatenate(
            [x_ref[:, pl.ds(flip(s, L), nc, stride=L), :].reshape(m, LANES) for s in range(L)], axis=1).astype(BF16)
        s_all = _dot(u, wts_scr[:, nt:])
        nq = nstate // LANES
        nh = nq // 2
        for q in range(nq):
            s_scr[q, 0:m, :] = s_all[:, blk(q)]
        a = a_ref[0, 0]
        a_re = [jnp.broadcast_to(a[:, blk(q)], (bsz, LANES)) for q in range(nh)]
        a_im = [jnp.broadcast_to(a[:, blk(nh + q)], (bsz, LANES)) for q in range(nh)]
        h_re = [h_scr[q] for q in range(nh)]
        h_im = [h_scr[nh + q] for q in range(nh)]
        for c in range(nc):
            rows = pl.ds(flip(c, nc), bsz, stride=nc)
            for q in range(nh):
                hs_scr.at[q][rows, :] = h_re[q]
                hs_scr.at[nh + q][rows, :] = h_im[q]
                s_re = s_scr.at[q][rows, :]
                s_im = s_scr.at[nh + q][rows, :]
                h_re[q], h_im[q] = (a_re[q] * h_re[q] - a_im[q] * h_im[q] + s_re,
                                    a_re[q] * h_im[q] + a_im[q] * h_re[q] + s_im)
        for q in range(nh):
            h_scr[q] = h_re[q]
            h_scr[nh + q] = h_im[q]
        hs = jnp.concatenate([hs_scr[q, 0:m, :] for q in range(nq)], axis=1).astype(BF16)
        y = _dot(u, wts_scr[:, :nt]) + _dot(hs, wy_scr[...])
        for tt in range(L):
            y_ref[:, pl.ds(flip(tt, L), nc, stride=L), :] = y[:, blk(tt)].reshape(bsz, nc, LANES)

    @pl.when(t == 0)
    def _():
        build_weights()
        h_scr[...] = jnp.zeros_like(h_scr)
        process(uc_ref, yc_ref)

    @pl.when(t > 0)
    def _():
        process(ul_ref, yl_ref)


def _s5_scan(u3, kc, ec, cl, a, seq, ctx_len):
    bsz, _, d = u3.shape
    tl = 1024
    ntl = seq // tl
    nb = d // LANES
    n_dir = 2
    ctx_blk = seq // ctx_len
    nstate = a.shape[-1]
    nt = S5_CHUNK * LANES
    r16, r64 = _replicators()

    def lat_idx(dd, t):
        i = jnp.maximum(t - 1, 0)
        return jnp.where(dd == 0, i, ntl - 1 - i)

    per_block = lambda arr: pl.BlockSpec((1, 1) + arr.shape[2:], lambda j, dd, t: (dd, j) + (0,) * (arr.ndim - 2))
    const = lambda arr: pl.BlockSpec(arr.shape, lambda j, dd, t: (0,) * arr.ndim)
    return pl.pallas_call(
        _s5_scan_kernel,
        out_shape=(jax.ShapeDtypeStruct((n_dir, bsz, seq, d), F32),
                   jax.ShapeDtypeStruct((n_dir, bsz, ctx_len, d), F32)),
        grid=(nb, n_dir, ntl + 1),
        in_specs=[
            pl.BlockSpec((bsz, tl, LANES), lambda j, dd, t: (0, lat_idx(dd, t), j)),
            pl.BlockSpec((bsz, ctx_len, LANES), lambda j, dd, t: (0, ctx_blk, j)),
            per_block(kc), per_block(ec), per_block(cl), const(r16), const(r64), per_block(a),
        ],
        out_specs=(
            pl.BlockSpec((None, bsz, tl, LANES), lambda j, dd, t: (dd, 0, lat_idx(dd, t), j)),
            pl.BlockSpec((None, bsz, ctx_len, LANES), lambda j, dd, t: (dd, 0, 0, j)),
        ),
        scratch_shapes=[
            pltpu.VMEM((nstate // LANES, bsz, LANES), F32),
            pltpu.VMEM((nstate // LANES, bsz * tl // S5_CHUNK, LANES), F32),
            pltpu.VMEM((nstate // LANES, bsz * tl // S5_CHUNK, LANES), F32),
            pltpu.VMEM((nt, nt + nstate), BF16),
            pltpu.VMEM((nstate, nt), BF16),
        ],
        compiler_params=_cparams(("arbitrary", "arbitrary", "arbitrary")),
        name="s5_scan",
    )(u3, u3, kc, ec, cl, r16, r64, a)


def _gelu_tanh(x):
    return 0.5 * x * (1.0 + jnp.tanh(math.sqrt(2.0 / math.pi) * (x + 0.044715 * (x * x * x))))


def _s5_glu_kernel(tiles_per_seq, h_ref, u_ref, ylf_ref, ylr_ref, ycf_ref, ycr_ref, mod_ref, dvec_ref,
                   w_ref, b_ref, o_ref):
    is_ctx = (pl.program_id(0) % tiles_per_seq) == tiles_per_seq - 1
    y_scan = jnp.where(is_ctx, ycf_ref[0, 0] + ycr_ref[0, 0], ylf_ref[0, 0] + ylr_ref[0, 0])
    y = dvec_ref[...] * u_ref[...] + y_scan
    z = _gelu_tanh(y)
    gate = 1.0 / (1.0 + jnp.exp(-(_dot(z.astype(BF16), w_ref[...]) + b_ref[...])))
    o_ref[...] = h_ref[...] + mod_ref[0, G_M:G_M + 1] * (z * gate)


def _s5_glu(h, u, y_lat, y_ctx, mod, dvec, glu_w, glu_b, seq):
    r, d = h.shape
    lat_tiles = seq // TILE
    tps = lat_tiles + 1

    def lat_map(dd):
        return lambda i: (dd, i // tps, jnp.minimum(i % tps, lat_tiles - 1), 0)

    def ctx_map(dd):
        return lambda i: (dd, i // tps, 0, 0)

    row = pl.BlockSpec((TILE, d), lambda i: (i, 0))
    vec = pl.BlockSpec((1, d), lambda i: (0, 0))
    return pl.pallas_call(
        functools.partial(_s5_glu_kernel, tps),
        out_shape=jax.ShapeDtypeStruct((r, d), F32),
        grid=(r // TILE,),
        in_specs=[row, row,
                  pl.BlockSpec((1, 1, TILE, d), lat_map(0)), pl.BlockSpec((1, 1, TILE, d), lat_map(1)),
                  pl.BlockSpec((1, 1, TILE, d), ctx_map(0)), pl.BlockSpec((1, 1, TILE, d), ctx_map(1)),
                  pl.BlockSpec((1, MOD_ROWS, d), lambda i: (i, 0, 0)),
                  vec, pl.BlockSpec((d, d), lambda i: (0, 0)), vec],
        out_specs=row,
        compiler_params=_cparams(("arbitrary",)),
        name="s5_glu",
    )(h, u, y_lat, y_lat, y_ctx, y_ctx, mod, dvec.reshape(1, d), glu_w.astype(BF16), glu_b.reshape(1, d))


def _pool_matrices(width, tile):
    mats = np.zeros((len(POOL_WINDOWS), tile, tile), np.float32)
    pos = np.arange(width)
    for g, win in enumerate(POOL_WINDOWS):
        lo = np.clip(pos - win // 2, 0, width)
        hi = np.clip(pos + win // 2, 0, width)
        for base in range(0, tile, width):
            for j in range(width):
                mats[g, base + j, base + lo[j]:base + hi[j]] = 1.0 / float(hi[j] - lo[j])
    return mats


def _pool_kernel(h_ref, mod_ref, a_ref, w_ref, b_ref, sc_ref, o_ref):
    mod = mod_ref[0]
    h = h_ref[...]
    u = _normmod(h, mod[GS_M:GS_M + 1], mod[SH_M:SH_M + 1])
    ub = u.astype(BF16)
    gw = w_ref.shape[-1]
    for g in range(w_ref.shape[0]):
        cols = slice(g * gw, (g + 1) * gw)
        res = _dot(a_ref[0, g], ub[:, cols]) - u[:, cols]
        mixed = _dot(res.astype(BF16), w_ref[g]) + b_ref[:, cols]
        o_ref[:, cols] = h[:, cols] + mod[G_M:G_M + 1, cols] * (mixed * sc_ref[:, cols])


def _pool(h, mod, pool_w, pool_b, pool_scale, seq):
    r, d = h.shape
    tps = seq // TILE + 1
    ng, gw, _ = pool_w.shape
    mats = jnp.asarray(np.stack([_pool_matrices(GRID_W, TILE), _pool_matrices(TILE, TILE)]), BF16)
    row = pl.BlockSpec((TILE, d), lambda i: (i, 0))
    vec = pl.BlockSpec((1, d), lambda i: (0, 0))
    return pl.pallas_call(
        _pool_kernel,
        out_shape=jax.ShapeDtypeStruct((r, d), F32),
        grid=(r // TILE,),
        in_specs=[row, pl.BlockSpec((1, MOD_ROWS, d), lambda i: (i, 0, 0)),
                  pl.BlockSpec((1, ng, TILE, TILE), lambda i: ((i % tps) // (tps - 1), 0, 0, 0)),
                  pl.BlockSpec((ng, gw, gw), lambda i: (0, 0, 0)), vec, vec],
        out_specs=row,
        compiler_params=_cparams(("arbitrary",)),
        name="pool_mixer",
    )(h, mod, mats, pool_w.astype(BF16), pool_b.reshape(1, d), pool_scale.reshape(1, d))


def _dft_tables(n):
    idx = np.arange(n)
    ang = 2.0 * np.pi * ((idx[:, None] * idx[None, :]) % n) / n
    scale = 1.0 / math.sqrt(n)
    return (np.cos(ang) * scale).astype(np.float32), (np.sin(ang) * scale).astype(np.float32)


def _fnet_fold_kernel(cc_ref, sc_ref, w_ref, o_ref):
    w = w_ref[...]
    d = w.shape[-1]
    o_ref[:, :d] = _dot3(cc_ref[...], w).astype(o_ref.dtype)
    o_ref[:, d:] = (-_dot3(sc_ref[...], w)).astype(o_ref.dtype)


def _fnet_fold(fnet_w):
    d = fnet_w.shape[0]
    gw = d // FNET_GROUPS
    cc, sc = _dft_tables(gw)
    sq = pl.BlockSpec((gw, gw), lambda i: (0, 0))
    return pl.pallas_call(
        _fnet_fold_kernel,
        out_shape=jax.ShapeDtypeStruct((d, 2 * d), BF16),
        grid=(FNET_GROUPS,),
        in_specs=[sq, sq, pl.BlockSpec((gw, d), lambda i: (i, 0))],
        out_specs=pl.BlockSpec((gw, 2 * d), lambda i: (i, 0)),
        compiler_params=_cparams(("arbitrary",)),
        name="fnet_fold",
    )(jnp.asarray(cc), jnp.asarray(sc), fnet_w)


def _fnet_proj_kernel(h_ref, mod_ref, g_ref, v_ref):
    for j in range(h_ref.shape[0] // TILE):
        rows = slice(j * TILE, (j + 1) * TILE)
        mod = mod_ref[j]
        u = _normmod(h_ref[rows, :], mod[GS_M:GS_M + 1], mod[SH_M:SH_M + 1])
        v_ref[rows, :] = _dot(u.astype(BF16), g_ref[...]).astype(v_ref.dtype)


def _fnet_proj(h, mod, g):
    r, d = h.shape
    sub = ROW_BLOCK // TILE
    return pl.pallas_call(
        _fnet_proj_kernel,
        out_shape=jax.ShapeDtypeStruct((r, 2 * d), BF16),
        grid=(r // ROW_BLOCK,),
        in_specs=[pl.BlockSpec((ROW_BLOCK, d), lambda i: (i, 0)),
                  pl.BlockSpec((sub, MOD_ROWS, d), lambda i: (i, 0, 0)),
                  pl.BlockSpec((d, 2 * d), lambda i: (0, 0))],
        out_specs=pl.BlockSpec((ROW_BLOCK, 2 * d), lambda i: (i, 0)),
        compiler_params=_cparams(("arbitrary",)),
        name="fnet_proj",
    )(h, mod, g)


def _dft_matrix(n):
    n1 = int(round(math.sqrt(n)))
    assert n1 * n1 == n
    hi = jnp.arange(n1, dtype=jnp.int32)[:, None]
    k = jnp.arange(n, dtype=jnp.int32)[None, :]
    ang_a = (2.0 * math.pi / n1) * ((hi * k) % n1).astype(F32)
    ang_b = (2.0 * math.pi / n) * ((hi * k) % n).astype(F32)
    ca, sa, cb, sb = jnp.cos(ang_a), jnp.sin(ang_a), jnp.cos(ang_b), jnp.sin(ang_b)
    scale = 1.0 / math.sqrt(n)
    c = (ca[:, None] * cb[None] - sa[:, None] * sb[None]).reshape(n, n) * scale
    s = (sa[:, None] * cb[None] + ca[:, None] * sb[None]).reshape(n, n) * scale
    return jnp.concatenate([c, s], axis=1).astype(BF16)


def _fnet_lat_kernel(f_ref, v_ref, h_ref, mod_ref, b_ref, o_ref, acc_ref):
    k = pl.program_id(2)

    @pl.when(k == 0)
    def _():
        acc_ref[...] = jnp.zeros_like(acc_ref)

    acc_ref[...] += _dot(f_ref[...], v_ref[0])

    @pl.when(k == pl.num_programs(2) - 1)
    def _():
        for j in range(h_ref.shape[1] // TILE):
            rows = slice(j * TILE, (j + 1) * TILE)
            o_ref[0, rows, :] = h_ref[0, rows, :] + mod_ref[0, j, G_M:G_M + 1] * (acc_ref[rows, :] + b_ref[...])


def _fnet_lat(dft, v3, h3, mod4, fnet_b, seq):
    bsz, tot, d = h3.shape
    tm, tk = ROW_BLOCK, min(2048, seq)
    kh = seq // tk
    sub = tm // TILE
    return pl.pallas_call(
        _fnet_lat_kernel,
        out_shape=jax.ShapeDtypeStruct((bsz, tot, d), F32),
        grid=(bsz, seq // tm, 2 * kh),
        in_specs=[pl.BlockSpec((tm, tk), lambda b, m, k: (m, k)),
                  pl.BlockSpec((1, tk, d), lambda b, m, k: (b, k % kh, k // kh)),
                  pl.BlockSpec((1, tm, d), lambda b, m, k: (b, m, 0)),
                  pl.BlockSpec((1, sub, MOD_ROWS, d), lambda b, m, k: (b, m, 0, 0)),
                  pl.BlockSpec((1, d), lambda b, m, k: (0, 0))],
        out_specs=pl.BlockSpec((1, tm, d), lambda b, m, k: (b, m, 0)),
        scratch_shapes=[pltpu.VMEM((tm, d), F32)],
        compiler_params=_cparams(("arbitrary", "arbitrary", "arbitrary")),
        name="fnet_dft_latent",
    )(dft, v3, h3, mod4, fnet_b.reshape(1, d))


def _fnet_ctx_kernel(f_ref, v_ref, h_ref, mod_ref, b_ref, prev_ref, o_ref):
    del prev_ref
    n = f_ref.shape[0]
    d = h_ref.shape[-1]
    acc = _dot(f_ref[:, :n], v_ref[0, :, :d]) + _dot(f_ref[:, n:], v_ref[0, :, d:])
    o_ref[0] = h_ref[0] + mod_ref[0, 0, G_M:G_M + 1] * (acc + b_ref[...])


def _fnet_ctx(dftc, v3, h3, mod4, fnet_b, partial, seq):
    bsz, tot, d = h3.shape
    n = tot - seq
    blk = seq // n
    return pl.pallas_call(
        _fnet_ctx_kernel,
        out_shape=jax.ShapeDtypeStruct((bsz, tot, d), F32),
        grid=(bsz,),
        in_specs=[pl.BlockSpec((n, 2 * n), lambda b: (0, 0)),
                  pl.BlockSpec((1, n, 2 * d), lambda b: (b, blk, 0)),
                  pl.BlockSpec((1, n, d), lambda b: (b, blk, 0)),
                  pl.BlockSpec((1, 1, MOD_ROWS, d), lambda b: (b, blk, 0, 0)),
                  pl.BlockSpec((1, d), lambda b: (0, 0)),
                  pl.BlockSpec(memory_space=pl.ANY)],
        out_specs=pl.BlockSpec((1, n, d), lambda b: (b, blk, 0)),
        input_output_aliases={5: 0},
        compiler_params=_cparams(("arbitrary",)),
        name="fnet_dft_context",
    )(dftc, v3, h3, mod4, fnet_b.reshape(1, d), partial)


def _silu(x):
    return x * (1.0 / (1.0 + jnp.exp(-x)))


def _ffn_kernel(h_ref, mod_ref, wg_ref, wu_ref, wd_ref, o_ref, u_scr, acc_ref):
    f = pl.program_id(1)
    sub = h_ref.shape[0] // TILE

    @pl.when(f == 0)
    def _():
        for j in range(sub):
            rows = slice(j * TILE, (j + 1) * TILE)
            mod = mod_ref[j]
            u_scr[rows, :] = _normmod(h_ref[rows, :], mod[GS_F:GS_F + 1], mod[SH_F:SH_F + 1]).astype(BF16)
        acc_ref[...] = jnp.zeros_like(acc_ref)

    u = u_scr[...]
    a = _silu(_dot(u, wg_ref[...])) * _dot(u, wu_ref[...])
    acc_ref[...] += _dot(a.astype(BF16), wd_ref[...])

    @pl.when(f == pl.num_programs(1) - 1)
    def _():
        for j in range(sub):
            rows = slice(j * TILE, (j + 1) * TILE)
            o_ref[rows, :] = h_ref[rows, :] + mod_ref[j, G_F:G_F + 1] * acc_ref[rows, :]


def _ffn(h, mod, w_gate, w_up, w_down):
    r, d = h.shape
    ff = w_gate.shape[1]
    tm, tf = ROW_BLOCK, 256
    sub = tm // TILE
    return pl.pallas_call(
        _ffn_kernel,
        out_shape=jax.ShapeDtypeStruct((r, d), F32),
        grid=(r // tm, ff // tf),
        in_specs=[pl.BlockSpec((tm, d), lambda i, f: (i, 0)),
                  pl.BlockSpec((sub, MOD_ROWS, d), lambda i, f: (i, 0, 0)),
                  pl.BlockSpec((d, tf), lambda i, f: (0, f)),
                  pl.BlockSpec((d, tf), lambda i, f: (0, f)),
                  pl.BlockSpec((tf, d), lambda i, f: (f, 0))],
        out_specs=pl.BlockSpec((tm, d), lambda i, f: (i, 0)),
        scratch_shapes=[pltpu.VMEM((tm, d), BF16), pltpu.VMEM((tm, d), F32)],
        compiler_params=_cparams(("arbitrary", "arbitrary")),
        name="ffn_swiglu",
    )(h, mod, w_gate.astype(BF16), w_up.astype(BF16), w_down.astype(BF16))


MOE_TILE = 2048
MOE_CHUNK = 256


def _router_kernel(h_ref, mod_ref, r_ref, u_ref, cw_ref, slot_ref, slott_ref, cnt_ref):
    lane = lax.broadcasted_iota(jnp.int32, (TILE, LANES), 1).astype(F32)
    earlier = jnp.where(lax.broadcasted_iota(jnp.int32, (TILE, TILE), 1)
                        < lax.broadcasted_iota(jnp.int32, (TILE, TILE), 0), 1.0, 0.0).astype(BF16)
    neg = jnp.float32(-jnp.inf)
    count = jnp.zeros((1, LANES), F32)
    for j in range(h_ref.shape[0] // TILE):
        rows = slice(j * TILE, (j + 1) * TILE)
        mod = mod_ref[j]
        u = _normmod(h_ref[rows, :], mod[GS_F:GS_F + 1], mod[SH_F:SH_F + 1])
        u_ref[rows, :] = u.astype(BF16)
        logits = _dot3(u, r_ref[...])
        logits = jnp.where(lane < N_EXPERTS, logits, neg)
        m1 = jnp.max(logits, axis=-1, keepdims=True)
        i1 = jnp.min(jnp.where(logits == m1, lane, float(LANES)), axis=-1, keepdims=True)
        rest = jnp.where(lane == i1, neg, logits)
        m2 = jnp.max(rest, axis=-1, keepdims=True)
        i2 = jnp.min(jnp.where(rest == m2, lane, float(LANES)), axis=-1, keepdims=True)
        e2 = jnp.exp(m2 - m1)
        w1 = 1.0 / (1.0 + e2)
        w2 = e2 / (1.0 + e2)
        cw_ref[rows, :] = jnp.where(lane == i1, w1, jnp.where(lane == i2, w2, 0.0))
        sel = jnp.where(lane == i1, 1.0, jnp.where(lane == i2, 1.0, 0.0))
        slot = jnp.where(sel > 0.0, _dot(earlier, sel.astype(BF16)) + count, -1.0)
        slot_ref[rows, :] = slot
        slott_ref[:, rows] = slot.T[:N_EXPERTS, :]
        count = count + jnp.sum(sel, axis=0, keepdims=True)
    cnt_ref[0] = jnp.broadcast_to(count, (N_EXPERTS, LANES))


def _router(h, mod, router):
    r, d = h.shape
    rp = jnp.zeros((d, LANES), F32).at[:, :N_EXPERTS].set(router)
    sub = MOE_TILE // TILE
    row = lambda w: pl.BlockSpec((MOE_TILE, w), lambda i: (i, 0))
    return pl.pallas_call(
        _router_kernel,
        out_shape=(jax.ShapeDtypeStruct((r, d), BF16), jax.ShapeDtypeStruct((r, LANES), F32),
                   jax.ShapeDtypeStruct((r, LANES), F32), jax.ShapeDtypeStruct((N_EXPERTS, r), F32),
                   jax.ShapeDtypeStruct((r // MOE_TILE, N_EXPERTS, LANES), F32)),
        grid=(r // MOE_TILE,),
        in_specs=[row(d),
                  pl.BlockSpec((sub, MOD_ROWS, d), lambda i: (i, 0, 0)),
                  pl.BlockSpec((d, LANES), lambda i: (0, 0))],
        out_specs=(row(d), row(LANES), row(LANES),
                   pl.BlockSpec((N_EXPERTS, MOE_TILE), lambda i: (0, i)),
                   pl.BlockSpec((1, N_EXPERTS, LANES), lambda i: (i, 0, 0))),
        compiler_params=_cparams(("arbitrary",)),
        name="moe_router",
    )(h, mod, rp)


def _moe_kernel(nch_ref, h_ref, u_ref, cw_ref, slot_ref, slott_ref, mod_ref, wg_ref, wu_ref, wd_ref, o_ref,
                xs_scr, y_scr):
    i = pl.program_id(0)
    e = pl.program_id(1)
    f = pl.program_id(2)
    last_f = pl.num_programs(2) - 1
    tile = u_ref.shape[0]
    nch = nch_ref[i * N_EXPERTS + e]
    block = 2 * MOE_CHUNK

    @pl.when((e == 0) & (f == 0))
    def _():
        o_ref[...] = jnp.zeros_like(o_ref)

    nblk = lax.shift_right_logical(nch, 1)
    has_tail = (nch & 1) == 1
    tail0 = pl.multiple_of(nblk * block, MOE_CHUNK)

    def run_block(r0, rn):
        rows = pl.ds(r0, rn)

        @pl.when(f == 0)
        def _():
            rid = (lax.broadcasted_iota(jnp.int32, (rn, 1), 0) + r0).astype(F32)
            pick = jnp.where(slott_ref[pl.ds(e, 1), :] == rid, 1.0, 0.0).astype(BF16)
            xs_scr[rows, :] = _dot(pick, u_ref[...]).astype(BF16)

        xs = xs_scr[rows, :]
        a = _silu(_dot(xs, wg_ref[0])) * _dot(xs, wu_ref[0])
        part = _dot(a.astype(BF16), wd_ref[0])

        @pl.when(f == 0)
        def _():
            y_scr[rows, :] = part

        @pl.when(f > 0)
        def _():
            y_scr[rows, :] += part

    def loop_blocks(fn):
        def body(b, carry):
            fn(pl.multiple_of(b * block, block), block)
            return carry
        lax.fori_loop(0, nblk, body, 0)
        pl.when(has_tail)(lambda: fn(tail0, MOE_CHUNK))

    loop_blocks(run_block)

    @pl.when((f == last_f) & (nch > 0))
    def _():
        mine = lax.broadcasted_iota(jnp.int32, (tile, LANES), 1) == e
        slot_col = jnp.sum(jnp.where(mine, slot_ref[...], 0.0), axis=-1, keepdims=True)
        cw_col = jnp.sum(jnp.where(mine, cw_ref[...], 0.0), axis=-1, keepdims=True)

        def scatter_block(r0, rn):
            cid = (lax.broadcasted_iota(jnp.int32, (1, rn), 1) + r0).astype(F32)
            put = jnp.where(slot_col == cid, 1.0, 0.0).astype(BF16)
            o_ref[...] += cw_col * _dot(put, y_scr[pl.ds(r0, rn), :].astype(BF16))

        loop_blocks(scatter_block)

    @pl.when((e == pl.num_programs(1) - 1) & (f == last_f))
    def _():
        for j in range(tile // TILE):
            rows = slice(j * TILE, (j + 1) * TILE)
            o_ref[rows, :] = h_ref[rows, :] + mod_ref[j, G_F:G_F + 1] * o_ref[rows, :]


def _moe(h, u, cw, slot, slott, counts, mod, w_gate, w_up, w_down):
    r, d = h.shape
    ne, _, ff = w_gate.shape
    tm, tf = MOE_TILE, 512
    sub = tm // TILE
    nch = ((counts[:, 0, :ne].astype(jnp.int32) + (MOE_CHUNK - 1)) // MOE_CHUNK).reshape(-1)
    once = pl.Buffered(1)
    row = lambda w: pl.BlockSpec((tm, w), lambda i, e, f, n: (i, 0), pipeline_mode=once)
    return pl.pallas_call(
        _moe_kernel,
        out_shape=jax.ShapeDtypeStruct((r, d), F32),
        grid_spec=pltpu.PrefetchScalarGridSpec(
            num_scalar_prefetch=1,
            grid=(r // tm, ne, ff // tf),
            in_specs=[row(d), row(d), row(LANES), row(LANES),
                      pl.BlockSpec((ne, tm), lambda i, e, f, n: (0, i), pipeline_mode=once),
                      pl.BlockSpec((sub, MOD_ROWS, d), lambda i, e, f, n: (i, 0, 0)),
                      pl.BlockSpec((1, d, tf), lambda i, e, f, n: (e, 0, f)),
                      pl.BlockSpec((1, d, tf), lambda i, e, f, n: (e, 0, f)),
                      pl.BlockSpec((1, tf, d), lambda i, e, f, n: (e, f, 0))],
            out_specs=pl.BlockSpec((tm, d), lambda i, e, f, n: (i, 0), pipeline_mode=once),
            scratch_shapes=[pltpu.VMEM((tm, d), BF16), pltpu.VMEM((tm, d), F32)]),
        compiler_params=_cparams(("arbitrary", "arbitrary", "arbitrary")),
        name="moe_experts",
    )(nch, h, u, cw, slot, slott, mod, w_gate.astype(BF16), w_up.astype(BF16), w_down.astype(BF16))


def _final_kernel(h_ref, g_ref, o_ref):
    h = h_ref[0]
    inv = lax.rsqrt(jnp.mean(h * h, axis=-1, keepdims=True) + EPS)
    o_ref[0] = h * inv * g_ref[...]


def _final_norm(h3, gain, seq):
    bsz, _, d = h3.shape
    return pl.pallas_call(
        _final_kernel,
        out_shape=jax.ShapeDtypeStruct((bsz, seq, d), F32),
        grid=(bsz, seq // TILE),
        in_specs=[pl.BlockSpec((1, TILE, d), lambda b, t: (b, t, 0)),
                  pl.BlockSpec((1, d), lambda b, t: (0, 0))],
        out_specs=pl.BlockSpec((1, TILE, d), lambda b, t: (b, t, 0)),
        compiler_params=_cparams(("arbitrary", "arbitrary")),
        name="final_norm",
    )(h3, gain.reshape(1, d))


def _tile_mods(mods, gain_mix, gain_ffn, bsz, lat_tiles):
    sh_m, sc_m, g_m, sh_f, sc_f, g_f = (mods[:, i] for i in range(N_MOD))
    rows = jnp.stack([gain_mix * (1.0 + sc_m), sh_m, g_m, gain_ffn * (1.0 + sc_f), sh_f, g_f,
                      jnp.zeros_like(g_f), jnp.zeros_like(g_f)], axis=1)
    lat = jnp.broadcast_to(rows[:bsz, None], (bsz, lat_tiles) + rows.shape[1:])
    ctx = jnp.broadcast_to(rows[bsz:, None], (bsz, 1) + rows.shape[1:])
    return jnp.concatenate([lat, ctx], axis=1).reshape(bsz * (lat_tiles + 1), MOD_ROWS, rows.shape[-1])


def kernel(x, c, ctx, c_ctx, ada_w, ada_b, norm_mix, norm_ffn, norm_final, s5_lambda_re, s5_lambda_im, s5_log_step, s5_b_re, s5_b_im, s5_c_re, s5_c_im, s5_d, s5_glu_w, s5_glu_b, pool_w, pool_b, pool_scale, fnet_w, fnet_b, ffn_w_gate, ffn_w_up, ffn_w_down, moe_router, moe_w_gate, moe_w_up, moe_w_down):
    bsz, seq, d = x.shape
    ctx_len = ctx.shape[1]
    depth = ada_w.shape[0]
    assert ctx_len == TILE and seq % ROW_BLOCK == 0 and d % LANES == 0
    tot = seq + ctx_len
    lat_tiles = seq // TILE
    tps = lat_tiles + 1
    r = bsz * tot
    assert r % ROW_BLOCK == 0

    cond = jnp.zeros((16, d), F32).at[:bsz].set(c).at[bsz].set(c_ctx)
    mods_all = _ada_all(cond, ada_w, ada_b)[:, :bsz + 1].reshape(depth, bsz + 1, N_MOD, d)

    h = jnp.concatenate([x, ctx], axis=1).reshape(r, d)
    for i in range(depth):
        kind, j = i % 3, i // 3
        mod = _tile_mods(mods_all[i], norm_mix[i], norm_ffn[i], bsz, lat_tiles)
        if kind == 0:
            u = _s5_pre(h, mod)
            kc, ec, cl, a = _s5_params(s5_lambda_re[j], s5_lambda_im[j], s5_log_step[j],
                                       s5_b_re[j], s5_b_im[j], s5_c_re[j], s5_c_im[j])
            y_lat, y_ctx = _s5_scan(u.reshape(bsz, tot, d), kc, ec, cl, a, seq, ctx_len)
            h = _s5_glu(h, u, y_lat, y_ctx, mod, s5_d[j], s5_glu_w[j], s5_glu_b[j], seq)
        elif kind == 1:
            h = _pool(h, mod, pool_w[j], pool_b[j], pool_scale[j], seq)
        else:
            g = _fnet_fold(fnet_w[j])
            v3 = _fnet_proj(h, mod, g).reshape(bsz, tot, 2 * d)
            h3 = h.reshape(bsz, tot, d)
            mod4 = mod.reshape(bsz, tps, MOD_ROWS, d)
            part = _fnet_lat(_dft_matrix(seq), v3, h3, mod4, fnet_b[j], seq)
            cc, sc = _dft_tables(ctx_len)
            dftc = jnp.asarray(np.concatenate([cc, sc], axis=1), BF16)
            h = _fnet_ctx(dftc, v3, h3, mod4, fnet_b[j], part, seq).reshape(r, d)
        kk = i // 2
        if i % 2 == 0:
            h = _ffn(h, mod, ffn_w_gate[kk], ffn_w_up[kk], ffn_w_down[kk])
        else:
            ub, cw, slot, slott, counts = _router(h, mod, moe_router[kk])
            h = _moe(h, ub, cw, slot, slott, counts, mod, moe_w_gate[kk], moe_w_up[kk], moe_w_down[kk])
    return _final_norm(h.reshape(bsz, tot, d), norm_final, seq)
```

```python
import functools
import math

import numpy as np
import jax
import jax.numpy as jnp
from jax import lax
from jax.experimental import pallas as pl
from jax.experimental.pallas import tpu as pltpu

F32 = jnp.float32
BF16 = jnp.bfloat16

EPS = 1e-6
GRID_W = 64
N_MOD = 6
S5_GROUP = 16
S5_STATE = 64
S5_CHUNK = 16
POOL_WINDOWS = (2, 4, 8, 16)
FNET_GROUPS = 4
N_EXPERTS = 8

LANES = 128
TILE = 256
ROW_BLOCK = 1024
VMEM_LIMIT = 56 * 1024 * 1024


def _cparams(sem, vmem=VMEM_LIMIT):
    return pltpu.CompilerParams(dimension_semantics=sem, vmem_limit_bytes=vmem)


def _dot(a, b):
    return jnp.dot(a, b, preferred_element_type=F32)


def _split(a):
    hi = a.astype(BF16)
    lo = (a - hi.astype(F32)).astype(BF16)
    return hi, lo


def _dot3(a, b):
    a_hi, a_lo = _split(a)
    b_hi, b_lo = _split(b)
    return _dot(a_hi, b_hi) + (_dot(a_lo, b_hi) + _dot(a_hi, b_lo))


def _normmod(h, gs, sh):
    inv = lax.rsqrt(jnp.mean(h * h, axis=-1, keepdims=True) + EPS)
    return h * inv * gs + sh


GS_M, SH_M, G_M, GS_F, SH_F, G_F = range(6)
MOD_ROWS = 8


def _ada_kernel(c_ref, w_ref, b_ref, o_ref):
    c = c_ref[...]
    s = c * (1.0 / (1.0 + jnp.exp(-c)))
    o_ref[0] = _dot3(s, w_ref[0]) + b_ref[0]


def _ada_all(cond, ada_w, ada_b):
    depth, d, n = ada_w.shape
    rows = cond.shape[0]
    tn = 1536
    return pl.pallas_call(
        _ada_kernel,
        out_shape=jax.ShapeDtypeStruct((depth, rows, n), F32),
        grid=(depth, n // tn),
        in_specs=[
            pl.BlockSpec((rows, d), lambda i, j: (0, 0)),
            pl.BlockSpec((1, d, tn), lambda i, j: (i, 0, j)),
            pl.BlockSpec((1, 1, tn), lambda i, j: (i, 0, j)),
        ],
        out_specs=pl.BlockSpec((1, rows, tn), lambda i, j: (i, 0, j)),
        compiler_params=_cparams(("arbitrary", "arbitrary")),
        name="ada_params",
    )(cond, ada_w, ada_b.reshape(depth, 1, n))


def _s5_pre_kernel(h_ref, mod_ref, u_ref):
    mod = mod_ref[0]
    u_ref[...] = _normmod(h_ref[...], mod[GS_M:GS_M + 1], mod[SH_M:SH_M + 1])


def _s5_pre(h, mod):
    r, d = h.shape
    return pl.pallas_call(
        _s5_pre_kernel,
        out_shape=jax.ShapeDtypeStruct((r, d), F32),
        grid=(r // TILE,),
        in_specs=[pl.BlockSpec((TILE, d), lambda i: (i, 0)),
                  pl.BlockSpec((1, MOD_ROWS, d), lambda i: (i, 0, 0))],
        out_specs=pl.BlockSpec((TILE, d), lambda i: (i, 0)),
        compiler_params=_cparams(("arbitrary",)),
        name="s5_pre",
    )(h, mod)


def _s5_params(lam_re, lam_im, log_step, b_re, b_im, c_re, c_im):
    hp = lax.Precision.HIGHEST
    L = S5_CHUNK
    n_dir, g, p = lam_re.shape
    gb = LANES // S5_GROUP
    nb = g // gb
    hch = S5_GROUP
    step = jnp.exp(log_step)[..., None]
    ar, ai = lam_re * step, lam_im * step
    k = jnp.arange(L + 1, dtype=F32)[None, :, None, None]
    mag = jnp.exp(ar[:, None] * k)
    ang = ai[:, None] * k
    pr, pi = mag * jnp.cos(ang), mag * jnp.sin(ang)
    lbr, lbi = pr[:, 1] - 1.0, pi[:, 1]
    den = lam_re * lam_re + lam_im * lam_im
    qr = (lbr * lam_re + lbi * lam_im) / den
    qi = (lbi * lam_re - lbr * lam_im) / den
    bbr = qr[..., None] * b_re - qi[..., None] * b_im
    bbi = qr[..., None] * b_im + qi[..., None] * b_re
    er = pr[..., None] * bbr[:, None] - pi[..., None] * bbi[:, None]
    ei = pr[..., None] * bbi[:, None] + pi[..., None] * bbr[:, None]
    kmat = (jnp.einsum('dgop,dkgpi->dgiko', c_re, er[:, :L], precision=hp)
            - jnp.einsum('dgop,dkgpi->dgiko', c_im, ei[:, :L], precision=hp))
    kc = kmat.reshape(n_dir, nb, gb * hch, L * hch)
    es = jnp.stack([er[:, L - 1::-1], ei[:, L - 1::-1]], axis=0)
    ec = jnp.transpose(es, (1, 3, 2, 5, 0, 4)).reshape(n_dir, nb, gb, L, hch, 2, p)
    ec = jnp.transpose(ec, (0, 1, 3, 2, 4, 5, 6)).reshape(n_dir, nb, L, gb * hch, 2 * p)
    clr = c_re[:, None] * pr[:, 1:, :, None, :] - c_im[:, None] * pi[:, 1:, :, None, :]
    cli = c_re[:, None] * pi[:, 1:, :, None, :] + c_im[:, None] * pr[:, 1:, :, None, :]
    cs = jnp.stack([clr, -cli], axis=0)
    cl = jnp.transpose(cs, (1, 3, 0, 5, 2, 4)).reshape(n_dir, nb, gb, 2, p, L, hch)
    cl = jnp.transpose(cl, (0, 1, 3, 2, 4, 5, 6)).reshape(n_dir, nb, 2 * gb * p, L * hch)
    a = jnp.stack([pr[:, L].reshape(n_dir, nb, 1, gb * p), pi[:, L].reshape(n_dir, nb, 1, gb * p)], axis=3)
    return kc.astype(BF16), ec.astype(BF16), cl.astype(BF16), a.reshape(n_dir, nb, 1, 2 * gb * p)


def _replicators():
    gb = LANES // S5_GROUP
    r16 = np.zeros((S5_CHUNK, S5_GROUP, S5_CHUNK, gb, S5_GROUP), np.float32)
    for k in range(S5_CHUNK):
        for o in range(S5_GROUP):
            r16[k, o, k, :, o] = 1.0
    r64 = np.zeros((2, S5_STATE, 2, gb, S5_STATE), np.float32)
    for part in range(2):
        for q in range(S5_STATE):
            r64[part, q, part, :, q] = 1.0
    return (jnp.asarray(r16.reshape(S5_CHUNK * S5_GROUP, S5_CHUNK * LANES), BF16),
            jnp.asarray(r64.reshape(2 * S5_STATE, 2 * gb * S5_STATE), BF16))


def _s5_scan_kernel(ul_ref, uc_ref, kc_ref, ec_ref, cl_ref, r16_ref, r64_ref, a_ref, yl_ref, yc_ref,
                    h_scr, s_scr, hs_scr, wts_scr, wy_scr):
    d = pl.program_id(1)
    t = pl.program_id(2)
    L = S5_CHUNK
    nstate = a_ref.shape[-1]
    half = nstate // 2
    nt = L * LANES
    blk = lambda i: slice(i * LANES, (i + 1) * LANES)

    def build_weights():
        row_gi = lax.broadcasted_iota(jnp.int32, (LANES, LANES), 0) // S5_GROUP
        lane_go = lax.broadcasted_iota(jnp.int32, (LANES, LANES), 1) // S5_GROUP
        toep = _dot(kc_ref[0, 0], r16_ref[...])
        zero = jnp.zeros((LANES, LANES), BF16)
        for lag in range(L):
            piece = jnp.where(row_gi == lane_go, toep[:, blk(lag)], 0.0).astype(BF16)
            for s in range(L - lag):
                wts_scr[blk(s), blk(s + lag)] = piece
        for tt in range(0, L, 2):
            wts_scr[blk(tt + 1), blk(tt)] = zero
        row_gi_w = lax.broadcasted_iota(jnp.int32, (LANES, nstate), 0) // S5_GROUP
        lane_gp = (lax.broadcasted_iota(jnp.int32, (LANES, nstate), 1) % half) // S5_STATE
        for s in range(L):
            e = _dot(ec_ref[0, 0, s], r64_ref[...])
            wts_scr[blk(s), nt:] = jnp.where(row_gi_w == lane_gp, e, 0.0).astype(BF16)
        row_gp = (lax.broadcasted_iota(jnp.int32, (nstate, LANES), 0) % half) // S5_STATE
        lane_go_t = lax.broadcasted_iota(jnp.int32, (nstate, LANES), 1) // S5_GROUP
        for tt in range(L):
            w = _dot(cl_ref[0, 0], r16_ref[:, blk(tt)])
            wy_scr[:, blk(tt)] = jnp.where(row_gp == lane_go_t, w, 0.0).astype(BF16)

    def process(x_ref, y_ref):
        bsz, tok, _ = x_ref.shape
        nc = tok // L
        m = bsz * nc
        flip = lambda i, n: i + d * (n - 1 - 2 * i)
        u = jnp.concatenate(
            [x_ref[:, pl.ds(flip(s, L), nc, stride=L), :].reshape(m, LANES) for s in range(L)], axis=1).astype(BF16)
        s_all = _dot(u, wts_scr[:, nt:])
        nq = nstate // LANES
        nh = nq // 2
        for q in range(nq):
            s_scr[q, 0:m, :] = s_all[:, blk(q)]
        a = a_ref[0, 0]
        a_re = [jnp.broadcast_to(a[:, blk(q)], (bsz, LANES)) for q in range(nh)]
        a_im = [jnp.broadcast_to(a[:, blk(nh + q)], (bsz, LANES)) for q in range(nh)]
        h_re = [h_scr[q] for q in range(nh)]
        h_im = [h_scr[nh + q] for q in range(nh)]
        for c in range(nc):
            rows = pl.ds(flip(c, nc), bsz, stride=nc)
            for q in range(nh):
                hs_scr.at[q][rows, :] = h_re[q]
                hs_scr.at[nh + q][rows, :] = h_im[q]
                s_re = s_scr.at[q][rows, :]
                s_im = s_scr.at[nh + q][rows, :]
                h_re[q], h_im[q] = (a_re[q] * h_re[q] - a_im[q] * h_im[q] + s_re,
                                    a_re[q] * h_im[q] + a_im[q] * h_re[q] + s_im)
        for q in range(nh):
            h_scr[q] = h_re[q]
            h_scr[nh + q] = h_im[q]
        hs = jnp.concatenate([hs_scr[q, 0:m, :] for q in range(nq)], axis=1).astype(BF16)
        for n in range(L // 2):
            cols = slice(2 * n * LANES, 2 * (n + 1) * LANES)
            y = _dot(u[:, :cols.stop], wts_scr[:cols.stop, cols]) + _dot(hs, wy_scr[:, cols])
            for tt in (2 * n, 2 * n + 1):
                y_ref[:, pl.ds(flip(tt, L), nc, stride=L), :] = (
                    y[:, blk(tt - 2 * n)].reshape(bsz, nc, LANES))

    @pl.when(t == 0)
    def _():
        build_weights()
        h_scr[...] = jnp.zeros_like(h_scr)
        process(uc_ref, yc_ref)

    @pl.when(t > 0)
    def _():
        process(ul_ref, yl_ref)


def _s5_scan(u3, kc, ec, cl, a, seq, ctx_len):
    bsz, _, d = u3.shape
    tl = 1024
    ntl = seq // tl
    nb = d // LANES
    n_dir = 2
    ctx_blk = seq // ctx_len
    nstate = a.shape[-1]
    nt = S5_CHUNK * LANES
    r16, r64 = _replicators()

    def lat_idx(dd, t):
        i = jnp.maximum(t - 1, 0)
        return jnp.where(dd == 0, i, ntl - 1 - i)

    per_block = lambda arr: pl.BlockSpec((1, 1) + arr.shape[2:], lambda j, dd, t: (dd, j) + (0,) * (arr.ndim - 2))
    const = lambda arr: pl.BlockSpec(arr.shape, lambda j, dd, t: (0,) * arr.ndim)
    return pl.pallas_call(
        _s5_scan_kernel,
        out_shape=(jax.ShapeDtypeStruct((n_dir, bsz, seq, d), F32),
                   jax.ShapeDtypeStruct((n_dir, bsz, ctx_len, d), F32)),
        grid=(nb, n_dir, ntl + 1),
        in_specs=[
            pl.BlockSpec((bsz, tl, LANES), lambda j, dd, t: (0, lat_idx(dd, t), j)),
            pl.BlockSpec((bsz, ctx_len, LANES), lambda j, dd, t: (0, ctx_blk, j)),
            per_block(kc), per_block(ec), per_block(cl), const(r16), const(r64), per_block(a),
        ],
        out_specs=(
            pl.BlockSpec((None, bsz, tl, LANES), lambda j, dd, t: (dd, 0, lat_idx(dd, t), j)),
            pl.BlockSpec((None, bsz, ctx_len, LANES), lambda j, dd, t: (dd, 0, 0, j)),
        ),
        scratch_shapes=[
            pltpu.VMEM((nstate // LANES, bsz, LANES), F32),
            pltpu.VMEM((nstate // LANES, bsz * tl // S5_CHUNK, LANES), F32),
            pltpu.VMEM((nstate // LANES, bsz * tl // S5_CHUNK, LANES), F32),
            pltpu.VMEM((nt, nt + nstate), BF16),
            pltpu.VMEM((nstate, nt), BF16),
        ],
        compiler_params=_cparams(("arbitrary", "arbitrary", "arbitrary")),
        name="s5_scan",
    )(u3, u3, kc, ec, cl, r16, r64, a)


def _gelu_tanh(x):
    return 0.5 * x * (1.0 + jnp.tanh(math.sqrt(2.0 / math.pi) * (x + 0.044715 * (x * x * x))))


def _s5_glu_kernel(tiles_per_seq, h_ref, u_ref, ylf_ref, ylr_ref, ycf_ref, ycr_ref, mod_ref, dvec_ref,
                   w_ref, b_ref, o_ref):
    is_ctx = (pl.program_id(0) % tiles_per_seq) == tiles_per_seq - 1
    y_scan = jnp.where(is_ctx, ycf_ref[0, 0] + ycr_ref[0, 0], ylf_ref[0, 0] + ylr_ref[0, 0])
    y = dvec_ref[...] * u_ref[...] + y_scan
    z = _gelu_tanh(y)
    gate = 1.0 / (1.0 + jnp.exp(-(_dot(z.astype(BF16), w_ref[...]) + b_ref[...])))
    o_ref[...] = h_ref[...] + mod_ref[0, G_M:G_M + 1] * (z * gate)


def _s5_glu(h, u, y_lat, y_ctx, mod, dvec, glu_w, glu_b, seq):
    r, d = h.shape
    lat_tiles = seq // TILE
    tps = lat_tiles + 1

    def lat_map(dd):
        return lambda i: (dd, i // tps, jnp.minimum(i % tps, lat_tiles - 1), 0)

    def ctx_map(dd):
        return lambda i: (dd, i // tps, 0, 0)

    row = pl.BlockSpec((TILE, d), lambda i: (i, 0))
    vec = pl.BlockSpec((1, d), lambda i: (0, 0))
    return pl.pallas_call(
        functools.partial(_s5_glu_kernel, tps),
        out_shape=jax.ShapeDtypeStruct((r, d), F32),
        grid=(r // TILE,),
        in_specs=[row, row,
                  pl.BlockSpec((1, 1, TILE, d), lat_map(0)), pl.BlockSpec((1, 1, TILE, d), lat_map(1)),
                  pl.BlockSpec((1, 1, TILE, d), ctx_map(0)), pl.BlockSpec((1, 1, TILE, d), ctx_map(1)),
                  pl.BlockSpec((1, MOD_ROWS, d), lambda i: (i, 0, 0)),
                  vec, pl.BlockSpec((d, d), lambda i: (0, 0)), vec],
        out_specs=row,
        compiler_params=_cparams(("arbitrary",)),
        name="s5_glu",
    )(h, u, y_lat, y_lat, y_ctx, y_ctx, mod, dvec.reshape(1, d), glu_w.astype(BF16), glu_b.reshape(1, d))


def _pool_matrices(width, tile):
    mats = np.zeros((len(POOL_WINDOWS), tile, tile), np.float32)
    pos = np.arange(width)
    for g, win in enumerate(POOL_WINDOWS):
        lo = np.clip(pos - win // 2, 0, width)
        hi = np.clip(pos + win // 2, 0, width)
        for base in range(0, tile, width):
            for j in range(width):
                mats[g, base + j, base + lo[j]:base + hi[j]] = 1.0 / float(hi[j] - lo[j])
    return mats


def _pool_kernel(h_ref, mod_ref, a_ref, w_ref, b_ref, sc_ref, o_ref):
    mod = mod_ref[0]
    h = h_ref[...]
    u = _normmod(h, mod[GS_M:GS_M + 1], mod[SH_M:SH_M + 1])
    ub = u.astype(BF16)
    gw = w_ref.shape[-1]
    for g in range(w_ref.shape[0]):
        cols = slice(g * gw, (g + 1) * gw)
        res = _dot(a_ref[0, g], ub[:, cols]) - u[:, cols]
        mixed = _dot(res.astype(BF16), w_ref[g]) + b_ref[:, cols]
        o_ref[:, cols] = h[:, cols] + mod[G_M:G_M + 1, cols] * (mixed * sc_ref[:, cols])


def _pool(h, mod, pool_w, pool_b, pool_scale, seq):
    r, d = h.shape
    tps = seq // TILE + 1
    ng, gw, _ = pool_w.shape
    mats = jnp.asarray(np.stack([_pool_matrices(GRID_W, TILE), _pool_matrices(TILE, TILE)]), BF16)
    row = pl.BlockSpec((TILE, d), lambda i: (i, 0))
    vec = pl.BlockSpec((1, d), lambda i: (0, 0))
    return pl.pallas_call(
        _pool_kernel,
        out_shape=jax.ShapeDtypeStruct((r, d), F32),
        grid=(r // TILE,),
        in_specs=[row, pl.BlockSpec((1, MOD_ROWS, d), lambda i: (i, 0, 0)),
                  pl.BlockSpec((1, ng, TILE, TILE), lambda i: ((i % tps) // (tps - 1), 0, 0, 0)),
                  pl.BlockSpec((ng, gw, gw), lambda i: (0, 0, 0)), vec, vec],
        out_specs=row,
        compiler_params=_cparams(("arbitrary",)),
        name="pool_mixer",
    )(h, mod, mats, pool_w.astype(BF16), pool_b.reshape(1, d), pool_scale.reshape(1, d))


def _dft_tables(n):
    idx = np.arange(n)
    ang = 2.0 * np.pi * ((idx[:, None] * idx[None, :]) % n) / n
    scale = 1.0 / math.sqrt(n)
    return (np.cos(ang) * scale).astype(np.float32), (np.sin(ang) * scale).astype(np.float32)


def _fnet_fold_kernel(cc_ref, sc_ref, w_ref, o_ref):
    w = w_ref[...]
    d = w.shape[-1]
    o_ref[:, :d] = _dot3(cc_ref[...], w).astype(o_ref.dtype)
    o_ref[:, d:] = (-_dot3(sc_ref[...], w)).astype(o_ref.dtype)


def _fnet_fold(fnet_w):
    d = fnet_w.shape[0]
    gw = d // FNET_GROUPS
    cc, sc = _dft_tables(gw)
    sq = pl.BlockSpec((gw, gw), lambda i: (0, 0))
    return pl.pallas_call(
        _fnet_fold_kernel,
        out_shape=jax.ShapeDtypeStruct((d, 2 * d), BF16),
        grid=(FNET_GROUPS,),
        in_specs=[sq, sq, pl.BlockSpec((gw, d), lambda i: (i, 0))],
        out_specs=pl.BlockSpec((gw, 2 * d), lambda i: (i, 0)),
        compiler_params=_cparams(("arbitrary",)),
        name="fnet_fold",
    )(jnp.asarray(cc), jnp.asarray(sc), fnet_w)


def _fnet_proj_kernel(h_ref, mod_ref, g_ref, v_ref):
    for j in range(h_ref.shape[0] // TILE):
        rows = slice(j * TILE, (j + 1) * TILE)
        mod = mod_ref[j]
        u = _normmod(h_ref[rows, :], mod[GS_M:GS_M + 1], mod[SH_M:SH_M + 1])
        v_ref[rows, :] = _dot(u.astype(BF16), g_ref[...]).astype(v_ref.dtype)


def _fnet_proj(h, mod, g):
    r, d = h.shape
    sub = ROW_BLOCK // TILE
    return pl.pallas_call(
        _fnet_proj_kernel,
        out_shape=jax.ShapeDtypeStruct((r, 2 * d), BF16),
        grid=(r // ROW_BLOCK,),
        in_specs=[pl.BlockSpec((ROW_BLOCK, d), lambda i: (i, 0)),
                  pl.BlockSpec((sub, MOD_ROWS, d), lambda i: (i, 0, 0)),
                  pl.BlockSpec((d, 2 * d), lambda i: (0, 0))],
        out_specs=pl.BlockSpec((ROW_BLOCK, 2 * d), lambda i: (i, 0)),
        compiler_params=_cparams(("arbitrary",)),
        name="fnet_proj",
    )(h, mod, g)


def _dft_matrix(n):
    n1 = int(round(math.sqrt(n)))
    assert n1 * n1 == n
    hi = jnp.arange(n1, dtype=jnp.int32)[:, None]
    k = jnp.arange(n, dtype=jnp.int32)[None, :]
    ang_a = (2.0 * math.pi / n1) * ((hi * k) % n1).astype(F32)
    ang_b = (2.0 * math.pi / n) * ((hi * k) % n).astype(F32)
    ca, sa, cb, sb = jnp.cos(ang_a), jnp.sin(ang_a), jnp.cos(ang_b), jnp.sin(ang_b)
    scale = 1.0 / math.sqrt(n)
    c = (ca[:, None] * cb[None] - sa[:, None] * sb[None]).reshape(n, n) * scale
    s = (sa[:, None] * cb[None] + ca[:, None] * sb[None]).reshape(n, n) * scale
    return jnp.concatenate([c, s], axis=1).astype(BF16)


def _fnet_lat_kernel(f_ref, v_ref, h_ref, mod_ref, b_ref, o_ref, acc_ref):
    k = pl.program_id(2)

    @pl.when(k == 0)
    def _():
        acc_ref[...] = jnp.zeros_like(acc_ref)

    acc_ref[...] += _dot(f_ref[...], v_ref[0])

    @pl.when(k == pl.num_programs(2) - 1)
    def _():
        for j in range(h_ref.shape[1] // TILE):
            rows = slice(j * TILE, (j + 1) * TILE)
            o_ref[0, rows, :] = h_ref[0, rows, :] + mod_ref[0, j, G_M:G_M + 1] * (acc_ref[rows, :] + b_ref[...])


def _fnet_lat(dft, v3, h3, mod4, fnet_b, seq):
    bsz, tot, d = h3.shape
    tm, tk = ROW_BLOCK, min(2048, seq)
    kh = seq // tk
    sub = tm // TILE
    return pl.pallas_call(
        _fnet_lat_kernel,
        out_shape=jax.ShapeDtypeStruct((bsz, tot, d), F32),
        grid=(bsz, seq // tm, 2 * kh),
        in_specs=[pl.BlockSpec((tm, tk), lambda b, m, k: (m, k)),
                  pl.BlockSpec((1, tk, d), lambda b, m, k: (b, k % kh, k // kh)),
                  pl.BlockSpec((1, tm, d), lambda b, m, k: (b, m, 0)),
                  pl.BlockSpec((1, sub, MOD_ROWS, d), lambda b, m, k: (b, m, 0, 0)),
                  pl.BlockSpec((1, d), lambda b, m, k: (0, 0))],
        out_specs=pl.BlockSpec((1, tm, d), lambda b, m, k: (b, m, 0)),
        scratch_shapes=[pltpu.VMEM((tm, d), F32)],
        compiler_params=_cparams(("arbitrary", "arbitrary", "arbitrary")),
        name="fnet_dft_latent",
    )(dft, v3, h3, mod4, fnet_b.reshape(1, d))


def _fnet_ctx_kernel(f_ref, v_ref, h_ref, mod_ref, b_ref, prev_ref, o_ref):
    del prev_ref
    n = f_ref.shape[0]
    d = h_ref.shape[-1]
    acc = _dot(f_ref[:, :n], v_ref[0, :, :d]) + _dot(f_ref[:, n:], v_ref[0, :, d:])
    o_ref[0] = h_ref[0] + mod_ref[0, 0, G_M:G_M + 1] * (acc + b_ref[...])


def _fnet_ctx(dftc, v3, h3, mod4, fnet_b, partial, seq):
    bsz, tot, d = h3.shape
    n = tot - seq
    blk = seq // n
    return pl.pallas_call(
        _fnet_ctx_kernel,
        out_shape=jax.ShapeDtypeStruct((bsz, tot, d), F32),
        grid=(bsz,),
        in_specs=[pl.BlockSpec((n, 2 * n), lambda b: (0, 0)),
                  pl.BlockSpec((1, n, 2 * d), lambda b: (b, blk, 0)),
                  pl.BlockSpec((1, n, d), lambda b: (b, blk, 0)),
                  pl.BlockSpec((1, 1, MOD_ROWS, d), lambda b: (b, blk, 0, 0)),
                  pl.BlockSpec((1, d), lambda b: (0, 0)),
                  pl.BlockSpec(memory_space=pl.ANY)],
        out_specs=pl.BlockSpec((1, n, d), lambda b: (b, blk, 0)),
        input_output_aliases={5: 0},
        compiler_params=_cparams(("arbitrary",)),
        name="fnet_dft_context",
    )(dftc, v3, h3, mod4, fnet_b.reshape(1, d), partial)


def _silu(x):
    return x * (1.0 / (1.0 + jnp.exp(-x)))


def _ffn_kernel(h_ref, mod_ref, wg_ref, wu_ref, wd_ref, o_ref, u_scr, acc_ref):
    f = pl.program_id(1)
    sub = h_ref.shape[0] // TILE

    @pl.when(f == 0)
    def _():
        for j in range(sub):
            rows = slice(j * TILE, (j + 1) * TILE)
            mod = mod_ref[j]
            u_scr[rows, :] = _normmod(h_ref[rows, :], mod[GS_F:GS_F + 1], mod[SH_F:SH_F + 1]).astype(BF16)
        acc_ref[...] = jnp.zeros_like(acc_ref)

    u = u_scr[...]
    a = _silu(_dot(u, wg_ref[...])) * _dot(u, wu_ref[...])
    acc_ref[...] += _dot(a.astype(BF16), wd_ref[...])

    @pl.when(f == pl.num_programs(1) - 1)
    def _():
        for j in range(sub):
            rows = slice(j * TILE, (j + 1) * TILE)
            o_ref[rows, :] = h_ref[rows, :] + mod_ref[j, G_F:G_F + 1] * acc_ref[rows, :]


def _ffn(h, mod, w_gate, w_up, w_down):
    r, d = h.shape
    ff = w_gate.shape[1]
    tm, tf = ROW_BLOCK, 256
    sub = tm // TILE
    return pl.pallas_call(
        _ffn_kernel,
        out_shape=jax.ShapeDtypeStruct((r, d), F32),
        grid=(r // tm, ff // tf),
        in_specs=[pl.BlockSpec((tm, d), lambda i, f: (i, 0)),
                  pl.BlockSpec((sub, MOD_ROWS, d), lambda i, f: (i, 0, 0)),
                  pl.BlockSpec((d, tf), lambda i, f: (0, f)),
                  pl.BlockSpec((d, tf), lambda i, f: (0, f)),
                  pl.BlockSpec((tf, d), lambda i, f: (f, 0))],
        out_specs=pl.BlockSpec((tm, d), lambda i, f: (i, 0)),
        scratch_shapes=[pltpu.VMEM((tm, d), BF16), pltpu.VMEM((tm, d), F32)],
        compiler_params=_cparams(("arbitrary", "arbitrary")),
        name="ffn_swiglu",
    )(h, mod, w_gate.astype(BF16), w_up.astype(BF16), w_down.astype(BF16))


MOE_TILE = 2048
MOE_CHUNK = 128


def _router_kernel(h_ref, mod_ref, r_ref, u_ref, cw_ref, slot_ref, slott_ref, cnt_ref):
    lane = lax.broadcasted_iota(jnp.int32, (TILE, LANES), 1).astype(F32)
    earlier = jnp.where(lax.broadcasted_iota(jnp.int32, (TILE, TILE), 1)
                        < lax.broadcasted_iota(jnp.int32, (TILE, TILE), 0), 1.0, 0.0).astype(BF16)
    neg = jnp.float32(-jnp.inf)
    count = jnp.zeros((1, LANES), F32)
    for j in range(h_ref.shape[0] // TILE):
        rows = slice(j * TILE, (j + 1) * TILE)
        mod = mod_ref[j]
        u = _normmod(h_ref[rows, :], mod[GS_F:GS_F + 1], mod[SH_F:SH_F + 1])
        u_ref[rows, :] = u.astype(BF16)
        logits = _dot3(u, r_ref[...])
        logits = jnp.where(lane < N_EXPERTS, logits, neg)
        m1 = jnp.max(logits, axis=-1, keepdims=True)
        i1 = jnp.min(jnp.where(logits == m1, lane, float(LANES)), axis=-1, keepdims=True)
        rest = jnp.where(lane == i1, neg, logits)
        m2 = jnp.max(rest, axis=-1, keepdims=True)
        i2 = jnp.min(jnp.where(rest == m2, lane, float(LANES)), axis=-1, keepdims=True)
        e2 = jnp.exp(m2 - m1)
        w1 = 1.0 / (1.0 + e2)
        w2 = e2 / (1.0 + e2)
        cw_ref[rows, :] = jnp.where(lane == i1, w1, jnp.where(lane == i2, w2, 0.0))
        sel = jnp.where(lane == i1, 1.0, jnp.where(lane == i2, 1.0, 0.0))
        slot = jnp.where(sel > 0.0, _dot(earlier, sel.astype(BF16)) + count, -1.0)
        slot_ref[rows, :] = slot
        slott_ref[:, rows] = slot.T[:N_EXPERTS, :]
        count = count + jnp.sum(sel, axis=0, keepdims=True)
    cnt_ref[0] = jnp.broadcast_to(count, (N_EXPERTS, LANES))


def _router(h, mod, router):
    r, d = h.shape
    rp = jnp.zeros((d, LANES), F32).at[:, :N_EXPERTS].set(router)
    sub = MOE_TILE // TILE
    row = lambda w: pl.BlockSpec((MOE_TILE, w), lambda i: (i, 0))
    return pl.pallas_call(
        _router_kernel,
        out_shape=(jax.ShapeDtypeStruct((r, d), BF16), jax.ShapeDtypeStruct((r, LANES), F32),
                   jax.ShapeDtypeStruct((r, LANES), F32), jax.ShapeDtypeStruct((N_EXPERTS, r), F32),
                   jax.ShapeDtypeStruct((r // MOE_TILE, N_EXPERTS, LANES), F32)),
        grid=(r // MOE_TILE,),
        in_specs=[row(d),
                  pl.BlockSpec((sub, MOD_ROWS, d), lambda i: (i, 0, 0)),
                  pl.BlockSpec((d, LANES), lambda i: (0, 0))],
        out_specs=(row(d), row(LANES), row(LANES),
                   pl.BlockSpec((N_EXPERTS, MOE_TILE), lambda i: (0, i)),
                   pl.BlockSpec((1, N_EXPERTS, LANES), lambda i: (i, 0, 0))),
        compiler_params=_cparams(("arbitrary",)),
        name="moe_router",
    )(h, mod, rp)


def _moe_kernel(nch_ref, h_ref, u_ref, cw_ref, slot_ref, slott_ref, mod_ref, wg_ref, wu_ref, wd_ref, o_ref,
                xs_scr, y_scr):
    i = pl.program_id(0)
    e = pl.program_id(1)
    f = pl.program_id(2)
    last_f = pl.num_programs(2) - 1
    tile = u_ref.shape[0]
    nch = nch_ref[i * N_EXPERTS + e]
    block = 4 * MOE_CHUNK

    @pl.when((e == 0) & (f == 0))
    def _():
        o_ref[...] = jnp.zeros_like(o_ref)

    nblk = lax.shift_right_logical(nch, 2)
    tail2 = pl.multiple_of(nblk * block, MOE_CHUNK)
    tail1 = pl.multiple_of(tail2 + (nch & 2) * MOE_CHUNK, MOE_CHUNK)

    def run_block(r0, rn):
        rows = pl.ds(r0, rn)

        @pl.when(f == 0)
        def _():
            rid = (lax.broadcasted_iota(jnp.int32, (rn, 1), 0) + r0).astype(F32)
            pick = jnp.where(slott_ref[pl.ds(e, 1), :] == rid, 1.0, 0.0).astype(BF16)
            xs_scr[rows, :] = _dot(pick, u_ref[...]).astype(BF16)

        xs = xs_scr[rows, :]
        a = _silu(_dot(xs, wg_ref[0])) * _dot(xs, wu_ref[0])
        part = _dot(a.astype(BF16), wd_ref[0])

        @pl.when(f == 0)
        def _():
            y_scr[rows, :] = part

        @pl.when(f > 0)
        def _():
            y_scr[rows, :] += part

    def loop_blocks(fn):
        def body(b, carry):
            fn(pl.multiple_of(b * block, block), block)
            return carry
        lax.fori_loop(0, nblk, body, 0)
        pl.when((nch & 2) != 0)(lambda: fn(tail2, 2 * MOE_CHUNK))
        pl.when((nch & 1) != 0)(lambda: fn(tail1, MOE_CHUNK))

    loop_blocks(run_block)

    @pl.when((f == last_f) & (nch > 0))
    def _():
        mine = lax.broadcasted_iota(jnp.int32, (tile, LANES), 1) == e
        slot_col = jnp.sum(jnp.where(mine, slot_ref[...], 0.0), axis=-1, keepdims=True)
        cw_col = jnp.sum(jnp.where(mine, cw_ref[...], 0.0), axis=-1, keepdims=True)

        def scatter_block(r0, rn):
            cid = (lax.broadcasted_iota(jnp.int32, (1, rn), 1) + r0).astype(F32)
            put = jnp.where(slot_col == cid, 1.0, 0.0).astype(BF16)
            o_ref[...] += cw_col * _dot(put, y_scr[pl.ds(r0, rn), :].astype(BF16))

        loop_blocks(scatter_block)

    @pl.when((e == pl.num_programs(1) - 1) & (f == last_f))
    def _():
        for j in range(tile // TILE):
            rows = slice(j * TILE, (j + 1) * TILE)
            o_ref[rows, :] = h_ref[rows, :] + mod_ref[j, G_F:G_F + 1] * o_ref[rows, :]


def _moe(h, u, cw, slot, slott, counts, mod, w_gate, w_up, w_down):
    r, d = h.shape
    ne, _, ff = w_gate.shape
    tm, tf = MOE_TILE, 512
    sub = tm // TILE
    nch = ((counts[:, 0, :ne].astype(jnp.int32) + (MOE_CHUNK - 1)) // MOE_CHUNK).reshape(-1)
    once = pl.Buffered(1)
    row = lambda w: pl.BlockSpec((tm, w), lambda i, e, f, n: (i, 0), pipeline_mode=once)
    return pl.pallas_call(
        _moe_kernel,
        out_shape=jax.ShapeDtypeStruct((r, d), F32),
        grid_spec=pltpu.PrefetchScalarGridSpec(
            num_scalar_prefetch=1,
            grid=(r // tm, ne, ff // tf),
            in_specs=[row(d), row(d), row(LANES), row(LANES),
                      pl.BlockSpec((ne, tm), lambda i, e, f, n: (0, i), pipeline_mode=once),
                      pl.BlockSpec((sub, MOD_ROWS, d), lambda i, e, f, n: (i, 0, 0)),
                      pl.BlockSpec((1, d, tf), lambda i, e, f, n: (e, 0, f)),
                      pl.BlockSpec((1, d, tf), lambda i, e, f, n: (e, 0, f)),
                      pl.BlockSpec((1, tf, d), lambda i, e, f, n: (e, f, 0))],
            out_specs=pl.BlockSpec((tm, d), lambda i, e, f, n: (i, 0), pipeline_mode=once),
            scratch_shapes=[pltpu.VMEM((tm, d), BF16), pltpu.VMEM((tm, d), F32)]),
        compiler_params=_cparams(("arbitrary", "arbitrary", "arbitrary")),
        name="moe_experts",
    )(nch, h, u, cw, slot, slott, mod, w_gate.astype(BF16), w_up.astype(BF16), w_down.astype(BF16))


def _final_kernel(h_ref, g_ref, o_ref):
    h = h_ref[0]
    inv = lax.rsqrt(jnp.mean(h * h, axis=-1, keepdims=True) + EPS)
    o_ref[0] = h * inv * g_ref[...]


def _final_norm(h3, gain, seq):
    bsz, _, d = h3.shape
    return pl.pallas_call(
        _final_kernel,
        out_shape=jax.ShapeDtypeStruct((bsz, seq, d), F32),
        grid=(bsz, seq // TILE),
        in_specs=[pl.BlockSpec((1, TILE, d), lambda b, t: (b, t, 0)),
                  pl.BlockSpec((1, d), lambda b, t: (0, 0))],
        out_specs=pl.BlockSpec((1, TILE, d), lambda b, t: (b, t, 0)),
        compiler_params=_cparams(("arbitrary", "arbitrary")),
        name="final_norm",
    )(h3, gain.reshape(1, d))


def _tile_mods(mods, gain_mix, gain_ffn, bsz, lat_tiles):
    sh_m, sc_m, g_m, sh_f, sc_f, g_f = (mods[:, i] for i in range(N_MOD))
    rows = jnp.stack([gain_mix * (1.0 + sc_m), sh_m, g_m, gain_ffn * (1.0 + sc_f), sh_f, g_f,
                      jnp.zeros_like(g_f), jnp.zeros_like(g_f)], axis=1)
    lat = jnp.broadcast_to(rows[:bsz, None], (bsz, lat_tiles) + rows.shape[1:])
    ctx = jnp.broadcast_to(rows[bsz:, None], (bsz, 1) + rows.shape[1:])
    return jnp.concatenate([lat, ctx], axis=1).reshape(bsz * (lat_tiles + 1), MOD_ROWS, rows.shape[-1])


def kernel(x, c, ctx, c_ctx, ada_w, ada_b, norm_mix, norm_ffn, norm_final, s5_lambda_re, s5_lambda_im, s5_log_step, s5_b_re, s5_b_im, s5_c_re, s5_c_im, s5_d, s5_glu_w, s5_glu_b, pool_w, pool_b, pool_scale, fnet_w, fnet_b, ffn_w_gate, ffn_w_up, ffn_w_down, moe_router, moe_w_gate, moe_w_up, moe_w_down):
    bsz, seq, d = x.shape
    ctx_len = ctx.shape[1]
    depth = ada_w.shape[0]
    assert ctx_len == TILE and seq % ROW_BLOCK == 0 and d % LANES == 0
    tot = seq + ctx_len
    lat_tiles = seq // TILE
    tps = lat_tiles + 1
    r = bsz * tot
    assert r % ROW_BLOCK == 0

    cond = jnp.zeros((16, d), F32).at[:bsz].set(c).at[bsz].set(c_ctx)
    mods_all = _ada_all(cond, ada_w, ada_b)[:, :bsz + 1].reshape(depth, bsz + 1, N_MOD, d)

    h = jnp.concatenate([x, ctx], axis=1).reshape(r, d)
    for i in range(depth):
        kind, j = i % 3, i // 3
        mod = _tile_mods(mods_all[i], norm_mix[i], norm_ffn[i], bsz, lat_tiles)
        if kind == 0:
            u = _s5_pre(h, mod)
            kc, ec, cl, a = _s5_params(s5_lambda_re[j], s5_lambda_im[j], s5_log_step[j],
                                       s5_b_re[j], s5_b_im[j], s5_c_re[j], s5_c_im[j])
            y_lat, y_ctx = _s5_scan(u.reshape(bsz, tot, d), kc, ec, cl, a, seq, ctx_len)
            h = _s5_glu(h, u, y_lat, y_ctx, mod, s5_d[j], s5_glu_w[j], s5_glu_b[j], seq)
        elif kind == 1:
            h = _pool(h, mod, pool_w[j], pool_b[j], pool_scale[j], seq)
        else:
            g = _fnet_fold(fnet_w[j])
            v3 = _fnet_proj(h, mod, g).reshape(bsz, tot, 2 * d)
            h3 = h.reshape(bsz, tot, d)
            mod4 = mod.reshape(bsz, tps, MOD_ROWS, d)
            part = _fnet_lat(_dft_matrix(seq), v3, h3, mod4, fnet_b[j], seq)
            cc, sc = _dft_tables(ctx_len)
            dftc = jnp.asarray(np.concatenate([cc, sc], axis=1), BF16)
            h = _fnet_ctx(dftc, v3, h3, mod4, fnet_b[j], part, seq).reshape(r, d)
        kk = i // 2
        if i % 2 == 0:
            h = _ffn(h, mod, ffn_w_gate[kk], ffn_w_up[kk], ffn_w_down[kk])
        else:
            ub, cw, slot, slott, counts = _router(h, mod, moe_router[kk])
            h = _moe(h, ub, cw, slot, slott, counts, mod, moe_w_gate[kk], moe_w_up[kk], moe_w_down[kk])
    return _final_norm(h.reshape(bsz, tot, d), norm_final, seq)
```

```python
import functools
import math

import numpy as np
import jax
import jax.numpy as jnp
from jax import lax
from jax.experimental import pallas as pl
from jax.experimental.pallas import tpu as pltpu

F32 = jnp.float32
BF16 = jnp.bfloat16

EPS = 1e-6
GRID_W = 64
N_MOD = 6
S5_GROUP = 16
S5_STATE = 64
S5_CHUNK = 16
POOL_WINDOWS = (2, 4, 8, 16)
FNET_GROUPS = 4
N_EXPERTS = 8

LANES = 128
TILE = 256
ROW_BLOCK = 1024
VMEM_LIMIT = 56 * 1024 * 1024


def _cparams(sem, vmem=VMEM_LIMIT):
    return pltpu.CompilerParams(dimension_semantics=sem, vmem_limit_bytes=vmem)


def _dot(a, b):
    return jnp.dot(a, b, preferred_element_type=F32)


def _split(a):
    hi = a.astype(BF16)
    lo = (a - hi.astype(F32)).astype(BF16)
    return hi, lo


def _dot3(a, b):
    a_hi, a_lo = _split(a)
    b_hi, b_lo = _split(b)
    return _dot(a_hi, b_hi) + (_dot(a_lo, b_hi) + _dot(a_hi, b_lo))


def _normmod(h, gs, sh):
    inv = lax.rsqrt(jnp.mean(h * h, axis=-1, keepdims=True) + EPS)
    return h * inv * gs + sh


GS_M, SH_M, G_M, GS_F, SH_F, G_F = range(6)
MOD_ROWS = 8


def _ada_kernel(c_ref, w_ref, b_ref, o_ref):
    c = c_ref[...]
    s = c * (1.0 / (1.0 + jnp.exp(-c)))
    o_ref[0] = _dot3(s, w_ref[0]) + b_ref[0]


def _ada_all(cond, ada_w, ada_b):
    depth, d, n = ada_w.shape
    rows = cond.shape[0]
    tn = 1536
    return pl.pallas_call(
        _ada_kernel,
        out_shape=jax.ShapeDtypeStruct((depth, rows, n), F32),
        grid=(depth, n // tn),
        in_specs=[
            pl.BlockSpec((rows, d), lambda i, j: (0, 0)),
            pl.BlockSpec((1, d, tn), lambda i, j: (i, 0, j)),
            pl.BlockSpec((1, 1, tn), lambda i, j: (i, 0, j)),
        ],
        out_specs=pl.BlockSpec((1, rows, tn), lambda i, j: (i, 0, j)),
        compiler_params=_cparams(("arbitrary", "arbitrary")),
        name="ada_params",
    )(cond, ada_w, ada_b.reshape(depth, 1, n))


def _stream_specs(src, tps, d):
    if len(src) == 1:
        return [pl.BlockSpec((TILE, d), lambda i: (i, 0))]
    return [pl.BlockSpec((None, TILE, d), lambda i: (i // tps, jnp.minimum(i % tps, tps - 2), 0)),
            pl.BlockSpec((None, TILE, d), lambda i: (i // tps, 0, 0))]


def _stream_tile(refs, tps):
    if len(refs) == 1:
        return refs[0][...]
    is_ctx = (pl.program_id(0) % tps) == tps - 1
    return jnp.where(is_ctx, refs[1][...], refs[0][...])


def _s5_pre_kernel(n_src, tps, *refs):
    mod_ref, u_ref = refs[n_src:]
    mod = mod_ref[0]
    u_ref[...] = _normmod(_stream_tile(refs[:n_src], tps), mod[GS_M:GS_M + 1], mod[SH_M:SH_M + 1])


def _s5_pre(src, mod, tps):
    r, _, d = mod.shape
    r *= TILE
    return pl.pallas_call(
        functools.partial(_s5_pre_kernel, len(src), tps),
        out_shape=jax.ShapeDtypeStruct((r, d), F32),
        grid=(r // TILE,),
        in_specs=_stream_specs(src, tps, d) + [pl.BlockSpec((1, MOD_ROWS, d), lambda i: (i, 0, 0))],
        out_specs=pl.BlockSpec((TILE, d), lambda i: (i, 0)),
        compiler_params=_cparams(("arbitrary",)),
        name="s5_pre",
    )(*src, mod)


def _s5_params(lam_re, lam_im, log_step, b_re, b_im, c_re, c_im):
    hp = lax.Precision.HIGHEST
    L = S5_CHUNK
    n_dir, g, p = lam_re.shape
    gb = LANES // S5_GROUP
    nb = g // gb
    hch = S5_GROUP
    step = jnp.exp(log_step)[..., None]
    ar, ai = lam_re * step, lam_im * step
    k = jnp.arange(L + 1, dtype=F32)[None, :, None, None]
    mag = jnp.exp(ar[:, None] * k)
    ang = ai[:, None] * k
    pr, pi = mag * jnp.cos(ang), mag * jnp.sin(ang)
    lbr, lbi = pr[:, 1] - 1.0, pi[:, 1]
    den = lam_re * lam_re + lam_im * lam_im
    qr = (lbr * lam_re + lbi * lam_im) / den
    qi = (lbi * lam_re - lbr * lam_im) / den
    bbr = qr[..., None] * b_re - qi[..., None] * b_im
    bbi = qr[..., None] * b_im + qi[..., None] * b_re
    er = pr[..., None] * bbr[:, None] - pi[..., None] * bbi[:, None]
    ei = pr[..., None] * bbi[:, None] + pi[..., None] * bbr[:, None]
    kmat = (jnp.einsum('dgop,dkgpi->dgiko', c_re, er[:, :L], precision=hp)
            - jnp.einsum('dgop,dkgpi->dgiko', c_im, ei[:, :L], precision=hp))
    kc = kmat.reshape(n_dir, nb, gb * hch, L * hch)
    es = jnp.stack([er[:, L - 1::-1], ei[:, L - 1::-1]], axis=0)
    ec = jnp.transpose(es, (1, 3, 2, 5, 0, 4)).reshape(n_dir, nb, gb, L, hch, 2, p)
    ec = jnp.transpose(ec, (0, 1, 3, 2, 4, 5, 6)).reshape(n_dir, nb, L, gb * hch, 2 * p)
    clr = c_re[:, None] * pr[:, 1:, :, None, :] - c_im[:, None] * pi[:, 1:, :, None, :]
    cli = c_re[:, None] * pi[:, 1:, :, None, :] + c_im[:, None] * pr[:, 1:, :, None, :]
    cs = jnp.stack([clr, -cli], axis=0)
    cl = jnp.transpose(cs, (1, 3, 0, 5, 2, 4)).reshape(n_dir, nb, gb, 2, p, L, hch)
    cl = jnp.transpose(cl, (0, 1, 3, 2, 4, 5, 6)).reshape(n_dir, nb, 2 * gb * p, L * hch)
    a = jnp.stack([pr[:, L].reshape(n_dir, nb, 1, gb * p), pi[:, L].reshape(n_dir, nb, 1, gb * p)], axis=3)
    return kc.astype(BF16), ec.astype(BF16), cl.astype(BF16), a.reshape(n_dir, nb, 1, 2 * gb * p)


def _replicators():
    gb = LANES // S5_GROUP
    r16 = np.zeros((S5_CHUNK, S5_GROUP, S5_CHUNK, gb, S5_GROUP), np.float32)
    for k in range(S5_CHUNK):
        for o in range(S5_GROUP):
            r16[k, o, k, :, o] = 1.0
    r64 = np.zeros((2, S5_STATE, 2, gb, S5_STATE), np.float32)
    for part in range(2):
        for q in range(S5_STATE):
            r64[part, q, part, :, q] = 1.0
    return (jnp.asarray(r16.reshape(S5_CHUNK * S5_GROUP, S5_CHUNK * LANES), BF16),
            jnp.asarray(r64.reshape(2 * S5_STATE, 2 * gb * S5_STATE), BF16))


def _s5_scan_kernel(ul_ref, uc_ref, kc_ref, ec_ref, cl_ref, r16_ref, r64_ref, a_ref, yl_ref, yc_ref,
                    h_scr, s_scr, hs_scr, wts_scr, wy_scr):
    d = pl.program_id(1)
    t = pl.program_id(2)
    L = S5_CHUNK
    nstate = a_ref.shape[-1]
    half = nstate // 2
    nt = L * LANES
    blk = lambda i: slice(i * LANES, (i + 1) * LANES)

    def build_weights():
        row_gi = lax.broadcasted_iota(jnp.int32, (LANES, LANES), 0) // S5_GROUP
        lane_go = lax.broadcasted_iota(jnp.int32, (LANES, LANES), 1) // S5_GROUP
        toep = _dot(kc_ref[0, 0], r16_ref[...])
        zero = jnp.zeros((LANES, LANES), BF16)
        for lag in range(L):
            piece = jnp.where(row_gi == lane_go, toep[:, blk(lag)], 0.0).astype(BF16)
            for s in range(L - lag):
                wts_scr[blk(s), blk(s + lag)] = piece
        for tt in range(0, L, 2):
            wts_scr[blk(tt + 1), blk(tt)] = zero
        row_gi_w = lax.broadcasted_iota(jnp.int32, (LANES, nstate), 0) // S5_GROUP
        lane_gp = (lax.broadcasted_iota(jnp.int32, (LANES, nstate), 1) % half) // S5_STATE
        for s in range(L):
            e = _dot(ec_ref[0, 0, s], r64_ref[...])
            wts_scr[blk(s), nt:] = jnp.where(row_gi_w == lane_gp, e, 0.0).astype(BF16)
        row_gp = (lax.broadcasted_iota(jnp.int32, (nstate, LANES), 0) % half) // S5_STATE
        lane_go_t = lax.broadcasted_iota(jnp.int32, (nstate, LANES), 1) // S5_GROUP
        for tt in range(L):
            w = _dot(cl_ref[0, 0], r16_ref[:, blk(tt)])
            wy_scr[:, blk(tt)] = jnp.where(row_gp == lane_go_t, w, 0.0).astype(BF16)

    def process(x_ref, y_ref):
        bsz, tok, _ = x_ref.shape
        nc = tok // L
        m = bsz * nc
        flip = lambda i, n: i + d * (n - 1 - 2 * i)
        u = jnp.concatenate(
            [x_ref[:, pl.ds(flip(s, L), nc, stride=L), :].reshape(m, LANES) for s in range(L)], axis=1).astype(BF16)
        s_all = _dot(u, wts_scr[:, nt:])
        nq = nstate // LANES
        nh = nq // 2
        for q in range(nq):
            s_scr[q, 0:m, :] = s_all[:, blk(q)]
        a = a_ref[0, 0]
        a_re = [jnp.broadcast_to(a[:, blk(q)], (bsz, LANES)) for q in range(nh)]
        a_im = [jnp.broadcast_to(a[:, blk(nh + q)], (bsz, LANES)) for q in range(nh)]
        h_re = [h_scr[q] for q in range(nh)]
        h_im = [h_scr[nh + q] for q in range(nh)]
        for c in range(nc):
            rows = pl.ds(flip(c, nc), bsz, stride=nc)
            for q in range(nh):
                hs_scr.at[q][rows, :] = h_re[q]
                hs_scr.at[nh + q][rows, :] = h_im[q]
                s_re = s_scr.at[q][rows, :]
                s_im = s_scr.at[nh + q][rows, :]
                h_re[q], h_im[q] = (a_re[q] * h_re[q] - a_im[q] * h_im[q] + s_re,
                                    a_re[q] * h_im[q] + a_im[q] * h_re[q] + s_im)
        for q in range(nh):
            h_scr[q] = h_re[q]
            h_scr[nh + q] = h_im[q]
        hs = jnp.concatenate([hs_scr[q, 0:m, :] for q in range(nq)], axis=1).astype(BF16)
        for n in range(L // 2):
            cols = slice(2 * n * LANES, 2 * (n + 1) * LANES)
            y = _dot(u[:, :cols.stop], wts_scr[:cols.stop, cols]) + _dot(hs, wy_scr[:, cols])
            for tt in (2 * n, 2 * n + 1):
                y_ref[:, pl.ds(flip(tt, L), nc, stride=L), :] = (
                    y[:, blk(tt - 2 * n)].reshape(bsz, nc, LANES))

    @pl.when(t == 0)
    def _():
        build_weights()
        h_scr[...] = jnp.zeros_like(h_scr)
        process(uc_ref, yc_ref)

    @pl.when(t > 0)
    def _():
        process(ul_ref, yl_ref)


def _s5_scan(u3, kc, ec, cl, a, seq, ctx_len):
    bsz, _, d = u3.shape
    tl = 1024
    ntl = seq // tl
    nb = d // LANES
    n_dir = 2
    ctx_blk = seq // ctx_len
    nstate = a.shape[-1]
    nt = S5_CHUNK * LANES
    r16, r64 = _replicators()

    def lat_idx(dd, t):
        i = jnp.maximum(t - 1, 0)
        return jnp.where(dd == 0, i, ntl - 1 - i)

    per_block = lambda arr: pl.BlockSpec((1, 1) + arr.shape[2:], lambda j, dd, t: (dd, j) + (0,) * (arr.ndim - 2))
    const = lambda arr: pl.BlockSpec(arr.shape, lambda j, dd, t: (0,) * arr.ndim)
    return pl.pallas_call(
        _s5_scan_kernel,
        out_shape=(jax.ShapeDtypeStruct((n_dir, bsz, seq, d), F32),
                   jax.ShapeDtypeStruct((n_dir, bsz, ctx_len, d), F32)),
        grid=(nb, n_dir, ntl + 1),
        in_specs=[
            pl.BlockSpec((bsz, tl, LANES), lambda j, dd, t: (0, lat_idx(dd, t), j)),
            pl.BlockSpec((bsz, ctx_len, LANES), lambda j, dd, t: (0, ctx_blk, j)),
            per_block(kc), per_block(ec), per_block(cl), const(r16), const(r64), per_block(a),
        ],
        out_specs=(
            pl.BlockSpec((None, bsz, tl, LANES), lambda j, dd, t: (dd, 0, lat_idx(dd, t), j)),
            pl.BlockSpec((None, bsz, ctx_len, LANES), lambda j, dd, t: (dd, 0, 0, j)),
        ),
        scratch_shapes=[
            pltpu.VMEM((nstate // LANES, bsz, LANES), F32),
            pltpu.VMEM((nstate // LANES, bsz * tl // S5_CHUNK, LANES), F32),
            pltpu.VMEM((nstate // LANES, bsz * tl // S5_CHUNK, LANES), F32),
            pltpu.VMEM((nt, nt + nstate), BF16),
            pltpu.VMEM((nstate, nt), BF16),
        ],
        compiler_params=_cparams(("arbitrary", "arbitrary", "arbitrary")),
        name="s5_scan",
    )(u3, u3, kc, ec, cl, r16, r64, a)


def _gelu_tanh(x):
    return 0.5 * x * (1.0 + jnp.tanh(math.sqrt(2.0 / math.pi) * (x + 0.044715 * (x * x * x))))


def _s5_glu_kernel(n_src, tiles_per_seq, *refs):
    ylf_ref, ylr_ref, ycf_ref, ycr_ref, mod_ref, dvec_ref, w_ref, b_ref, o_ref = refs[n_src:]
    is_ctx = (pl.program_id(0) % tiles_per_seq) == tiles_per_seq - 1
    y_scan = jnp.where(is_ctx, ycf_ref[0, 0] + ycr_ref[0, 0], ylf_ref[0, 0] + ylr_ref[0, 0])
    h = _stream_tile(refs[:n_src], tiles_per_seq)
    mod = mod_ref[0]
    y = dvec_ref[...] * _normmod(h, mod[GS_M:GS_M + 1], mod[SH_M:SH_M + 1]) + y_scan
    z = _gelu_tanh(y)
    gate = 1.0 / (1.0 + jnp.exp(-(_dot(z.astype(BF16), w_ref[...]) + b_ref[...])))
    o_ref[...] = h + mod[G_M:G_M + 1] * (z * gate)


def _s5_glu(src, y_lat, y_ctx, mod, dvec, glu_w, glu_b, seq):
    d = mod.shape[-1]
    r = mod.shape[0] * TILE
    lat_tiles = seq // TILE
    tps = lat_tiles + 1

    def lat_map(dd):
        return lambda i: (dd, i // tps, jnp.minimum(i % tps, lat_tiles - 1), 0)

    def ctx_map(dd):
        return lambda i: (dd, i // tps, 0, 0)

    row = pl.BlockSpec((TILE, d), lambda i: (i, 0))
    vec = pl.BlockSpec((1, d), lambda i: (0, 0))
    return pl.pallas_call(
        functools.partial(_s5_glu_kernel, len(src), tps),
        out_shape=jax.ShapeDtypeStruct((r, d), F32),
        grid=(r // TILE,),
        in_specs=_stream_specs(src, tps, d) + [
                  pl.BlockSpec((1, 1, TILE, d), lat_map(0)), pl.BlockSpec((1, 1, TILE, d), lat_map(1)),
                  pl.BlockSpec((1, 1, TILE, d), ctx_map(0)), pl.BlockSpec((1, 1, TILE, d), ctx_map(1)),
                  pl.BlockSpec((1, MOD_ROWS, d), lambda i: (i, 0, 0)),
                  vec, pl.BlockSpec((d, d), lambda i: (0, 0)), vec],
        out_specs=row,
        compiler_params=_cparams(("arbitrary",)),
        name="s5_glu",
    )(*src, y_lat, y_lat, y_ctx, y_ctx, mod, dvec.reshape(1, d), glu_w.astype(BF16), glu_b.reshape(1, d))


def _pool_matrices(width, tile):
    mats = np.zeros((len(POOL_WINDOWS), tile, tile), np.float32)
    pos = np.arange(width)
    for g, win in enumerate(POOL_WINDOWS):
        lo = np.clip(pos - win // 2, 0, width)
        hi = np.clip(pos + win // 2, 0, width)
        for base in range(0, tile, width):
            for j in range(width):
                mats[g, base + j, base + lo[j]:base + hi[j]] = 1.0 / float(hi[j] - lo[j])
    return mats


def _pool_kernel(h_ref, mod_ref, a_ref, w_ref, b_ref, sc_ref, o_ref):
    mod = mod_ref[0]
    h = h_ref[...]
    u = _normmod(h, mod[GS_M:GS_M + 1], mod[SH_M:SH_M + 1])
    ub = u.astype(BF16)
    gw = w_ref.shape[-1]
    for g in range(w_ref.shape[0]):
        cols = slice(g * gw, (g + 1) * gw)
        res = _dot(a_ref[0, g], ub[:, cols]) - u[:, cols]
        mixed = _dot(res.astype(BF16), w_ref[g]) + b_ref[:, cols]
        o_ref[:, cols] = h[:, cols] + mod[G_M:G_M + 1, cols] * (mixed * sc_ref[:, cols])


def _pool(h, mod, pool_w, pool_b, pool_scale, seq):
    r, d = h.shape
    tps = seq // TILE + 1
    ng, gw, _ = pool_w.shape
    mats = jnp.asarray(np.stack([_pool_matrices(GRID_W, TILE), _pool_matrices(TILE, TILE)]), BF16)
    row = pl.BlockSpec((TILE, d), lambda i: (i, 0))
    vec = pl.BlockSpec((1, d), lambda i: (0, 0))
    return pl.pallas_call(
        _pool_kernel,
        out_shape=jax.ShapeDtypeStruct((r, d), F32),
        grid=(r // TILE,),
        in_specs=[row, pl.BlockSpec((1, MOD_ROWS, d), lambda i: (i, 0, 0)),
                  pl.BlockSpec((1, ng, TILE, TILE), lambda i: ((i % tps) // (tps - 1), 0, 0, 0)),
                  pl.BlockSpec((ng, gw, gw), lambda i: (0, 0, 0)), vec, vec],
        out_specs=row,
        compiler_params=_cparams(("arbitrary",)),
        name="pool_mixer",
    )(h, mod, mats, pool_w.astype(BF16), pool_b.reshape(1, d), pool_scale.reshape(1, d))


def _dft_tables(n):
    idx = np.arange(n)
    ang = 2.0 * np.pi * ((idx[:, None] * idx[None, :]) % n) / n
    scale = 1.0 / math.sqrt(n)
    return (np.cos(ang) * scale).astype(np.float32), (np.sin(ang) * scale).astype(np.float32)


def _fnet_fold_kernel(cc_ref, sc_ref, w_ref, o_ref):
    w = w_ref[...]
    d = w.shape[-1]
    o_ref[:, :d] = _dot3(cc_ref[...], w).astype(o_ref.dtype)
    o_ref[:, d:] = (-_dot3(sc_ref[...], w)).astype(o_ref.dtype)


def _fnet_fold(fnet_w):
    d = fnet_w.shape[0]
    gw = d // FNET_GROUPS
    cc, sc = _dft_tables(gw)
    sq = pl.BlockSpec((gw, gw), lambda i: (0, 0))
    return pl.pallas_call(
        _fnet_fold_kernel,
        out_shape=jax.ShapeDtypeStruct((d, 2 * d), BF16),
        grid=(FNET_GROUPS,),
        in_specs=[sq, sq, pl.BlockSpec((gw, d), lambda i: (i, 0))],
        out_specs=pl.BlockSpec((gw, 2 * d), lambda i: (i, 0)),
        compiler_params=_cparams(("arbitrary",)),
        name="fnet_fold",
    )(jnp.asarray(cc), jnp.asarray(sc), fnet_w)


def _fnet_proj_kernel(h_ref, mod_ref, g_ref, v_ref):
    for j in range(h_ref.shape[0] // TILE):
        rows = slice(j * TILE, (j + 1) * TILE)
        mod = mod_ref[j]
        u = _normmod(h_ref[rows, :], mod[GS_M:GS_M + 1], mod[SH_M:SH_M + 1])
        v_ref[rows, :] = _dot(u.astype(BF16), g_ref[...]).astype(v_ref.dtype)


def _fnet_proj(h, mod, g):
    r, d = h.shape
    sub = ROW_BLOCK // TILE
    return pl.pallas_call(
        _fnet_proj_kernel,
        out_shape=jax.ShapeDtypeStruct((r, 2 * d), BF16),
        grid=(r // ROW_BLOCK,),
        in_specs=[pl.BlockSpec((ROW_BLOCK, d), lambda i: (i, 0)),
                  pl.BlockSpec((sub, MOD_ROWS, d), lambda i: (i, 0, 0)),
                  pl.BlockSpec((d, 2 * d), lambda i: (0, 0))],
        out_specs=pl.BlockSpec((ROW_BLOCK, 2 * d), lambda i: (i, 0)),
        compiler_params=_cparams(("arbitrary",)),
        name="fnet_proj",
    )(h, mod, g)


def _dft_matrix(n):
    n1 = int(round(math.sqrt(n)))
    assert n1 * n1 == n
    hi = jnp.arange(n1, dtype=jnp.int32)[:, None]
    k = jnp.arange(n, dtype=jnp.int32)[None, :]
    ang_a = (2.0 * math.pi / n1) * ((hi * k) % n1).astype(F32)
    ang_b = (2.0 * math.pi / n) * ((hi * k) % n).astype(F32)
    ca, sa, cb, sb = jnp.cos(ang_a), jnp.sin(ang_a), jnp.cos(ang_b), jnp.sin(ang_b)
    scale = 1.0 / math.sqrt(n)
    c = (ca[:, None] * cb[None] - sa[:, None] * sb[None]).reshape(n, n) * scale
    s = (sa[:, None] * cb[None] + ca[:, None] * sb[None]).reshape(n, n) * scale
    return jnp.concatenate([c, s], axis=1).astype(BF16)


def _fnet_lat_kernel(f_ref, v_ref, h_ref, mod_ref, b_ref, o_ref, acc_ref):
    k = pl.program_id(2)

    @pl.when(k == 0)
    def _():
        acc_ref[...] = jnp.zeros_like(acc_ref)

    acc_ref[...] += _dot(f_ref[...], v_ref[0])

    @pl.when(k == pl.num_programs(2) - 1)
    def _():
        for j in range(h_ref.shape[1] // TILE):
            rows = slice(j * TILE, (j + 1) * TILE)
            o_ref[0, rows, :] = h_ref[0, rows, :] + mod_ref[0, j, G_M:G_M + 1] * (acc_ref[rows, :] + b_ref[...])


def _fnet_lat(dft, v3, h3, mod4, fnet_b, seq):
    bsz, tot, d = h3.shape
    tm, tk = ROW_BLOCK, min(2048, seq)
    kh = seq // tk
    sub = tm // TILE
    return pl.pallas_call(
        _fnet_lat_kernel,
        out_shape=jax.ShapeDtypeStruct((bsz, tot, d), F32),
        grid=(bsz, seq // tm, 2 * kh),
        in_specs=[pl.BlockSpec((tm, tk), lambda b, m, k: (m, k)),
                  pl.BlockSpec((1, tk, d), lambda b, m, k: (b, k % kh, k // kh)),
                  pl.BlockSpec((1, tm, d), lambda b, m, k: (b, m, 0)),
                  pl.BlockSpec((1, sub, MOD_ROWS, d), lambda b, m, k: (b, m, 0, 0)),
                  pl.BlockSpec((1, d), lambda b, m, k: (0, 0))],
        out_specs=pl.BlockSpec((1, tm, d), lambda b, m, k: (b, m, 0)),
        scratch_shapes=[pltpu.VMEM((tm, d), F32)],
        compiler_params=_cparams(("arbitrary", "arbitrary", "arbitrary")),
        name="fnet_dft_latent",
    )(dft, v3, h3, mod4, fnet_b.reshape(1, d))


def _fnet_ctx_kernel(f_ref, v_ref, h_ref, mod_ref, b_ref, prev_ref, o_ref):
    del prev_ref
    n = f_ref.shape[0]
    d = h_ref.shape[-1]
    acc = _dot(f_ref[:, :n], v_ref[0, :, :d]) + _dot(f_ref[:, n:], v_ref[0, :, d:])
    o_ref[0] = h_ref[0] + mod_ref[0, 0, G_M:G_M + 1] * (acc + b_ref[...])


def _fnet_ctx(dftc, v3, h3, mod4, fnet_b, partial, seq):
    bsz, tot, d = h3.shape
    n = tot - seq
    blk = seq // n
    return pl.pallas_call(
        _fnet_ctx_kernel,
        out_shape=jax.ShapeDtypeStruct((bsz, tot, d), F32),
        grid=(bsz,),
        in_specs=[pl.BlockSpec((n, 2 * n), lambda b: (0, 0)),
                  pl.BlockSpec((1, n, 2 * d), lambda b: (b, blk, 0)),
                  pl.BlockSpec((1, n, d), lambda b: (b, blk, 0)),
                  pl.BlockSpec((1, 1, MOD_ROWS, d), lambda b: (b, blk, 0, 0)),
                  pl.BlockSpec((1, d), lambda b: (0, 0)),
                  pl.BlockSpec(memory_space=pl.ANY)],
        out_specs=pl.BlockSpec((1, n, d), lambda b: (b, blk, 0)),
        input_output_aliases={5: 0},
        compiler_params=_cparams(("arbitrary",)),
        name="fnet_dft_context",
    )(dftc, v3, h3, mod4, fnet_b.reshape(1, d), partial)


def _silu(x):
    return x * (1.0 / (1.0 + jnp.exp(-x)))


def _ffn_kernel(h_ref, mod_ref, wg_ref, wu_ref, wd_ref, o_ref, u_scr, acc_ref):
    f = pl.program_id(1)
    sub = h_ref.shape[0] // TILE

    @pl.when(f == 0)
    def _():
        for j in range(sub):
            rows = slice(j * TILE, (j + 1) * TILE)
            mod = mod_ref[j]
            u_scr[rows, :] = _normmod(h_ref[rows, :], mod[GS_F:GS_F + 1], mod[SH_F:SH_F + 1]).astype(BF16)
        acc_ref[...] = jnp.zeros_like(acc_ref)

    u = u_scr[...]
    a = _silu(_dot(u, wg_ref[...])) * _dot(u, wu_ref[...])
    acc_ref[...] += _dot(a.astype(BF16), wd_ref[...])

    @pl.when(f == pl.num_programs(1) - 1)
    def _():
        for j in range(sub):
            rows = slice(j * TILE, (j + 1) * TILE)
            o_ref[rows, :] = h_ref[rows, :] + mod_ref[j, G_F:G_F + 1] * acc_ref[rows, :]


def _ffn(h, mod, w_gate, w_up, w_down):
    r, d = h.shape
    ff = w_gate.shape[1]
    tm, tf = ROW_BLOCK, 256
    sub = tm // TILE
    return pl.pallas_call(
        _ffn_kernel,
        out_shape=jax.ShapeDtypeStruct((r, d), F32),
        grid=(r // tm, ff // tf),
        in_specs=[pl.BlockSpec((tm, d), lambda i, f: (i, 0)),
                  pl.BlockSpec((sub, MOD_ROWS, d), lambda i, f: (i, 0, 0)),
                  pl.BlockSpec((d, tf), lambda i, f: (0, f)),
                  pl.BlockSpec((d, tf), lambda i, f: (0, f)),
                  pl.BlockSpec((tf, d), lambda i, f: (f, 0))],
        out_specs=pl.BlockSpec((tm, d), lambda i, f: (i, 0)),
        scratch_shapes=[pltpu.VMEM((tm, d), BF16), pltpu.VMEM((tm, d), F32)],
        compiler_params=_cparams(("arbitrary", "arbitrary")),
        name="ffn_swiglu",
    )(h, mod, w_gate.astype(BF16), w_up.astype(BF16), w_down.astype(BF16))


MOE_TILE = 2048
MOE_CHUNK = 128


def _router_kernel(tiles_per_seq, route_ctx, h_ref, mod_ref, r_ref, u_ref, cw_ref, slot_ref, slott_ref, cnt_ref):
    nsub = h_ref.shape[0] // TILE
    lane = lax.broadcasted_iota(jnp.int32, (TILE, LANES), 1).astype(F32)
    earlier = jnp.where(lax.broadcasted_iota(jnp.int32, (TILE, TILE), 1)
                        < lax.broadcasted_iota(jnp.int32, (TILE, TILE), 0), 1.0, 0.0).astype(BF16)
    neg = jnp.float32(-jnp.inf)
    count = jnp.zeros((1, LANES), F32)
    for j in range(nsub):
        rows = slice(j * TILE, (j + 1) * TILE)
        mod = mod_ref[j]
        u = _normmod(h_ref[rows, :], mod[GS_F:GS_F + 1], mod[SH_F:SH_F + 1])
        u_ref[rows, :] = u.astype(BF16)
        logits = _dot3(u, r_ref[...])
        logits = jnp.where(lane < N_EXPERTS, logits, neg)
        m1 = jnp.max(logits, axis=-1, keepdims=True)
        i1 = jnp.min(jnp.where(logits == m1, lane, float(LANES)), axis=-1, keepdims=True)
        rest = jnp.where(lane == i1, neg, logits)
        m2 = jnp.max(rest, axis=-1, keepdims=True)
        i2 = jnp.min(jnp.where(rest == m2, lane, float(LANES)), axis=-1, keepdims=True)
        e2 = jnp.exp(m2 - m1)
        w1 = 1.0 / (1.0 + e2)
        w2 = e2 / (1.0 + e2)
        cw = jnp.where(lane == i1, w1, jnp.where(lane == i2, w2, 0.0))
        sel = jnp.where(lane == i1, 1.0, jnp.where(lane == i2, 1.0, 0.0))
        if not route_ctx:
            is_ctx = (pl.program_id(0) * nsub + j) % tiles_per_seq == tiles_per_seq - 1
            cw = jnp.where(is_ctx, 0.0, cw)
            sel = jnp.where(is_ctx, 0.0, sel)
        cw_ref[rows, :] = cw
        slot = jnp.where(sel > 0.0, _dot(earlier, sel.astype(BF16)) + count, -1.0)
        slot_ref[rows, :] = slot
        slott_ref[:, rows] = slot.T[:N_EXPERTS, :]
        count = count + jnp.sum(sel, axis=0, keepdims=True)
    cnt_ref[0] = jnp.broadcast_to(count, (N_EXPERTS, LANES))


def _router(h, mod, router, tiles_per_seq, route_ctx):
    r, d = h.shape
    rp = jnp.zeros((d, LANES), F32).at[:, :N_EXPERTS].set(router)
    sub = MOE_TILE // TILE
    row = lambda w: pl.BlockSpec((MOE_TILE, w), lambda i: (i, 0))
    return pl.pallas_call(
        functools.partial(_router_kernel, tiles_per_seq, route_ctx),
        out_shape=(jax.ShapeDtypeStruct((r, d), BF16), jax.ShapeDtypeStruct((r, LANES), F32),
                   jax.ShapeDtypeStruct((r, LANES), F32), jax.ShapeDtypeStruct((N_EXPERTS, r), F32),
                   jax.ShapeDtypeStruct((r // MOE_TILE, N_EXPERTS, LANES), F32)),
        grid=(r // MOE_TILE,),
        in_specs=[row(d),
                  pl.BlockSpec((sub, MOD_ROWS, d), lambda i: (i, 0, 0)),
                  pl.BlockSpec((d, LANES), lambda i: (0, 0))],
        out_specs=(row(d), row(LANES), row(LANES),
                   pl.BlockSpec((N_EXPERTS, MOE_TILE), lambda i: (0, i)),
                   pl.BlockSpec((1, N_EXPERTS, LANES), lambda i: (i, 0, 0))),
        compiler_params=_cparams(("arbitrary",)),
        name="moe_router",
    )(h, mod, rp)


def _moe_kernel(nch_ref, h_ref, u_ref, cw_ref, slot_ref, slott_ref, mod_ref, wg_ref, wu_ref, wd_ref, o_ref,
                xs_scr, y_scr):
    i = pl.program_id(0)
    e = pl.program_id(1)
    f = pl.program_id(2)
    last_f = pl.num_programs(2) - 1
    tile = u_ref.shape[0]
    nch = nch_ref[i * N_EXPERTS + e]
    block = 4 * MOE_CHUNK

    @pl.when((e == 0) & (f == 0))
    def _():
        o_ref[...] = jnp.zeros_like(o_ref)

    nblk = lax.shift_right_logical(nch, 2)
    tail2 = pl.multiple_of(nblk * block, MOE_CHUNK)
    tail1 = pl.multiple_of(tail2 + (nch & 2) * MOE_CHUNK, MOE_CHUNK)

    def run_block(r0, rn):
        rows = pl.ds(r0, rn)

        @pl.when(f == 0)
        def _():
            rid = (lax.broadcasted_iota(jnp.int32, (rn, 1), 0) + r0).astype(F32)
            pick = jnp.where(slott_ref[pl.ds(e, 1), :] == rid, 1.0, 0.0).astype(BF16)
            xs_scr[rows, :] = _dot(pick, u_ref[...]).astype(BF16)

        xs = xs_scr[rows, :]
        a = _silu(_dot(xs, wg_ref[0])) * _dot(xs, wu_ref[0])
        part = _dot(a.astype(BF16), wd_ref[0])

        @pl.when(f == 0)
        def _():
            y_scr[rows, :] = part

        @pl.when(f > 0)
        def _():
            y_scr[rows, :] += part

    def loop_blocks(fn):
        def body(b, carry):
            fn(pl.multiple_of(b * block, block), block)
            return carry
        lax.fori_loop(0, nblk, body, 0)
        pl.when((nch & 2) != 0)(lambda: fn(tail2, 2 * MOE_CHUNK))
        pl.when((nch & 1) != 0)(lambda: fn(tail1, MOE_CHUNK))

    loop_blocks(run_block)

    @pl.when((f == last_f) & (nch > 0))
    def _():
        mine = lax.broadcasted_iota(jnp.int32, (tile, LANES), 1) == e
        slot_col = jnp.sum(jnp.where(mine, slot_ref[...], 0.0), axis=-1, keepdims=True)
        cw_col = jnp.sum(jnp.where(mine, cw_ref[...], 0.0), axis=-1, keepdims=True)

        def scatter_block(r0, rn):
            cid = (lax.broadcasted_iota(jnp.int32, (1, rn), 1) + r0).astype(F32)
            put = jnp.where(slot_col == cid, 1.0, 0.0).astype(BF16)
            o_ref[...] += cw_col * _dot(put, y_scr[pl.ds(r0, rn), :].astype(BF16))

        loop_blocks(scatter_block)

    @pl.when((e == pl.num_programs(1) - 1) & (f == last_f))
    def _():
        for j in range(tile // TILE):
            rows = slice(j * TILE, (j + 1) * TILE)
            o_ref[rows, :] = h_ref[rows, :] + mod_ref[j, G_F:G_F + 1] * o_ref[rows, :]


def _moe(h, u, cw, slot, slott, counts, mod, w_gate, w_up, w_down):
    r, d = h.shape
    ne, _, ff = w_gate.shape
    tm, tf = MOE_TILE, 512
    sub = tm // TILE
    nch = ((counts[:, 0, :ne].astype(jnp.int32) + (MOE_CHUNK - 1)) // MOE_CHUNK).reshape(-1)
    once = pl.Buffered(1)
    row = lambda w: pl.BlockSpec((tm, w), lambda i, e, f, n: (i, 0), pipeline_mode=once)
    return pl.pallas_call(
        _moe_kernel,
        out_shape=jax.ShapeDtypeStruct((r, d), F32),
        grid_spec=pltpu.PrefetchScalarGridSpec(
            num_scalar_prefetch=1,
            grid=(r // tm, ne, ff // tf),
            in_specs=[row(d), row(d), row(LANES), row(LANES),
                      pl.BlockSpec((ne, tm), lambda i, e, f, n: (0, i), pipeline_mode=once),
                      pl.BlockSpec((sub, MOD_ROWS, d), lambda i, e, f, n: (i, 0, 0)),
                      pl.BlockSpec((1, d, tf), lambda i, e, f, n: (e, 0, f)),
                      pl.BlockSpec((1, d, tf), lambda i, e, f, n: (e, 0, f)),
                      pl.BlockSpec((1, tf, d), lambda i, e, f, n: (e, f, 0))],
            out_specs=pl.BlockSpec((tm, d), lambda i, e, f, n: (i, 0), pipeline_mode=once),
            scratch_shapes=[pltpu.VMEM((tm, d), BF16), pltpu.VMEM((tm, d), F32)]),
        compiler_params=_cparams(("arbitrary", "arbitrary", "arbitrary")),
        name="moe_experts",
    )(nch, h, u, cw, slot, slott, mod, w_gate.astype(BF16), w_up.astype(BF16), w_down.astype(BF16))


def _final_kernel(h_ref, g_ref, o_ref):
    h = h_ref[0]
    inv = lax.rsqrt(jnp.mean(h * h, axis=-1, keepdims=True) + EPS)
    o_ref[0] = h * inv * g_ref[...]


def _final_norm(h3, gain, seq):
    bsz, _, d = h3.shape
    return pl.pallas_call(
        _final_kernel,
        out_shape=jax.ShapeDtypeStruct((bsz, seq, d), F32),
        grid=(bsz, seq // TILE),
        in_specs=[pl.BlockSpec((1, TILE, d), lambda b, t: (b, t, 0)),
                  pl.BlockSpec((1, d), lambda b, t: (0, 0))],
        out_specs=pl.BlockSpec((1, TILE, d), lambda b, t: (b, t, 0)),
        compiler_params=_cparams(("arbitrary", "arbitrary")),
        name="final_norm",
    )(h3, gain.reshape(1, d))


def _tile_mods(mods, gain_mix, gain_ffn, bsz, lat_tiles):
    sh_m, sc_m, g_m, sh_f, sc_f, g_f = (mods[:, i] for i in range(N_MOD))
    rows = jnp.stack([gain_mix * (1.0 + sc_m), sh_m, g_m, gain_ffn * (1.0 + sc_f), sh_f, g_f,
                      jnp.zeros_like(g_f), jnp.zeros_like(g_f)], axis=1)
    lat = jnp.broadcast_to(rows[:bsz, None], (bsz, lat_tiles) + rows.shape[1:])
    ctx = jnp.broadcast_to(rows[bsz:, None], (bsz, 1) + rows.shape[1:])
    return jnp.concatenate([lat, ctx], axis=1).reshape(bsz * (lat_tiles + 1), MOD_ROWS, rows.shape[-1])


def kernel(x, c, ctx, c_ctx, ada_w, ada_b, norm_mix, norm_ffn, norm_final, s5_lambda_re, s5_lambda_im, s5_log_step, s5_b_re, s5_b_im, s5_c_re, s5_c_im, s5_d, s5_glu_w, s5_glu_b, pool_w, pool_b, pool_scale, fnet_w, fnet_b, ffn_w_gate, ffn_w_up, ffn_w_down, moe_router, moe_w_gate, moe_w_up, moe_w_down):
    bsz, seq, d = x.shape
    ctx_len = ctx.shape[1]
    depth = ada_w.shape[0]
    assert ctx_len == TILE and seq % ROW_BLOCK == 0 and d % LANES == 0
    tot = seq + ctx_len
    lat_tiles = seq // TILE
    tps = lat_tiles + 1
    r = bsz * tot
    assert r % ROW_BLOCK == 0

    cond = jnp.zeros((16, d), F32).at[:bsz].set(c).at[bsz].set(c_ctx)
    mods_all = _ada_all(cond, ada_w, ada_b)[:, :bsz + 1].reshape(depth, bsz + 1, N_MOD, d)

    h = None
    for i in range(depth):
        kind, j = i % 3, i // 3
        mod = _tile_mods(mods_all[i], norm_mix[i], norm_ffn[i], bsz, lat_tiles)
        if kind == 0:
            src = (x, ctx) if h is None else (h,)
            u = _s5_pre(src, mod, tps)
            kc, ec, cl, a = _s5_params(s5_lambda_re[j], s5_lambda_im[j], s5_log_step[j],
                                       s5_b_re[j], s5_b_im[j], s5_c_re[j], s5_c_im[j])
            y_lat, y_ctx = _s5_scan(u.reshape(bsz, tot, d), kc, ec, cl, a, seq, ctx_len)
            h = _s5_glu(src, y_lat, y_ctx, mod, s5_d[j], s5_glu_w[j], s5_glu_b[j], seq)
        elif kind == 1:
            h = _pool(h, mod, pool_w[j], pool_b[j], pool_scale[j], seq)
        else:
            g = _fnet_fold(fnet_w[j])
            v3 = _fnet_proj(h, mod, g).reshape(bsz, tot, 2 * d)
            h3 = h.reshape(bsz, tot, d)
            mod4 = mod.reshape(bsz, tps, MOD_ROWS, d)
            part = _fnet_lat(_dft_matrix(seq), v3, h3, mod4, fnet_b[j], seq)
            cc, sc = _dft_tables(ctx_len)
            dftc = jnp.asarray(np.concatenate([cc, sc], axis=1), BF16)
            h = _fnet_ctx(dftc, v3, h3, mod4, fnet_b[j], part, seq).reshape(r, d)
        kk = i // 2
        if i % 2 == 0:
            h = _ffn(h, mod, ffn_w_gate[kk], ffn_w_up[kk], ffn_w_down[kk])
        else:
            ub, cw, slot, slott, counts = _router(h, mod, moe_router[kk], tps, route_ctx=i < depth - 1)
            h = _moe(h, ub, cw, slot, slott, counts, mod, moe_w_gate[kk], moe_w_up[kk], moe_w_down[kk])
    return _final_norm(h.reshape(bsz, tot, d), norm_final, seq)
```

```python
import functools
import math

import numpy as np
import jax
import jax.numpy as jnp
from jax import lax
from jax.experimental import pallas as pl
from jax.experimental.pallas import tpu as pltpu

F32 = jnp.float32
BF16 = jnp.bfloat16

EPS = 1e-6
GRID_W = 64
N_MOD = 6
S5_GROUP = 16
S5_STATE = 64
S5_CHUNK = 16
POOL_WINDOWS = (2, 4, 8, 16)
FNET_GROUPS = 4
N_EXPERTS = 8

LANES = 128
TILE = 256
ROW_BLOCK = 1024
VMEM_LIMIT = 56 * 1024 * 1024


def _cparams(sem, vmem=VMEM_LIMIT):
    return pltpu.CompilerParams(dimension_semantics=sem, vmem_limit_bytes=vmem)


def _dot(a, b):
    return jnp.dot(a, b, preferred_element_type=F32)


def _split(a):
    hi = a.astype(BF16)
    lo = (a - hi.astype(F32)).astype(BF16)
    return hi, lo


def _dot3(a, b):
    a_hi, a_lo = _split(a)
    b_hi, b_lo = _split(b)
    return _dot(a_hi, b_hi) + (_dot(a_lo, b_hi) + _dot(a_hi, b_lo))


def _normmod(h, gs, sh):
    inv = lax.rsqrt(jnp.mean(h * h, axis=-1, keepdims=True) + EPS)
    return h * inv * gs + sh


GS_M, SH_M, G_M, GS_F, SH_F, G_F = range(6)
MOD_ROWS = 8


def _ada_kernel(c_ref, w_ref, b_ref, o_ref):
    c = c_ref[...]
    s = c * (1.0 / (1.0 + jnp.exp(-c)))
    o_ref[0] = _dot3(s, w_ref[0]) + b_ref[0]


def _ada_all(cond, ada_w, ada_b):
    depth, d, n = ada_w.shape
    rows = cond.shape[0]
    tn = 1536
    return pl.pallas_call(
        _ada_kernel,
        out_shape=jax.ShapeDtypeStruct((depth, rows, n), F32),
        grid=(depth, n // tn),
        in_specs=[
            pl.BlockSpec((rows, d), lambda i, j: (0, 0)),
            pl.BlockSpec((1, d, tn), lambda i, j: (i, 0, j)),
            pl.BlockSpec((1, 1, tn), lambda i, j: (i, 0, j)),
        ],
        out_specs=pl.BlockSpec((1, rows, tn), lambda i, j: (i, 0, j)),
        compiler_params=_cparams(("arbitrary", "arbitrary")),
        name="ada_params",
    )(cond, ada_w, ada_b.reshape(depth, 1, n))


def _stream_specs(src, tps, d):
    if len(src) == 1:
        return [pl.BlockSpec((TILE, d), lambda i: (i, 0))]
    return [pl.BlockSpec((None, TILE, d), lambda i: (i // tps, jnp.minimum(i % tps, tps - 2), 0)),
            pl.BlockSpec((None, TILE, d), lambda i: (i // tps, 0, 0))]


def _stream_tile(refs, tps):
    if len(refs) == 1:
        return refs[0][...]
    is_ctx = (pl.program_id(0) % tps) == tps - 1
    return jnp.where(is_ctx, refs[1][...], refs[0][...])


def _s5_pre_kernel(n_src, tps, *refs):
    mod_ref, u_ref = refs[n_src:]
    mod = mod_ref[0]
    u_ref[...] = _normmod(_stream_tile(refs[:n_src], tps), mod[GS_M:GS_M + 1], mod[SH_M:SH_M + 1])


def _s5_pre(src, mod, tps):
    r, _, d = mod.shape
    r *= TILE
    return pl.pallas_call(
        functools.partial(_s5_pre_kernel, len(src), tps),
        out_shape=jax.ShapeDtypeStruct((r, d), F32),
        grid=(r // TILE,),
        in_specs=_stream_specs(src, tps, d) + [pl.BlockSpec((1, MOD_ROWS, d), lambda i: (i, 0, 0))],
        out_specs=pl.BlockSpec((TILE, d), lambda i: (i, 0)),
        compiler_params=_cparams(("arbitrary",)),
        name="s5_pre",
    )(*src, mod)


def _s5_params(lam_re, lam_im, log_step, b_re, b_im, c_re, c_im):
    hp = lax.Precision.HIGHEST
    L = S5_CHUNK
    n_dir, g, p = lam_re.shape
    gb = LANES // S5_GROUP
    nb = g // gb
    hch = S5_GROUP
    step = jnp.exp(log_step)[..., None]
    ar, ai = lam_re * step, lam_im * step
    k = jnp.arange(L + 1, dtype=F32)[None, :, None, None]
    mag = jnp.exp(ar[:, None] * k)
    ang = ai[:, None] * k
    pr, pi = mag * jnp.cos(ang), mag * jnp.sin(ang)
    lbr, lbi = pr[:, 1] - 1.0, pi[:, 1]
    den = lam_re * lam_re + lam_im * lam_im
    qr = (lbr * lam_re + lbi * lam_im) / den
    qi = (lbi * lam_re - lbr * lam_im) / den
    bbr = qr[..., None] * b_re - qi[..., None] * b_im
    bbi = qr[..., None] * b_im + qi[..., None] * b_re
    er = pr[..., None] * bbr[:, None] - pi[..., None] * bbi[:, None]
    ei = pr[..., None] * bbi[:, None] + pi[..., None] * bbr[:, None]
    kmat = (jnp.einsum('dgop,dkgpi->dgiko', c_re, er[:, :L], precision=hp)
            - jnp.einsum('dgop,dkgpi->dgiko', c_im, ei[:, :L], precision=hp))
    kc = kmat.reshape(n_dir, nb, gb * hch, L * hch)
    es = jnp.stack([er[:, L - 1::-1], ei[:, L - 1::-1]], axis=0)
    ec = jnp.transpose(es, (1, 3, 2, 5, 0, 4)).reshape(n_dir, nb, gb, L, hch, 2, p)
    ec = jnp.transpose(ec, (0, 1, 3, 2, 4, 5, 6)).reshape(n_dir, nb, L, gb * hch, 2 * p)
    clr = c_re[:, None] * pr[:, 1:, :, None, :] - c_im[:, None] * pi[:, 1:, :, None, :]
    cli = c_re[:, None] * pi[:, 1:, :, None, :] + c_im[:, None] * pr[:, 1:, :, None, :]
    cs = jnp.stack([clr, -cli], axis=0)
    cl = jnp.transpose(cs, (1, 3, 0, 5, 2, 4)).reshape(n_dir, nb, gb, 2, p, L, hch)
    cl = jnp.transpose(cl, (0, 1, 3, 2, 4, 5, 6)).reshape(n_dir, nb, 2 * gb * p, L * hch)
    a = jnp.stack([pr[:, L].reshape(n_dir, nb, 1, gb * p), pi[:, L].reshape(n_dir, nb, 1, gb * p)], axis=3)
    return kc.astype(BF16), ec.astype(BF16), cl.astype(BF16), a.reshape(n_dir, nb, 1, 2 * gb * p)


def _replicators():
    gb = LANES // S5_GROUP
    r16 = np.zeros((S5_CHUNK, S5_GROUP, S5_CHUNK, gb, S5_GROUP), np.float32)
    for k in range(S5_CHUNK):
        for o in range(S5_GROUP):
            r16[k, o, k, :, o] = 1.0
    r64 = np.zeros((2, S5_STATE, 2, gb, S5_STATE), np.float32)
    for part in range(2):
        for q in range(S5_STATE):
            r64[part, q, part, :, q] = 1.0
    return (jnp.asarray(r16.reshape(S5_CHUNK * S5_GROUP, S5_CHUNK * LANES), BF16),
            jnp.asarray(r64.reshape(2 * S5_STATE, 2 * gb * S5_STATE), BF16))


def _s5_scan_kernel(ul_ref, uc_ref, kc_ref, ec_ref, cl_ref, r16_ref, r64_ref, a_ref, yl_ref, yc_ref,
                    h_scr, s_scr, hs_scr, wts_scr, wy_scr):
    d = pl.program_id(1)
    t = pl.program_id(2)
    L = S5_CHUNK
    nstate = a_ref.shape[-1]
    half = nstate // 2
    nt = L * LANES
    blk = lambda i: slice(i * LANES, (i + 1) * LANES)

    def build_weights():
        row_gi = lax.broadcasted_iota(jnp.int32, (LANES, LANES), 0) // S5_GROUP
        lane_go = lax.broadcasted_iota(jnp.int32, (LANES, LANES), 1) // S5_GROUP
        toep = _dot(kc_ref[0, 0], r16_ref[...])
        zero = jnp.zeros((LANES, LANES), BF16)
        for lag in range(L):
            piece = jnp.where(row_gi == lane_go, toep[:, blk(lag)], 0.0).astype(BF16)
            for s in range(L - lag):
                wts_scr[blk(s), blk(s + lag)] = piece
        for tt in range(0, L, 2):
            wts_scr[blk(tt + 1), blk(tt)] = zero
        row_gi_w = lax.broadcasted_iota(jnp.int32, (LANES, nstate), 0) // S5_GROUP
        lane_gp = (lax.broadcasted_iota(jnp.int32, (LANES, nstate), 1) % half) // S5_STATE
        for s in range(L):
            e = _dot(ec_ref[0, 0, s], r64_ref[...])
            wts_scr[blk(s), nt:] = jnp.where(row_gi_w == lane_gp, e, 0.0).astype(BF16)
        row_gp = (lax.broadcasted_iota(jnp.int32, (nstate, LANES), 0) % half) // S5_STATE
        lane_go_t = lax.broadcasted_iota(jnp.int32, (nstate, LANES), 1) // S5_GROUP
        for tt in range(L):
            w = _dot(cl_ref[0, 0], r16_ref[:, blk(tt)])
            wy_scr[:, blk(tt)] = jnp.where(row_gp == lane_go_t, w, 0.0).astype(BF16)

    def process(x_ref, y_ref):
        bsz, tok, _ = x_ref.shape
        nc = tok // L
        m = bsz * nc
        flip = lambda i, n: i + d * (n - 1 - 2 * i)
        u = jnp.concatenate(
            [x_ref[:, pl.ds(flip(s, L), nc, stride=L), :].reshape(m, LANES) for s in range(L)], axis=1).astype(BF16)
        s_all = _dot(u, wts_scr[:, nt:])
        nq = nstate // LANES
        nh = nq // 2
        for q in range(nq):
            s_scr[q, 0:m, :] = s_all[:, blk(q)]
        a = a_ref[0, 0]
        a_re = [jnp.broadcast_to(a[:, blk(q)], (bsz, LANES)) for q in range(nh)]
        a_im = [jnp.broadcast_to(a[:, blk(nh + q)], (bsz, LANES)) for q in range(nh)]
        h_re = [h_scr[q] for q in range(nh)]
        h_im = [h_scr[nh + q] for q in range(nh)]
        for c in range(nc):
            rows = pl.ds(flip(c, nc), bsz, stride=nc)
            for q in range(nh):
                hs_scr.at[q][rows, :] = h_re[q]
                hs_scr.at[nh + q][rows, :] = h_im[q]
                s_re = s_scr.at[q][rows, :]
                s_im = s_scr.at[nh + q][rows, :]
                h_re[q], h_im[q] = (a_re[q] * h_re[q] - a_im[q] * h_im[q] + s_re,
                                    a_re[q] * h_im[q] + a_im[q] * h_re[q] + s_im)
        for q in range(nh):
            h_scr[q] = h_re[q]
            h_scr[nh + q] = h_im[q]
        hs = jnp.concatenate([hs_scr[q, 0:m, :] for q in range(nq)], axis=1).astype(BF16)
        for n in range(L // 2):
            cols = slice(2 * n * LANES, 2 * (n + 1) * LANES)
            y = _dot(u[:, :cols.stop], wts_scr[:cols.stop, cols]) + _dot(hs, wy_scr[:, cols])
            for tt in (2 * n, 2 * n + 1):
                y_ref[:, pl.ds(flip(tt, L), nc, stride=L), :] = (
                    y[:, blk(tt - 2 * n)].reshape(bsz, nc, LANES))

    @pl.when(t == 0)
    def _():
        build_weights()
        h_scr[...] = jnp.zeros_like(h_scr)
        process(uc_ref, yc_ref)

    @pl.when(t > 0)
    def _():
        process(ul_ref, yl_ref)


def _s5_scan(u3, kc, ec, cl, a, seq, ctx_len):
    bsz, _, d = u3.shape
    tl = 1024
    ntl = seq // tl
    nb = d // LANES
    n_dir = 2
    ctx_blk = seq // ctx_len
    nstate = a.shape[-1]
    nt = S5_CHUNK * LANES
    r16, r64 = _replicators()

    def lat_idx(dd, t):
        i = jnp.maximum(t - 1, 0)
        return jnp.where(dd == 0, i, ntl - 1 - i)

    per_block = lambda arr: pl.BlockSpec((1, 1) + arr.shape[2:], lambda j, dd, t: (dd, j) + (0,) * (arr.ndim - 2))
    const = lambda arr: pl.BlockSpec(arr.shape, lambda j, dd, t: (0,) * arr.ndim)
    return pl.pallas_call(
        _s5_scan_kernel,
        out_shape=(jax.ShapeDtypeStruct((n_dir, bsz, seq, d), F32),
                   jax.ShapeDtypeStruct((n_dir, bsz, ctx_len, d), F32)),
        grid=(nb, n_dir, ntl + 1),
        in_specs=[
            pl.BlockSpec((bsz, tl, LANES), lambda j, dd, t: (0, lat_idx(dd, t), j)),
            pl.BlockSpec((bsz, ctx_len, LANES), lambda j, dd, t: (0, ctx_blk, j)),
            per_block(kc), per_block(ec), per_block(cl), const(r16), const(r64), per_block(a),
        ],
        out_specs=(
            pl.BlockSpec((None, bsz, tl, LANES), lambda j, dd, t: (dd, 0, lat_idx(dd, t), j)),
            pl.BlockSpec((None, bsz, ctx_len, LANES), lambda j, dd, t: (dd, 0, 0, j)),
        ),
        scratch_shapes=[
            pltpu.VMEM((nstate // LANES, bsz, LANES), F32),
            pltpu.VMEM((nstate // LANES, bsz * tl // S5_CHUNK, LANES), F32),
            pltpu.VMEM((nstate // LANES, bsz * tl // S5_CHUNK, LANES), F32),
            pltpu.VMEM((nt, nt + nstate), BF16),
            pltpu.VMEM((nstate, nt), BF16),
        ],
        compiler_params=_cparams(("arbitrary", "arbitrary", "arbitrary")),
        name="s5_scan",
    )(u3, u3, kc, ec, cl, r16, r64, a)


def _gelu_tanh(x):
    return 0.5 * x * (1.0 + jnp.tanh(math.sqrt(2.0 / math.pi) * (x + 0.044715 * (x * x * x))))


def _s5_glu_kernel(n_src, tiles_per_seq, *refs):
    ylf_ref, ylr_ref, ycf_ref, ycr_ref, mod_ref, dvec_ref, w_ref, b_ref, o_ref = refs[n_src:]
    is_ctx = (pl.program_id(0) % tiles_per_seq) == tiles_per_seq - 1
    y_scan = jnp.where(is_ctx, ycf_ref[0, 0] + ycr_ref[0, 0], ylf_ref[0, 0] + ylr_ref[0, 0])
    h = _stream_tile(refs[:n_src], tiles_per_seq)
    mod = mod_ref[0]
    y = dvec_ref[...] * _normmod(h, mod[GS_M:GS_M + 1], mod[SH_M:SH_M + 1]) + y_scan
    z = _gelu_tanh(y)
    gate = 1.0 / (1.0 + jnp.exp(-(_dot(z.astype(BF16), w_ref[...]) + b_ref[...])))
    o_ref[...] = h + mod[G_M:G_M + 1] * (z * gate)


def _s5_glu(src, y_lat, y_ctx, mod, dvec, glu_w, glu_b, seq):
    d = mod.shape[-1]
    r = mod.shape[0] * TILE
    lat_tiles = seq // TILE
    tps = lat_tiles + 1

    def lat_map(dd):
        return lambda i: (dd, i // tps, jnp.minimum(i % tps, lat_tiles - 1), 0)

    def ctx_map(dd):
        return lambda i: (dd, i // tps, 0, 0)

    row = pl.BlockSpec((TILE, d), lambda i: (i, 0))
    vec = pl.BlockSpec((1, d), lambda i: (0, 0))
    return pl.pallas_call(
        functools.partial(_s5_glu_kernel, len(src), tps),
        out_shape=jax.ShapeDtypeStruct((r, d), F32),
        grid=(r // TILE,),
        in_specs=_stream_specs(src, tps, d) + [
                  pl.BlockSpec((1, 1, TILE, d), lat_map(0)), pl.BlockSpec((1, 1, TILE, d), lat_map(1)),
                  pl.BlockSpec((1, 1, TILE, d), ctx_map(0)), pl.BlockSpec((1, 1, TILE, d), ctx_map(1)),
                  pl.BlockSpec((1, MOD_ROWS, d), lambda i: (i, 0, 0)),
                  vec, pl.BlockSpec((d, d), lambda i: (0, 0)), vec],
        out_specs=row,
        compiler_params=_cparams(("arbitrary",)),
        name="s5_glu",
    )(*src, y_lat, y_lat, y_ctx, y_ctx, mod, dvec.reshape(1, d), glu_w.astype(BF16), glu_b.reshape(1, d))


def _pool_matrices(width, tile):
    mats = np.zeros((len(POOL_WINDOWS), tile, tile), np.float32)
    pos = np.arange(width)
    for g, win in enumerate(POOL_WINDOWS):
        lo = np.clip(pos - win // 2, 0, width)
        hi = np.clip(pos + win // 2, 0, width)
        for base in range(0, tile, width):
            for j in range(width):
                mats[g, base + j, base + lo[j]:base + hi[j]] = 1.0 / float(hi[j] - lo[j])
    return mats


def _pool_kernel(h_ref, mod_ref, a_ref, w_ref, b_ref, sc_ref, o_ref):
    mod = mod_ref[0]
    h = h_ref[...]
    u = _normmod(h, mod[GS_M:GS_M + 1], mod[SH_M:SH_M + 1])
    ub = u.astype(BF16)
    gw = w_ref.shape[-1]
    for g in range(w_ref.shape[0]):
        cols = slice(g * gw, (g + 1) * gw)
        res = _dot(a_ref[0, g], ub[:, cols]) - u[:, cols]
        mixed = _dot(res.astype(BF16), w_ref[g]) + b_ref[:, cols]
        o_ref[:, cols] = h[:, cols] + mod[G_M:G_M + 1, cols] * (mixed * sc_ref[:, cols])


def _pool(h, mod, pool_w, pool_b, pool_scale, seq):
    r, d = h.shape
    tps = seq // TILE + 1
    ng, gw, _ = pool_w.shape
    mats = jnp.asarray(np.stack([_pool_matrices(GRID_W, TILE), _pool_matrices(TILE, TILE)]), BF16)
    row = pl.BlockSpec((TILE, d), lambda i: (i, 0))
    vec = pl.BlockSpec((1, d), lambda i: (0, 0))
    return pl.pallas_call(
        _pool_kernel,
        out_shape=jax.ShapeDtypeStruct((r, d), F32),
        grid=(r // TILE,),
        in_specs=[row, pl.BlockSpec((1, MOD_ROWS, d), lambda i: (i, 0, 0)),
                  pl.BlockSpec((1, ng, TILE, TILE), lambda i: ((i % tps) // (tps - 1), 0, 0, 0)),
                  pl.BlockSpec((ng, gw, gw), lambda i: (0, 0, 0)), vec, vec],
        out_specs=row,
        compiler_params=_cparams(("arbitrary",)),
        name="pool_mixer",
    )(h, mod, mats, pool_w.astype(BF16), pool_b.reshape(1, d), pool_scale.reshape(1, d))


def _dft_tables(n):
    idx = np.arange(n)
    ang = 2.0 * np.pi * ((idx[:, None] * idx[None, :]) % n) / n
    scale = 1.0 / math.sqrt(n)
    return (np.cos(ang) * scale).astype(np.float32), (np.sin(ang) * scale).astype(np.float32)


def _fnet_fold_kernel(cc_ref, sc_ref, w_ref, o_ref):
    w = w_ref[...]
    d = w.shape[-1]
    o_ref[:, :d] = _dot3(cc_ref[...], w).astype(o_ref.dtype)
    o_ref[:, d:] = (-_dot3(sc_ref[...], w)).astype(o_ref.dtype)


def _fnet_fold(fnet_w):
    d = fnet_w.shape[0]
    gw = d // FNET_GROUPS
    cc, sc = _dft_tables(gw)
    sq = pl.BlockSpec((gw, gw), lambda i: (0, 0))
    return pl.pallas_call(
        _fnet_fold_kernel,
        out_shape=jax.ShapeDtypeStruct((d, 2 * d), BF16),
        grid=(FNET_GROUPS,),
        in_specs=[sq, sq, pl.BlockSpec((gw, d), lambda i: (i, 0))],
        out_specs=pl.BlockSpec((gw, 2 * d), lambda i: (i, 0)),
        compiler_params=_cparams(("arbitrary",)),
        name="fnet_fold",
    )(jnp.asarray(cc), jnp.asarray(sc), fnet_w)


def _fnet_proj_kernel(h_ref, mod_ref, g_ref, v_ref):
    for j in range(h_ref.shape[0] // TILE):
        rows = slice(j * TILE, (j + 1) * TILE)
        mod = mod_ref[j]
        u = _normmod(h_ref[rows, :], mod[GS_M:GS_M + 1], mod[SH_M:SH_M + 1])
        v_ref[rows, :] = _dot(u.astype(BF16), g_ref[...]).astype(v_ref.dtype)


def _fnet_proj(h, mod, g):
    r, d = h.shape
    sub = ROW_BLOCK // TILE
    return pl.pallas_call(
        _fnet_proj_kernel,
        out_shape=jax.ShapeDtypeStruct((r, 2 * d), BF16),
        grid=(r // ROW_BLOCK,),
        in_specs=[pl.BlockSpec((ROW_BLOCK, d), lambda i: (i, 0)),
                  pl.BlockSpec((sub, MOD_ROWS, d), lambda i: (i, 0, 0)),
                  pl.BlockSpec((d, 2 * d), lambda i: (0, 0))],
        out_specs=pl.BlockSpec((ROW_BLOCK, 2 * d), lambda i: (i, 0)),
        compiler_params=_cparams(("arbitrary",)),
        name="fnet_proj",
    )(h, mod, g)


def _dft_matrix(n):
    n1 = int(round(math.sqrt(n)))
    assert n1 * n1 == n
    hi = jnp.arange(n1, dtype=jnp.int32)[:, None]
    k = jnp.arange(n, dtype=jnp.int32)[None, :]
    ang_a = (2.0 * math.pi / n1) * ((hi * k) % n1).astype(F32)
    ang_b = (2.0 * math.pi / n) * ((hi * k) % n).astype(F32)
    ca, sa, cb, sb = jnp.cos(ang_a), jnp.sin(ang_a), jnp.cos(ang_b), jnp.sin(ang_b)
    scale = 1.0 / math.sqrt(n)
    c = (ca[:, None] * cb[None] - sa[:, None] * sb[None]).reshape(n, n) * scale
    s = (sa[:, None] * cb[None] + ca[:, None] * sb[None]).reshape(n, n) * scale
    return jnp.concatenate([c, s], axis=1).astype(BF16)


def _fnet_lat_kernel(f_ref, v_ref, h_ref, mod_ref, b_ref, o_ref, acc_ref):
    k = pl.program_id(2)

    @pl.when(k == 0)
    def _():
        acc_ref[...] = jnp.zeros_like(acc_ref)

    acc_ref[...] += _dot(f_ref[...], v_ref[0])

    @pl.when(k == pl.num_programs(2) - 1)
    def _():
        for j in range(h_ref.shape[1] // TILE):
            rows = slice(j * TILE, (j + 1) * TILE)
            o_ref[0, rows, :] = h_ref[0, rows, :] + mod_ref[0, j, G_M:G_M + 1] * (acc_ref[rows, :] + b_ref[...])


def _fnet_lat(dft, v3, h3, mod4, fnet_b, seq):
    bsz, tot, d = h3.shape
    tm, tk = ROW_BLOCK, min(2048, seq)
    kh = seq // tk
    sub = tm // TILE
    return pl.pallas_call(
        _fnet_lat_kernel,
        out_shape=jax.ShapeDtypeStruct((bsz, tot, d), F32),
        grid=(bsz, seq // tm, 2 * kh),
        in_specs=[pl.BlockSpec((tm, tk), lambda b, m, k: (m, k)),
                  pl.BlockSpec((1, tk, d), lambda b, m, k: (b, k % kh, k // kh)),
                  pl.BlockSpec((1, tm, d), lambda b, m, k: (b, m, 0)),
                  pl.BlockSpec((1, sub, MOD_ROWS, d), lambda b, m, k: (b, m, 0, 0)),
                  pl.BlockSpec((1, d), lambda b, m, k: (0, 0))],
        out_specs=pl.BlockSpec((1, tm, d), lambda b, m, k: (b, m, 0)),
        scratch_shapes=[pltpu.VMEM((tm, d), F32)],
        compiler_params=_cparams(("arbitrary", "arbitrary", "arbitrary")),
        name="fnet_dft_latent",
    )(dft, v3, h3, mod4, fnet_b.reshape(1, d))


def _fnet_ctx_kernel(f_ref, v_ref, h_ref, mod_ref, b_ref, prev_ref, o_ref):
    del prev_ref
    n = f_ref.shape[0]
    d = h_ref.shape[-1]
    acc = _dot(f_ref[:, :n], v_ref[0, :, :d]) + _dot(f_ref[:, n:], v_ref[0, :, d:])
    o_ref[0] = h_ref[0] + mod_ref[0, 0, G_M:G_M + 1] * (acc + b_ref[...])


def _fnet_ctx(dftc, v3, h3, mod4, fnet_b, partial, seq):
    bsz, tot, d = h3.shape
    n = tot - seq
    blk = seq // n
    return pl.pallas_call(
        _fnet_ctx_kernel,
        out_shape=jax.ShapeDtypeStruct((bsz, tot, d), F32),
        grid=(bsz,),
        in_specs=[pl.BlockSpec((n, 2 * n), lambda b: (0, 0)),
                  pl.BlockSpec((1, n, 2 * d), lambda b: (b, blk, 0)),
                  pl.BlockSpec((1, n, d), lambda b: (b, blk, 0)),
                  pl.BlockSpec((1, 1, MOD_ROWS, d), lambda b: (b, blk, 0, 0)),
                  pl.BlockSpec((1, d), lambda b: (0, 0)),
                  pl.BlockSpec(memory_space=pl.ANY)],
        out_specs=pl.BlockSpec((1, n, d), lambda b: (b, blk, 0)),
        input_output_aliases={5: 0},
        compiler_params=_cparams(("arbitrary",)),
        name="fnet_dft_context",
    )(dftc, v3, h3, mod4, fnet_b.reshape(1, d), partial)


def _silu(x):
    return x * (1.0 / (1.0 + jnp.exp(-x)))


def _ffn_kernel(h_ref, mod_ref, wg_ref, wu_ref, wd_ref, o_ref, u_scr, acc_ref):
    f = pl.program_id(1)
    sub = h_ref.shape[0] // TILE

    @pl.when(f == 0)
    def _():
        for j in range(sub):
            rows = slice(j * TILE, (j + 1) * TILE)
            mod = mod_ref[j]
            u_scr[rows, :] = _normmod(h_ref[rows, :], mod[GS_F:GS_F + 1], mod[SH_F:SH_F + 1]).astype(BF16)
        acc_ref[...] = jnp.zeros_like(acc_ref)

    u = u_scr[...]
    a = _silu(_dot(u, wg_ref[...])) * _dot(u, wu_ref[...])
    acc_ref[...] += _dot(a.astype(BF16), wd_ref[...])

    @pl.when(f == pl.num_programs(1) - 1)
    def _():
        for j in range(sub):
            rows = slice(j * TILE, (j + 1) * TILE)
            o_ref[rows, :] = h_ref[rows, :] + mod_ref[j, G_F:G_F + 1] * acc_ref[rows, :]


def _ffn(h, mod, w_gate, w_up, w_down):
    r, d = h.shape
    ff = w_gate.shape[1]
    tm, tf = ROW_BLOCK, 256
    sub = tm // TILE
    return pl.pallas_call(
        _ffn_kernel,
        out_shape=jax.ShapeDtypeStruct((r, d), F32),
        grid=(r // tm, ff // tf),
        in_specs=[pl.BlockSpec((tm, d), lambda i, f: (i, 0)),
                  pl.BlockSpec((sub, MOD_ROWS, d), lambda i, f: (i, 0, 0)),
                  pl.BlockSpec((d, tf), lambda i, f: (0, f)),
                  pl.BlockSpec((d, tf), lambda i, f: (0, f)),
                  pl.BlockSpec((tf, d), lambda i, f: (f, 0))],
        out_specs=pl.BlockSpec((tm, d), lambda i, f: (i, 0)),
        scratch_shapes=[pltpu.VMEM((tm, d), BF16), pltpu.VMEM((tm, d), F32)],
        compiler_params=_cparams(("arbitrary", "arbitrary")),
        name="ffn_swiglu",
    )(h, mod, w_gate.astype(BF16), w_up.astype(BF16), w_down.astype(BF16))


MOE_TILE = 2048
MOE_CHUNK = 128


def _router_kernel(tiles_per_seq, route_ctx, h_ref, mod_ref, r_ref, u_ref, cw_ref, slot_ref, slott_ref, cnt_ref):
    nsub = h_ref.shape[0] // TILE
    lane = lax.broadcasted_iota(jnp.int32, (TILE, LANES), 1).astype(F32)
    earlier = jnp.where(lax.broadcasted_iota(jnp.int32, (TILE, TILE), 1)
                        < lax.broadcasted_iota(jnp.int32, (TILE, TILE), 0), 1.0, 0.0).astype(BF16)
    neg = jnp.float32(-jnp.inf)
    count = jnp.zeros((1, LANES), F32)
    for j in range(nsub):
        rows = slice(j * TILE, (j + 1) * TILE)
        mod = mod_ref[j]
        u = _normmod(h_ref[rows, :], mod[GS_F:GS_F + 1], mod[SH_F:SH_F + 1])
        u_ref[rows, :] = u.astype(BF16)
        logits = _dot3(u, r_ref[...])
        logits = jnp.where(lane < N_EXPERTS, logits, neg)
        m1 = jnp.max(logits, axis=-1, keepdims=True)
        i1 = jnp.min(jnp.where(logits == m1, lane, float(LANES)), axis=-1, keepdims=True)
        rest = jnp.where(lane == i1, neg, logits)
        m2 = jnp.max(rest, axis=-1, keepdims=True)
        i2 = jnp.min(jnp.where(rest == m2, lane, float(LANES)), axis=-1, keepdims=True)
        e2 = jnp.exp(m2 - m1)
        w1 = 1.0 / (1.0 + e2)
        w2 = e2 / (1.0 + e2)
        cw = jnp.where(lane == i1, w1, jnp.where(lane == i2, w2, 0.0))
        sel = jnp.where(lane == i1, 1.0, jnp.where(lane == i2, 1.0, 0.0))
        if not route_ctx:
            is_ctx = (pl.program_id(0) * nsub + j) % tiles_per_seq == tiles_per_seq - 1
            cw = jnp.where(is_ctx, 0.0, cw)
            sel = jnp.where(is_ctx, 0.0, sel)
        cw_ref[rows, :] = cw
        slot = jnp.where(sel > 0.0, _dot(earlier, sel.astype(BF16)) + count, -1.0)
        slot_ref[rows, :] = slot
        slott_ref[:, rows] = slot.T[:N_EXPERTS, :]
        count = count + jnp.sum(sel, axis=0, keepdims=True)
    cnt_ref[0] = jnp.broadcast_to(count, (N_EXPERTS, LANES))


def _router(h, mod, router, tiles_per_seq, route_ctx):
    r, d = h.shape
    rp = jnp.zeros((d, LANES), F32).at[:, :N_EXPERTS].set(router)
    sub = MOE_TILE // TILE
    row = lambda w: pl.BlockSpec((MOE_TILE, w), lambda i: (i, 0))
    return pl.pallas_call(
        functools.partial(_router_kernel, tiles_per_seq, route_ctx),
        out_shape=(jax.ShapeDtypeStruct((r, d), BF16), jax.ShapeDtypeStruct((r, LANES), F32),
                   jax.ShapeDtypeStruct((r, LANES), F32), jax.ShapeDtypeStruct((N_EXPERTS, r), F32),
                   jax.ShapeDtypeStruct((r // MOE_TILE, N_EXPERTS, LANES), F32)),
        grid=(r // MOE_TILE,),
        in_specs=[row(d),
                  pl.BlockSpec((sub, MOD_ROWS, d), lambda i: (i, 0, 0)),
                  pl.BlockSpec((d, LANES), lambda i: (0, 0))],
        out_specs=(row(d), row(LANES), row(LANES),
                   pl.BlockSpec((N_EXPERTS, MOE_TILE), lambda i: (0, i)),
                   pl.BlockSpec((1, N_EXPERTS, LANES), lambda i: (i, 0, 0))),
        compiler_params=_cparams(("arbitrary",)),
        name="moe_router",
    )(h, mod, rp)


def _moe_kernel(nch_ref, h_ref, u_ref, cw_ref, slot_ref, slott_ref, mod_ref, wg_ref, wu_ref, wd_ref, o_ref,
                xs_scr, y_scr):
    i = pl.program_id(0)
    e = pl.program_id(1)
    f = pl.program_id(2)
    last_f = pl.num_programs(2) - 1
    tile = u_ref.shape[0]
    nch = nch_ref[i * N_EXPERTS + e]
    block = 4 * MOE_CHUNK

    @pl.when((e == 0) & (f == 0))
    def _():
        o_ref[...] = jnp.zeros_like(o_ref)

    nblk = lax.shift_right_logical(nch, 2)
    tail2 = pl.multiple_of(nblk * block, MOE_CHUNK)
    tail1 = pl.multiple_of(tail2 + (nch & 2) * MOE_CHUNK, MOE_CHUNK)

    def run_block(r0, rn):
        rows = pl.ds(r0, rn)

        @pl.when(f == 0)
        def _():
            rid = (lax.broadcasted_iota(jnp.int32, (rn, 1), 0) + r0).astype(F32)
            pick = jnp.where(slott_ref[pl.ds(e, 1), :] == rid, 1.0, 0.0).astype(BF16)
            xs_scr[rows, :] = _dot(pick, u_ref[...]).astype(BF16)

        xs = xs_scr[rows, :]
        a = _silu(_dot(xs, wg_ref[0])) * _dot(xs, wu_ref[0])
        part = _dot(a.astype(BF16), wd_ref[0])

        @pl.when(f == 0)
        def _():
            y_scr[rows, :] = part

        @pl.when(f > 0)
        def _():
            y_scr[rows, :] += part

    def loop_blocks(fn):
        def body(b, carry):
            fn(pl.multiple_of(b * block, block), block)
            return carry
        lax.fori_loop(0, nblk, body, 0)
        pl.when((nch & 2) != 0)(lambda: fn(tail2, 2 * MOE_CHUNK))
        pl.when((nch & 1) != 0)(lambda: fn(tail1, MOE_CHUNK))

    loop_blocks(run_block)

    @pl.when((f == last_f) & (nch > 0))
    def _():
        mine = lax.broadcasted_iota(jnp.int32, (tile, LANES), 1) == e
        slot_col = jnp.sum(jnp.where(mine, slot_ref[...], 0.0), axis=-1, keepdims=True)
        cw_col = jnp.sum(jnp.where(mine, cw_ref[...], 0.0), axis=-1, keepdims=True)

        def scatter_block(r0, rn):
            cid = (lax.broadcasted_iota(jnp.int32, (1, rn), 1) + r0).astype(F32)
            put = jnp.where(slot_col == cid, 1.0, 0.0).astype(BF16)
            o_ref[...] += cw_col * _dot(put, y_scr[pl.ds(r0, rn), :].astype(BF16))

        loop_blocks(scatter_block)

    @pl.when((e == pl.num_programs(1) - 1) & (f == last_f))
    def _():
        for j in range(tile // TILE):
            rows = slice(j * TILE, (j + 1) * TILE)
            o_ref[rows, :] = h_ref[rows, :] + mod_ref[j, G_F:G_F + 1] * o_ref[rows, :]


def _moe(h, u, cw, slot, slott, counts, mod, w_gate, w_up, w_down, layer):
    r, d = h.shape
    _, ne, _, ff = w_gate.shape
    tm, tf = MOE_TILE, 512
    sub = tm // TILE
    nch = ((counts[:, 0, :ne].astype(jnp.int32) + (MOE_CHUNK - 1)) // MOE_CHUNK).reshape(-1)
    once = pl.Buffered(1)
    row = lambda w: pl.BlockSpec((tm, w), lambda i, e, f, n: (i, 0), pipeline_mode=once)
    return pl.pallas_call(
        _moe_kernel,
        out_shape=jax.ShapeDtypeStruct((r, d), F32),
        grid_spec=pltpu.PrefetchScalarGridSpec(
            num_scalar_prefetch=1,
            grid=(r // tm, ne, ff // tf),
            in_specs=[row(d), row(d), row(LANES), row(LANES),
                      pl.BlockSpec((ne, tm), lambda i, e, f, n: (0, i), pipeline_mode=once),
                      pl.BlockSpec((sub, MOD_ROWS, d), lambda i, e, f, n: (i, 0, 0)),
                      pl.BlockSpec((None, 1, d, tf), lambda i, e, f, n: (layer, e, 0, f)),
                      pl.BlockSpec((None, 1, d, tf), lambda i, e, f, n: (layer, e, 0, f)),
                      pl.BlockSpec((None, 1, tf, d), lambda i, e, f, n: (layer, e, f, 0))],
            out_specs=pl.BlockSpec((tm, d), lambda i, e, f, n: (i, 0), pipeline_mode=once),
            scratch_shapes=[pltpu.VMEM((tm, d), BF16), pltpu.VMEM((tm, d), F32)]),
        compiler_params=_cparams(("arbitrary", "arbitrary", "arbitrary")),
        name="moe_experts",
    )(nch, h, u, cw, slot, slott, mod, w_gate, w_up, w_down)


def _final_kernel(h_ref, g_ref, o_ref):
    h = h_ref[0]
    inv = lax.rsqrt(jnp.mean(h * h, axis=-1, keepdims=True) + EPS)
    o_ref[0] = h * inv * g_ref[...]


def _final_norm(h3, gain, seq):
    bsz, _, d = h3.shape
    return pl.pallas_call(
        _final_kernel,
        out_shape=jax.ShapeDtypeStruct((bsz, seq, d), F32),
        grid=(bsz, seq // TILE),
        in_specs=[pl.BlockSpec((1, TILE, d), lambda b, t: (b, t, 0)),
                  pl.BlockSpec((1, d), lambda b, t: (0, 0))],
        out_specs=pl.BlockSpec((1, TILE, d), lambda b, t: (b, t, 0)),
        compiler_params=_cparams(("arbitrary", "arbitrary")),
        name="final_norm",
    )(h3, gain.reshape(1, d))


def _tile_mods(mods, gain_mix, gain_ffn, bsz, lat_tiles):
    sh_m, sc_m, g_m, sh_f, sc_f, g_f = (mods[:, i] for i in range(N_MOD))
    rows = jnp.stack([gain_mix * (1.0 + sc_m), sh_m, g_m, gain_ffn * (1.0 + sc_f), sh_f, g_f,
                      jnp.zeros_like(g_f), jnp.zeros_like(g_f)], axis=1)
    lat = jnp.broadcast_to(rows[:bsz, None], (bsz, lat_tiles) + rows.shape[1:])
    ctx = jnp.broadcast_to(rows[bsz:, None], (bsz, 1) + rows.shape[1:])
    return jnp.concatenate([lat, ctx], axis=1).reshape(bsz * (lat_tiles + 1), MOD_ROWS, rows.shape[-1])


def kernel(x, c, ctx, c_ctx, ada_w, ada_b, norm_mix, norm_ffn, norm_final, s5_lambda_re, s5_lambda_im, s5_log_step, s5_b_re, s5_b_im, s5_c_re, s5_c_im, s5_d, s5_glu_w, s5_glu_b, pool_w, pool_b, pool_scale, fnet_w, fnet_b, ffn_w_gate, ffn_w_up, ffn_w_down, moe_router, moe_w_gate, moe_w_up, moe_w_down):
    bsz, seq, d = x.shape
    ctx_len = ctx.shape[1]
    depth = ada_w.shape[0]
    assert ctx_len == TILE and seq % ROW_BLOCK == 0 and d % LANES == 0
    tot = seq + ctx_len
    lat_tiles = seq // TILE
    tps = lat_tiles + 1
    r = bsz * tot
    assert r % ROW_BLOCK == 0

    cond = jnp.zeros((16, d), F32).at[:bsz].set(c).at[bsz].set(c_ctx)
    mods_all = _ada_all(cond, ada_w, ada_b)[:, :bsz + 1].reshape(depth, bsz + 1, N_MOD, d)

    moe_wg, moe_wu, moe_wd = (w.astype(BF16) for w in (moe_w_gate, moe_w_up, moe_w_down))
    h = None
    for i in range(depth):
        kind, j = i % 3, i // 3
        mod = _tile_mods(mods_all[i], norm_mix[i], norm_ffn[i], bsz, lat_tiles)
        if kind == 0:
            src = (x, ctx) if h is None else (h,)
            u = _s5_pre(src, mod, tps)
            kc, ec, cl, a = _s5_params(s5_lambda_re[j], s5_lambda_im[j], s5_log_step[j],
                                       s5_b_re[j], s5_b_im[j], s5_c_re[j], s5_c_im[j])
            y_lat, y_ctx = _s5_scan(u.reshape(bsz, tot, d), kc, ec, cl, a, seq, ctx_len)
            h = _s5_glu(src, y_lat, y_ctx, mod, s5_d[j], s5_glu_w[j], s5_glu_b[j], seq)
        elif kind == 1:
            h = _pool(h, mod, pool_w[j], pool_b[j], pool_scale[j], seq)
        else:
            g = _fnet_fold(fnet_w[j])
            v3 = _fnet_proj(h, mod, g).reshape(bsz, tot, 2 * d)
            h3 = h.reshape(bsz, tot, d)
            mod4 = mod.reshape(bsz, tps, MOD_ROWS, d)
            part = _fnet_lat(_dft_matrix(seq), v3, h3, mod4, fnet_b[j], seq)
            cc, sc = _dft_tables(ctx_len)
            dftc = jnp.asarray(np.concatenate([cc, sc], axis=1), BF16)
            h = _fnet_ctx(dftc, v3, h3, mod4, fnet_b[j], part, seq).reshape(r, d)
        kk = i // 2
        if i % 2 == 0:
            h = _ffn(h, mod, ffn_w_gate[kk], ffn_w_up[kk], ffn_w_down[kk])
        else:
            ub, cw, slot, slott, counts = _router(h, mod, moe_router[kk], tps, route_ctx=i < depth - 1)
            h = _moe(h, ub, cw, slot, slott, counts, mod, moe_wg, moe_wu, moe_wd, kk)
    return _final_norm(h.reshape(bsz, tot, d), norm_final, seq)
```

```python
import functools
import math

import numpy as np
import jax
import jax.numpy as jnp
from jax import lax
from jax.experimental import pallas as pl
from jax.experimental.pallas import tpu as pltpu

F32 = jnp.float32
BF16 = jnp.bfloat16

EPS = 1e-6
GRID_W = 64
N_MOD = 6
S5_GROUP = 16
S5_STATE = 64
S5_CHUNK = 16
POOL_WINDOWS = (2, 4, 8, 16)
FNET_GROUPS = 4
N_EXPERTS = 8

LANES = 128
SUBLANES = 8
TILE = 256
ROW_BLOCK = 1024
VMEM_LIMIT = 56 * 1024 * 1024


def _cparams(sem, vmem=VMEM_LIMIT):
    return pltpu.CompilerParams(dimension_semantics=sem, vmem_limit_bytes=vmem)


def _dot(a, b):
    return jnp.dot(a, b, preferred_element_type=F32)


def _split(a):
    hi = a.astype(BF16)
    lo = (a - hi.astype(F32)).astype(BF16)
    return hi, lo


def _dot3(a, b):
    a_hi, a_lo = _split(a)
    b_hi, b_lo = _split(b)
    return _dot(a_hi, b_hi) + (_dot(a_lo, b_hi) + _dot(a_hi, b_lo))


def _normmod(h, gs, sh):
    inv = lax.rsqrt(jnp.mean(h * h, axis=-1, keepdims=True) + EPS)
    return h * inv * gs + sh


GS_M, SH_M, G_M, GS_F, SH_F, G_F = range(6)
MOD_ROWS = 8


def _ada_kernel(c_ref, w_ref, b_ref, o_ref):
    c = c_ref[...]
    s = c * (1.0 / (1.0 + jnp.exp(-c)))
    o_ref[0] = _dot3(s, w_ref[0]) + b_ref[0]


def _ada_all(cond, ada_w, ada_b):
    depth, d, n = ada_w.shape
    rows = cond.shape[0]
    tn = 1536
    return pl.pallas_call(
        _ada_kernel,
        out_shape=jax.ShapeDtypeStruct((depth, rows, n), F32),
        grid=(depth, n // tn),
        in_specs=[
            pl.BlockSpec((rows, d), lambda i, j: (0, 0)),
            pl.BlockSpec((1, d, tn), lambda i, j: (i, 0, j)),
            pl.BlockSpec((1, 1, tn), lambda i, j: (i, 0, j)),
        ],
        out_specs=pl.BlockSpec((1, rows, tn), lambda i, j: (i, 0, j)),
        compiler_params=_cparams(("arbitrary", "arbitrary")),
        name="ada_params",
    )(cond, ada_w, ada_b.reshape(depth, 1, n))


def _stream_specs(src, tps, d):
    if len(src) == 1:
        return [pl.BlockSpec((TILE, d), lambda i: (i, 0))]
    return [pl.BlockSpec((None, TILE, d), lambda i: (i // tps, jnp.minimum(i % tps, tps - 2), 0)),
            pl.BlockSpec((None, TILE, d), lambda i: (i // tps, 0, 0))]


def _stream_tile(refs, tps):
    if len(refs) == 1:
        return refs[0][...]
    is_ctx = (pl.program_id(0) % tps) == tps - 1
    return jnp.where(is_ctx, refs[1][...], refs[0][...])


def _s5_pre_kernel(n_src, tps, *refs):
    mod_ref, u_ref = refs[n_src:]
    mod = mod_ref[0]
    u_ref[...] = _normmod(_stream_tile(refs[:n_src], tps), mod[GS_M:GS_M + 1], mod[SH_M:SH_M + 1])


def _s5_pre(src, mod, tps):
    r, _, d = mod.shape
    r *= TILE
    return pl.pallas_call(
        functools.partial(_s5_pre_kernel, len(src), tps),
        out_shape=jax.ShapeDtypeStruct((r, d), F32),
        grid=(r // TILE,),
        in_specs=_stream_specs(src, tps, d) + [pl.BlockSpec((1, MOD_ROWS, d), lambda i: (i, 0, 0))],
        out_specs=pl.BlockSpec((TILE, d), lambda i: (i, 0)),
        compiler_params=_cparams(("arbitrary",)),
        name="s5_pre",
    )(*src, mod)


def _s5_params(lam_re, lam_im, log_step, b_re, b_im, c_re, c_im):
    hp = lax.Precision.HIGHEST
    L = S5_CHUNK
    n_dir, g, p = lam_re.shape
    gb = LANES // S5_GROUP
    nb = g // gb
    hch = S5_GROUP
    step = jnp.exp(log_step)[..., None]
    ar, ai = lam_re * step, lam_im * step
    k = jnp.arange(L + 1, dtype=F32)[None, :, None, None]
    mag = jnp.exp(ar[:, None] * k)
    ang = ai[:, None] * k
    pr, pi = mag * jnp.cos(ang), mag * jnp.sin(ang)
    lbr, lbi = pr[:, 1] - 1.0, pi[:, 1]
    den = lam_re * lam_re + lam_im * lam_im
    qr = (lbr * lam_re + lbi * lam_im) / den
    qi = (lbi * lam_re - lbr * lam_im) / den
    bbr = qr[..., None] * b_re - qi[..., None] * b_im
    bbi = qr[..., None] * b_im + qi[..., None] * b_re
    er = pr[..., None] * bbr[:, None] - pi[..., None] * bbi[:, None]
    ei = pr[..., None] * bbi[:, None] + pi[..., None] * bbr[:, None]
    kmat = (jnp.einsum('dgop,dkgpi->dgiko', c_re, er[:, :L], precision=hp)
            - jnp.einsum('dgop,dkgpi->dgiko', c_im, ei[:, :L], precision=hp))
    kc = kmat.reshape(n_dir, nb, gb * hch, L * hch)
    es = jnp.stack([er[:, L - 1::-1], ei[:, L - 1::-1]], axis=0)
    ec = jnp.transpose(es, (1, 3, 2, 5, 0, 4)).reshape(n_dir, nb, gb, L, hch, 2, p)
    ec = jnp.transpose(ec, (0, 1, 3, 2, 4, 5, 6)).reshape(n_dir, nb, L, gb * hch, 2 * p)
    clr = c_re[:, None] * pr[:, 1:, :, None, :] - c_im[:, None] * pi[:, 1:, :, None, :]
    cli = c_re[:, None] * pi[:, 1:, :, None, :] + c_im[:, None] * pr[:, 1:, :, None, :]
    cs = jnp.stack([clr, -cli], axis=0)
    cl = jnp.transpose(cs, (1, 3, 0, 5, 2, 4)).reshape(n_dir, nb, gb, 2, p, L, hch)
    cl = jnp.transpose(cl, (0, 1, 3, 2, 4, 5, 6)).reshape(n_dir, nb, 2 * gb * p, L * hch)
    a = jnp.stack([pr[:, L].reshape(n_dir, nb, 1, gb * p), pi[:, L].reshape(n_dir, nb, 1, gb * p)], axis=3)
    return kc.astype(BF16), ec.astype(BF16), cl.astype(BF16), a.reshape(n_dir, nb, 1, 2 * gb * p)


def _replicators():
    gb = LANES // S5_GROUP
    r16 = np.zeros((S5_CHUNK, S5_GROUP, S5_CHUNK, gb, S5_GROUP), np.float32)
    for k in range(S5_CHUNK):
        for o in range(S5_GROUP):
            r16[k, o, k, :, o] = 1.0
    r64 = np.zeros((2, S5_STATE, 2, gb, S5_STATE), np.float32)
    for part in range(2):
        for q in range(S5_STATE):
            r64[part, q, part, :, q] = 1.0
    return (jnp.asarray(r16.reshape(S5_CHUNK * S5_GROUP, S5_CHUNK * LANES), BF16),
            jnp.asarray(r64.reshape(2 * S5_STATE, 2 * gb * S5_STATE), BF16))


def _s5_scan_kernel(ul_ref, uc_ref, kc_ref, ec_ref, cl_ref, r16_ref, r64_ref, a_ref, yl_ref, yc_ref,
                    h_scr, s_scr, hs_scr, wts_scr, wy_scr):
    d = pl.program_id(1)
    t = pl.program_id(2)
    L = S5_CHUNK
    nstate = a_ref.shape[-1]
    half = nstate // 2
    nt = L * LANES
    blk = lambda i: slice(i * LANES, (i + 1) * LANES)

    def build_weights():
        row_gi = lax.broadcasted_iota(jnp.int32, (LANES, LANES), 0) // S5_GROUP
        lane_go = lax.broadcasted_iota(jnp.int32, (LANES, LANES), 1) // S5_GROUP
        toep = _dot(kc_ref[0, 0], r16_ref[...])
        zero = jnp.zeros((LANES, LANES), BF16)
        for lag in range(L):
            piece = jnp.where(row_gi == lane_go, toep[:, blk(lag)], 0.0).astype(BF16)
            for s in range(L - lag):
                wts_scr[blk(s), blk(s + lag)] = piece
        for tt in range(0, L, 2):
            wts_scr[blk(tt + 1), blk(tt)] = zero
        row_gi_w = lax.broadcasted_iota(jnp.int32, (LANES, nstate), 0) // S5_GROUP
        lane_gp = (lax.broadcasted_iota(jnp.int32, (LANES, nstate), 1) % half) // S5_STATE
        for s in range(L):
            e = _dot(ec_ref[0, 0, s], r64_ref[...])
            wts_scr[blk(s), nt:] = jnp.where(row_gi_w == lane_gp, e, 0.0).astype(BF16)
        row_gp = (lax.broadcasted_iota(jnp.int32, (nstate, LANES), 0) % half) // S5_STATE
        lane_go_t = lax.broadcasted_iota(jnp.int32, (nstate, LANES), 1) // S5_GROUP
        for tt in range(L):
            w = _dot(cl_ref[0, 0], r16_ref[:, blk(tt)])
            wy_scr[:, blk(tt)] = jnp.where(row_gp == lane_go_t, w, 0.0).astype(BF16)

    def process(x_ref, y_ref):
        bsz, tok, _ = x_ref.shape
        nc = tok // L
        m = bsz * nc
        flip = lambda i, n: i + d * (n - 1 - 2 * i)
        u = jnp.concatenate(
            [x_ref[:, pl.ds(flip(s, L), nc, stride=L), :].reshape(m, LANES) for s in range(L)], axis=1).astype(BF16)
        s_all = _dot(u, wts_scr[:, nt:])
        nq = nstate // LANES
        nh = nq // 2
        ncp = nc + SUBLANES
        for q in range(nq):
            for b in range(bsz):
                s_scr[q, b * ncp:b * ncp + nc, :] = s_all[b * nc:(b + 1) * nc, blk(q)]
        a = a_ref[0, 0]
        a_re = [jnp.broadcast_to(a[:, blk(q)], (bsz, LANES)) for q in range(nh)]
        a_im = [jnp.broadcast_to(a[:, blk(nh + q)], (bsz, LANES)) for q in range(nh)]
        h_re = [h_scr[q] for q in range(nh)]
        h_im = [h_scr[nh + q] for q in range(nh)]
        for c in range(nc):
            rows = pl.ds(flip(c, nc), bsz, stride=ncp)
            for q in range(nh):
                hs_scr.at[q][rows, :] = h_re[q]
                hs_scr.at[nh + q][rows, :] = h_im[q]
                s_re = s_scr.at[q][rows, :]
                s_im = s_scr.at[nh + q][rows, :]
                h_re[q], h_im[q] = (a_re[q] * h_re[q] - a_im[q] * h_im[q] + s_re,
                                    a_re[q] * h_im[q] + a_im[q] * h_re[q] + s_im)
        for q in range(nh):
            h_scr[q] = h_re[q]
            h_scr[nh + q] = h_im[q]
        hs = jnp.concatenate(
            [jnp.concatenate([hs_scr[q, b * ncp:b * ncp + nc, :] for b in range(bsz)], axis=0) for q in range(nq)],
            axis=1).astype(BF16)
        for n in range(L // 2):
            cols = slice(2 * n * LANES, 2 * (n + 1) * LANES)
            y = _dot(u[:, :cols.stop], wts_scr[:cols.stop, cols]) + _dot(hs, wy_scr[:, cols])
            for tt in (2 * n, 2 * n + 1):
                y_ref[:, pl.ds(flip(tt, L), nc, stride=L), :] = (
                    y[:, blk(tt - 2 * n)].reshape(bsz, nc, LANES))

    @pl.when(t == 0)
    def _():
        build_weights()
        h_scr[...] = jnp.zeros_like(h_scr)
        process(uc_ref, yc_ref)

    @pl.when(t > 0)
    def _():
        process(ul_ref, yl_ref)


def _s5_scan(u3, kc, ec, cl, a, seq, ctx_len):
    bsz, _, d = u3.shape
    tl = 1024
    ntl = seq // tl
    nb = d // LANES
    n_dir = 2
    ctx_blk = seq // ctx_len
    nstate = a.shape[-1]
    nt = S5_CHUNK * LANES
    r16, r64 = _replicators()

    def lat_idx(dd, t):
        i = jnp.maximum(t - 1, 0)
        return jnp.where(dd == 0, i, ntl - 1 - i)

    per_block = lambda arr: pl.BlockSpec((1, 1) + arr.shape[2:], lambda j, dd, t: (dd, j) + (0,) * (arr.ndim - 2))
    const = lambda arr: pl.BlockSpec(arr.shape, lambda j, dd, t: (0,) * arr.ndim)
    return pl.pallas_call(
        _s5_scan_kernel,
        out_shape=(jax.ShapeDtypeStruct((n_dir, bsz, seq, d), F32),
                   jax.ShapeDtypeStruct((n_dir, bsz, ctx_len, d), F32)),
        grid=(nb, n_dir, ntl + 1),
        in_specs=[
            pl.BlockSpec((bsz, tl, LANES), lambda j, dd, t: (0, lat_idx(dd, t), j)),
            pl.BlockSpec((bsz, ctx_len, LANES), lambda j, dd, t: (0, ctx_blk, j)),
            per_block(kc), per_block(ec), per_block(cl), const(r16), const(r64), per_block(a),
        ],
        out_specs=(
            pl.BlockSpec((None, bsz, tl, LANES), lambda j, dd, t: (dd, 0, lat_idx(dd, t), j)),
            pl.BlockSpec((None, bsz, ctx_len, LANES), lambda j, dd, t: (dd, 0, 0, j)),
        ),
        scratch_shapes=[
            pltpu.VMEM((nstate // LANES, bsz, LANES), F32),
            pltpu.VMEM((nstate // LANES, bsz * (tl // S5_CHUNK + SUBLANES), LANES), F32),
            pltpu.VMEM((nstate // LANES, bsz * (tl // S5_CHUNK + SUBLANES), LANES), F32),
            pltpu.VMEM((nt, nt + nstate), BF16),
            pltpu.VMEM((nstate, nt), BF16),
        ],
        compiler_params=_cparams(("arbitrary", "arbitrary", "arbitrary")),
        name="s5_scan",
    )(u3, u3, kc, ec, cl, r16, r64, a)


def _gelu_tanh(x):
    return 0.5 * x * (1.0 + jnp.tanh(math.sqrt(2.0 / math.pi) * (x + 0.044715 * (x * x * x))))


def _s5_glu_kernel(n_src, tiles_per_seq, *refs):
    ylf_ref, ylr_ref, ycf_ref, ycr_ref, mod_ref, dvec_ref, w_ref, b_ref, o_ref = refs[n_src:]
    is_ctx = (pl.program_id(0) % tiles_per_seq) == tiles_per_seq - 1
    y_scan = jnp.where(is_ctx, ycf_ref[0, 0] + ycr_ref[0, 0], ylf_ref[0, 0] + ylr_ref[0, 0])
    h = _stream_tile(refs[:n_src], tiles_per_seq)
    mod = mod_ref[0]
    y = dvec_ref[...] * _normmod(h, mod[GS_M:GS_M + 1], mod[SH_M:SH_M + 1]) + y_scan
    z = _gelu_tanh(y)
    gate = 1.0 / (1.0 + jnp.exp(-(_dot(z.astype(BF16), w_ref[...]) + b_ref[...])))
    o_ref[...] = h + mod[G_M:G_M + 1] * (z * gate)


def _s5_glu(src, y_lat, y_ctx, mod, dvec, glu_w, glu_b, seq):
    d = mod.shape[-1]
    r = mod.shape[0] * TILE
    lat_tiles = seq // TILE
    tps = lat_tiles + 1

    def lat_map(dd):
        return lambda i: (dd, i // tps, jnp.minimum(i % tps, lat_tiles - 1), 0)

    def ctx_map(dd):
        return lambda i: (dd, i // tps, 0, 0)

    row = pl.BlockSpec((TILE, d), lambda i: (i, 0))
    vec = pl.BlockSpec((1, d), lambda i: (0, 0))
    return pl.pallas_call(
        functools.partial(_s5_glu_kernel, len(src), tps),
        out_shape=jax.ShapeDtypeStruct((r, d), F32),
        grid=(r // TILE,),
        in_specs=_stream_specs(src, tps, d) + [
                  pl.BlockSpec((1, 1, TILE, d), lat_map(0)), pl.BlockSpec((1, 1, TILE, d), lat_map(1)),
                  pl.BlockSpec((1, 1, TILE, d), ctx_map(0)), pl.BlockSpec((1, 1, TILE, d), ctx_map(1)),
                  pl.BlockSpec((1, MOD_ROWS, d), lambda i: (i, 0, 0)),
                  vec, pl.BlockSpec((d, d), lambda i: (0, 0)), vec],
        out_specs=row,
        compiler_params=_cparams(("arbitrary",)),
        name="s5_glu",
    )(*src, y_lat, y_lat, y_ctx, y_ctx, mod, dvec.reshape(1, d), glu_w.astype(BF16), glu_b.reshape(1, d))


def _pool_matrices(width, tile):
    mats = np.zeros((len(POOL_WINDOWS), tile, tile), np.float32)
    pos = np.arange(width)
    for g, win in enumerate(POOL_WINDOWS):
        lo = np.clip(pos - win // 2, 0, width)
        hi = np.clip(pos + win // 2, 0, width)
        for base in range(0, tile, width):
            for j in range(width):
                mats[g, base + j, base + lo[j]:base + hi[j]] = 1.0 / float(hi[j] - lo[j])
    return mats


def _pool_kernel(h_ref, mod_ref, a_ref, w_ref, b_ref, sc_ref, o_ref):
    mod = mod_ref[0]
    h = h_ref[...]
    u = _normmod(h, mod[GS_M:GS_M + 1], mod[SH_M:SH_M + 1])
    ub = u.astype(BF16)
    gw = w_ref.shape[-1]
    for g in range(w_ref.shape[0]):
        cols = slice(g * gw, (g + 1) * gw)
        res = _dot(a_ref[0, g], ub[:, cols]) - u[:, cols]
        mixed = _dot(res.astype(BF16), w_ref[g]) + b_ref[:, cols]
        o_ref[:, cols] = h[:, cols] + mod[G_M:G_M + 1, cols] * (mixed * sc_ref[:, cols])


def _pool(h, mod, pool_w, pool_b, pool_scale, seq):
    r, d = h.shape
    tps = seq // TILE + 1
    ng, gw, _ = pool_w.shape
    mats = jnp.asarray(np.stack([_pool_matrices(GRID_W, TILE), _pool_matrices(TILE, TILE)]), BF16)
    row = pl.BlockSpec((TILE, d), lambda i: (i, 0))
    vec = pl.BlockSpec((1, d), lambda i: (0, 0))
    return pl.pallas_call(
        _pool_kernel,
        out_shape=jax.ShapeDtypeStruct((r, d), F32),
        grid=(r // TILE,),
        in_specs=[row, pl.BlockSpec((1, MOD_ROWS, d), lambda i: (i, 0, 0)),
                  pl.BlockSpec((1, ng, TILE, TILE), lambda i: ((i % tps) // (tps - 1), 0, 0, 0)),
                  pl.BlockSpec((ng, gw, gw), lambda i: (0, 0, 0)), vec, vec],
        out_specs=row,
        compiler_params=_cparams(("arbitrary",)),
        name="pool_mixer",
    )(h, mod, mats, pool_w.astype(BF16), pool_b.reshape(1, d), pool_scale.reshape(1, d))


def _dft_tables(n):
    idx = np.arange(n)
    ang = 2.0 * np.pi * ((idx[:, None] * idx[None, :]) % n) / n
    scale = 1.0 / math.sqrt(n)
    return (np.cos(ang) * scale).astype(np.float32), (np.sin(ang) * scale).astype(np.float32)


def _fnet_fold_kernel(cc_ref, sc_ref, w_ref, o_ref):
    w = w_ref[...]
    d = w.shape[-1]
    o_ref[:, :d] = _dot3(cc_ref[...], w).astype(o_ref.dtype)
    o_ref[:, d:] = (-_dot3(sc_ref[...], w)).astype(o_ref.dtype)


def _fnet_fold(fnet_w):
    d = fnet_w.shape[0]
    gw = d // FNET_GROUPS
    cc, sc = _dft_tables(gw)
    sq = pl.BlockSpec((gw, gw), lambda i: (0, 0))
    return pl.pallas_call(
        _fnet_fold_kernel,
        out_shape=jax.ShapeDtypeStruct((d, 2 * d), BF16),
        grid=(FNET_GROUPS,),
        in_specs=[sq, sq, pl.BlockSpec((gw, d), lambda i: (i, 0))],
        out_specs=pl.BlockSpec((gw, 2 * d), lambda i: (i, 0)),
        compiler_params=_cparams(("arbitrary",)),
        name="fnet_fold",
    )(jnp.asarray(cc), jnp.asarray(sc), fnet_w)


def _fnet_proj_kernel(h_ref, mod_ref, g_ref, v_ref):
    for j in range(h_ref.shape[0] // TILE):
        rows = slice(j * TILE, (j + 1) * TILE)
        mod = mod_ref[j]
        u = _normmod(h_ref[rows, :], mod[GS_M:GS_M + 1], mod[SH_M:SH_M + 1])
        v_ref[rows, :] = _dot(u.astype(BF16), g_ref[...]).astype(v_ref.dtype)


def _fnet_proj(h, mod, g):
    r, d = h.shape
    sub = ROW_BLOCK // TILE
    return pl.pallas_call(
        _fnet_proj_kernel,
        out_shape=jax.ShapeDtypeStruct((r, 2 * d), BF16),
        grid=(r // ROW_BLOCK,),
        in_specs=[pl.BlockSpec((ROW_BLOCK, d), lambda i: (i, 0)),
                  pl.BlockSpec((sub, MOD_ROWS, d), lambda i: (i, 0, 0)),
                  pl.BlockSpec((d, 2 * d), lambda i: (0, 0))],
        out_specs=pl.BlockSpec((ROW_BLOCK, 2 * d), lambda i: (i, 0)),
        compiler_params=_cparams(("arbitrary",)),
        name="fnet_proj",
    )(h, mod, g)


def _dft_matrix(n):
    n1 = int(round(math.sqrt(n)))
    assert n1 * n1 == n
    hi = jnp.arange(n1, dtype=jnp.int32)[:, None]
    k = jnp.arange(n, dtype=jnp.int32)[None, :]
    ang_a = (2.0 * math.pi / n1) * ((hi * k) % n1).astype(F32)
    ang_b = (2.0 * math.pi / n) * ((hi * k) % n).astype(F32)
    ca, sa, cb, sb = jnp.cos(ang_a), jnp.sin(ang_a), jnp.cos(ang_b), jnp.sin(ang_b)
    scale = 1.0 / math.sqrt(n)
    c = (ca[:, None] * cb[None] - sa[:, None] * sb[None]).reshape(n, n) * scale
    s = (sa[:, None] * cb[None] + ca[:, None] * sb[None]).reshape(n, n) * scale
    return jnp.concatenate([c, s], axis=1).astype(BF16)


def _fnet_lat_kernel(f_ref, v_ref, h_ref, mod_ref, b_ref, o_ref, acc_ref):
    k = pl.program_id(2)

    @pl.when(k == 0)
    def _():
        acc_ref[...] = jnp.zeros_like(acc_ref)

    acc_ref[...] += _dot(f_ref[...], v_ref[0])

    @pl.when(k == pl.num_programs(2) - 1)
    def _():
        for j in range(h_ref.shape[1] // TILE):
            rows = slice(j * TILE, (j + 1) * TILE)
            o_ref[0, rows, :] = h_ref[0, rows, :] + mod_ref[0, j, G_M:G_M + 1] * (acc_ref[rows, :] + b_ref[...])


def _fnet_lat(dft, v3, h3, mod4, fnet_b, seq):
    bsz, tot, d = h3.shape
    tm, tk = ROW_BLOCK, min(2048, seq)
    kh = seq // tk
    sub = tm // TILE
    return pl.pallas_call(
        _fnet_lat_kernel,
        out_shape=jax.ShapeDtypeStruct((bsz, tot, d), F32),
        grid=(bsz, seq // tm, 2 * kh),
        in_specs=[pl.BlockSpec((tm, tk), lambda b, m, k: (m, k)),
                  pl.BlockSpec((1, tk, d), lambda b, m, k: (b, k % kh, k // kh)),
                  pl.BlockSpec((1, tm, d), lambda b, m, k: (b, m, 0)),
                  pl.BlockSpec((1, sub, MOD_ROWS, d), lambda b, m, k: (b, m, 0, 0)),
                  pl.BlockSpec((1, d), lambda b, m, k: (0, 0))],
        out_specs=pl.BlockSpec((1, tm, d), lambda b, m, k: (b, m, 0)),
        scratch_shapes=[pltpu.VMEM((tm, d), F32)],
        compiler_params=_cparams(("arbitrary", "arbitrary", "arbitrary")),
        name="fnet_dft_latent",
    )(dft, v3, h3, mod4, fnet_b.reshape(1, d))


def _fnet_ctx_kernel(f_ref, v_ref, h_ref, mod_ref, b_ref, prev_ref, o_ref):
    del prev_ref
    n = f_ref.shape[0]
    d = h_ref.shape[-1]
    acc = _dot(f_ref[:, :n], v_ref[0, :, :d]) + _dot(f_ref[:, n:], v_ref[0, :, d:])
    o_ref[0] = h_ref[0] + mod_ref[0, 0, G_M:G_M + 1] * (acc + b_ref[...])


def _fnet_ctx(dftc, v3, h3, mod4, fnet_b, partial, seq):
    bsz, tot, d = h3.shape
    n = tot - seq
    blk = seq // n
    return pl.pallas_call(
        _fnet_ctx_kernel,
        out_shape=jax.ShapeDtypeStruct((bsz, tot, d), F32),
        grid=(bsz,),
        in_specs=[pl.BlockSpec((n, 2 * n), lambda b: (0, 0)),
                  pl.BlockSpec((1, n, 2 * d), lambda b: (b, blk, 0)),
                  pl.BlockSpec((1, n, d), lambda b: (b, blk, 0)),
                  pl.BlockSpec((1, 1, MOD_ROWS, d), lambda b: (b, blk, 0, 0)),
                  pl.BlockSpec((1, d), lambda b: (0, 0)),
                  pl.BlockSpec(memory_space=pl.ANY)],
        out_specs=pl.BlockSpec((1, n, d), lambda b: (b, blk, 0)),
        input_output_aliases={5: 0},
        compiler_params=_cparams(("arbitrary",)),
        name="fnet_dft_context",
    )(dftc, v3, h3, mod4, fnet_b.reshape(1, d), partial)


def _silu(x):
    return x * (1.0 / (1.0 + jnp.exp(-x)))


def _ffn_kernel(h_ref, mod_ref, wg_ref, wu_ref, wd_ref, o_ref, u_scr, acc_ref):
    f = pl.program_id(1)
    sub = h_ref.shape[0] // TILE

    @pl.when(f == 0)
    def _():
        for j in range(sub):
            rows = slice(j * TILE, (j + 1) * TILE)
            mod = mod_ref[j]
            u_scr[rows, :] = _normmod(h_ref[rows, :], mod[GS_F:GS_F + 1], mod[SH_F:SH_F + 1]).astype(BF16)
        acc_ref[...] = jnp.zeros_like(acc_ref)

    u = u_scr[...]
    a = _silu(_dot(u, wg_ref[...])) * _dot(u, wu_ref[...])
    acc_ref[...] += _dot(a.astype(BF16), wd_ref[...])

    @pl.when(f == pl.num_programs(1) - 1)
    def _():
        for j in range(sub):
            rows = slice(j * TILE, (j + 1) * TILE)
            o_ref[rows, :] = h_ref[rows, :] + mod_ref[j, G_F:G_F + 1] * acc_ref[rows, :]


def _ffn(h, mod, w_gate, w_up, w_down):
    r, d = h.shape
    ff = w_gate.shape[1]
    tm, tf = ROW_BLOCK, 256
    sub = tm // TILE
    return pl.pallas_call(
        _ffn_kernel,
        out_shape=jax.ShapeDtypeStruct((r, d), F32),
        grid=(r // tm, ff // tf),
        in_specs=[pl.BlockSpec((tm, d), lambda i, f: (i, 0)),
                  pl.BlockSpec((sub, MOD_ROWS, d), lambda i, f: (i, 0, 0)),
                  pl.BlockSpec((d, tf), lambda i, f: (0, f)),
                  pl.BlockSpec((d, tf), lambda i, f: (0, f)),
                  pl.BlockSpec((tf, d), lambda i, f: (f, 0))],
        out_specs=pl.BlockSpec((tm, d), lambda i, f: (i, 0)),
        scratch_shapes=[pltpu.VMEM((tm, d), BF16), pltpu.VMEM((tm, d), F32)],
        compiler_params=_cparams(("arbitrary", "arbitrary")),
        name="ffn_swiglu",
    )(h, mod, w_gate.astype(BF16), w_up.astype(BF16), w_down.astype(BF16))


MOE_TILE = 2048
MOE_CHUNK = 128


def _router_kernel(tiles_per_seq, route_ctx, h_ref, mod_ref, r_ref, u_ref, cw_ref, slot_ref, slott_ref, cnt_ref):
    nsub = h_ref.shape[0] // TILE
    lane = lax.broadcasted_iota(jnp.int32, (TILE, LANES), 1).astype(F32)
    earlier = jnp.where(lax.broadcasted_iota(jnp.int32, (TILE, TILE), 1)
                        < lax.broadcasted_iota(jnp.int32, (TILE, TILE), 0), 1.0, 0.0).astype(BF16)
    neg = jnp.float32(-jnp.inf)
    count = jnp.zeros((1, LANES), F32)
    for j in range(nsub):
        rows = slice(j * TILE, (j + 1) * TILE)
        mod = mod_ref[j]
        u = _normmod(h_ref[rows, :], mod[GS_F:GS_F + 1], mod[SH_F:SH_F + 1])
        u_ref[rows, :] = u.astype(BF16)
        logits = _dot3(u, r_ref[...])
        logits = jnp.where(lane < N_EXPERTS, logits, neg)
        m1 = jnp.max(logits, axis=-1, keepdims=True)
        i1 = jnp.min(jnp.where(logits == m1, lane, float(LANES)), axis=-1, keepdims=True)
        rest = jnp.where(lane == i1, neg, logits)
        m2 = jnp.max(rest, axis=-1, keepdims=True)
        i2 = jnp.min(jnp.where(rest == m2, lane, float(LANES)), axis=-1, keepdims=True)
        e2 = jnp.exp(m2 - m1)
        w1 = 1.0 / (1.0 + e2)
        w2 = e2 / (1.0 + e2)
        cw = jnp.where(lane == i1, w1, jnp.where(lane == i2, w2, 0.0))
        sel = jnp.where(lane == i1, 1.0, jnp.where(lane == i2, 1.0, 0.0))
        if not route_ctx:
            is_ctx = (pl.program_id(0) * nsub + j) % tiles_per_seq == tiles_per_seq - 1
            cw = jnp.where(is_ctx, 0.0, cw)
            sel = jnp.where(is_ctx, 0.0, sel)
        cw_ref[rows, :] = cw
        slot = jnp.where(sel > 0.0, _dot(earlier, sel.astype(BF16)) + count, -1.0)
        slot_ref[rows, :] = slot
        slott_ref[:, rows] = slot.T[:N_EXPERTS, :]
        count = count + jnp.sum(sel, axis=0, keepdims=True)
    cnt_ref[0] = jnp.broadcast_to(count, (N_EXPERTS, LANES))


def _router(h, mod, router, tiles_per_seq, route_ctx):
    r, d = h.shape
    rp = jnp.zeros((d, LANES), F32).at[:, :N_EXPERTS].set(router)
    sub = MOE_TILE // TILE
    row = lambda w: pl.BlockSpec((MOE_TILE, w), lambda i: (i, 0))
    return pl.pallas_call(
        functools.partial(_router_kernel, tiles_per_seq, route_ctx),
        out_shape=(jax.ShapeDtypeStruct((r, d), BF16), jax.ShapeDtypeStruct((r, LANES), F32),
                   jax.ShapeDtypeStruct((r, LANES), F32), jax.ShapeDtypeStruct((N_EXPERTS, r), F32),
                   jax.ShapeDtypeStruct((r // MOE_TILE, N_EXPERTS, LANES), F32)),
        grid=(r // MOE_TILE,),
        in_specs=[row(d),
                  pl.BlockSpec((sub, MOD_ROWS, d), lambda i: (i, 0, 0)),
                  pl.BlockSpec((d, LANES), lambda i: (0, 0))],
        out_specs=(row(d), row(LANES), row(LANES),
                   pl.BlockSpec((N_EXPERTS, MOE_TILE), lambda i: (0, i)),
                   pl.BlockSpec((1, N_EXPERTS, LANES), lambda i: (i, 0, 0))),
        compiler_params=_cparams(("arbitrary",)),
        name="moe_router",
    )(h, mod, rp)


def _moe_kernel(nch_ref, h_ref, u_ref, cw_ref, slot_ref, slott_ref, mod_ref, wg_ref, wu_ref, wd_ref, o_ref,
                xs_scr, y_scr):
    i = pl.program_id(0)
    e = pl.program_id(1)
    f = pl.program_id(2)
    last_f = pl.num_programs(2) - 1
    tile = u_ref.shape[0]
    nch = nch_ref[i * N_EXPERTS + e]
    block = 4 * MOE_CHUNK

    @pl.when((e == 0) & (f == 0))
    def _():
        o_ref[...] = jnp.zeros_like(o_ref)

    nblk = lax.shift_right_logical(nch, 2)
    tail2 = pl.multiple_of(nblk * block, MOE_CHUNK)
    tail1 = pl.multiple_of(tail2 + (nch & 2) * MOE_CHUNK, MOE_CHUNK)

    def run_block(r0, rn):
        rows = pl.ds(r0, rn)

        @pl.when(f == 0)
        def _():
            rid = (lax.broadcasted_iota(jnp.int32, (rn, 1), 0) + r0).astype(F32)
            pick = jnp.where(slott_ref[pl.ds(e, 1), :] == rid, 1.0, 0.0).astype(BF16)
            xs_scr[rows, :] = _dot(pick, u_ref[...]).astype(BF16)

        xs = xs_scr[rows, :]
        a = _silu(_dot(xs, wg_ref[0])) * _dot(xs, wu_ref[0])
        part = _dot(a.astype(BF16), wd_ref[0])

        @pl.when(f == 0)
        def _():
            y_scr[rows, :] = part

        @pl.when(f > 0)
        def _():
            y_scr[rows, :] += part

    def loop_blocks(fn):
        def body(b, carry):
            fn(pl.multiple_of(b * block, block), block)
            return carry
        lax.fori_loop(0, nblk, body, 0)
        pl.when((nch & 2) != 0)(lambda: fn(tail2, 2 * MOE_CHUNK))
        pl.when((nch & 1) != 0)(lambda: fn(tail1, MOE_CHUNK))

    loop_blocks(run_block)

    @pl.when((f == last_f) & (nch > 0))
    def _():
        mine = lax.broadcasted_iota(jnp.int32, (tile, LANES), 1) == e
        slot_col = jnp.sum(jnp.where(mine, slot_ref[...], 0.0), axis=-1, keepdims=True)
        cw_col = jnp.sum(jnp.where(mine, cw_ref[...], 0.0), axis=-1, keepdims=True)

        def scatter_block(r0, rn):
            cid = (lax.broadcasted_iota(jnp.int32, (1, rn), 1) + r0).astype(F32)
            put = jnp.where(slot_col == cid, 1.0, 0.0).astype(BF16)
            o_ref[...] += cw_col * _dot(put, y_scr[pl.ds(r0, rn), :].astype(BF16))

        loop_blocks(scatter_block)

    @pl.when((e == pl.num_programs(1) - 1) & (f == last_f))
    def _():
        for j in range(tile // TILE):
            rows = slice(j * TILE, (j + 1) * TILE)
            o_ref[rows, :] = h_ref[rows, :] + mod_ref[j, G_F:G_F + 1] * o_ref[rows, :]


def _moe(h, u, cw, slot, slott, counts, mod, w_gate, w_up, w_down, layer):
    r, d = h.shape
    _, ne, _, ff = w_gate.shape
    tm, tf = MOE_TILE, 512
    sub = tm // TILE
    nch = ((counts[:, 0, :ne].astype(jnp.int32) + (MOE_CHUNK - 1)) // MOE_CHUNK).reshape(-1)
    once = pl.Buffered(1)
    row = lambda w: pl.BlockSpec((tm, w), lambda i, e, f, n: (i, 0), pipeline_mode=once)
    return pl.pallas_call(
        _moe_kernel,
        out_shape=jax.ShapeDtypeStruct((r, d), F32),
        grid_spec=pltpu.PrefetchScalarGridSpec(
            num_scalar_prefetch=1,
            grid=(r // tm, ne, ff // tf),
            in_specs=[row(d), row(d), row(LANES), row(LANES),
                      pl.BlockSpec((ne, tm), lambda i, e, f, n: (0, i), pipeline_mode=once),
                      pl.BlockSpec((sub, MOD_ROWS, d), lambda i, e, f, n: (i, 0, 0)),
                      pl.BlockSpec((None, 1, d, tf), lambda i, e, f, n: (layer, e, 0, f)),
                      pl.BlockSpec((None, 1, d, tf), lambda i, e, f, n: (layer, e, 0, f)),
                      pl.BlockSpec((None, 1, tf, d), lambda i, e, f, n: (layer, e, f, 0))],
            out_specs=pl.BlockSpec((tm, d), lambda i, e, f, n: (i, 0), pipeline_mode=once),
            scratch_shapes=[pltpu.VMEM((tm, d), BF16), pltpu.VMEM((tm, d), F32)]),
        compiler_params=_cparams(("arbitrary", "arbitrary", "arbitrary")),
        name="moe_experts",
    )(nch, h, u, cw, slot, slott, mod, w_gate, w_up, w_down)


def _final_kernel(h_ref, g_ref, o_ref):
    h = h_ref[0]
    inv = lax.rsqrt(jnp.mean(h * h, axis=-1, keepdims=True) + EPS)
    o_ref[0] = h * inv * g_ref[...]


def _final_norm(h3, gain, seq):
    bsz, _, d = h3.shape
    return pl.pallas_call(
        _final_kernel,
        out_shape=jax.ShapeDtypeStruct((bsz, seq, d), F32),
        grid=(bsz, seq // TILE),
        in_specs=[pl.BlockSpec((1, TILE, d), lambda b, t: (b, t, 0)),
                  pl.BlockSpec((1, d), lambda b, t: (0, 0))],
        out_specs=pl.BlockSpec((1, TILE, d), lambda b, t: (b, t, 0)),
        compiler_params=_cparams(("arbitrary", "arbitrary")),
        name="final_norm",
    )(h3, gain.reshape(1, d))


def _tile_mods(mods, gain_mix, gain_ffn, bsz, lat_tiles):
    sh_m, sc_m, g_m, sh_f, sc_f, g_f = (mods[:, i] for i in range(N_MOD))
    rows = jnp.stack([gain_mix * (1.0 + sc_m), sh_m, g_m, gain_ffn * (1.0 + sc_f), sh_f, g_f,
                      jnp.zeros_like(g_f), jnp.zeros_like(g_f)], axis=1)
    lat = jnp.broadcast_to(rows[:bsz, None], (bsz, lat_tiles) + rows.shape[1:])
    ctx = jnp.broadcast_to(rows[bsz:, None], (bsz, 1) + rows.shape[1:])
    return jnp.concatenate([lat, ctx], axis=1).reshape(bsz * (lat_tiles + 1), MOD_ROWS, rows.shape[-1])


def kernel(x, c, ctx, c_ctx, ada_w, ada_b, norm_mix, norm_ffn, norm_final, s5_lambda_re, s5_lambda_im, s5_log_step, s5_b_re, s5_b_im, s5_c_re, s5_c_im, s5_d, s5_glu_w, s5_glu_b, pool_w, pool_b, pool_scale, fnet_w, fnet_b, ffn_w_gate, ffn_w_up, ffn_w_down, moe_router, moe_w_gate, moe_w_up, moe_w_down):
    bsz, seq, d = x.shape
    ctx_len = ctx.shape[1]
    depth = ada_w.shape[0]
    assert ctx_len == TILE and seq % ROW_BLOCK == 0 and d % LANES == 0
    tot = seq + ctx_len
    lat_tiles = seq // TILE
    tps = lat_tiles + 1
    r = bsz * tot
    assert r % ROW_BLOCK == 0

    cond = jnp.zeros((16, d), F32).at[:bsz].set(c).at[bsz].set(c_ctx)
    mods_all = _ada_all(cond, ada_w, ada_b)[:, :bsz + 1].reshape(depth, bsz + 1, N_MOD, d)

    moe_wg, moe_wu, moe_wd = (w.astype(BF16) for w in (moe_w_gate, moe_w_up, moe_w_down))
    h = None
    for i in range(depth):
        kind, j = i % 3, i // 3
        mod = _tile_mods(mods_all[i], norm_mix[i], norm_ffn[i], bsz, lat_tiles)
        if kind == 0:
            src = (x, ctx) if h is None else (h,)
            u = _s5_pre(src, mod, tps)
            kc, ec, cl, a = _s5_params(s5_lambda_re[j], s5_lambda_im[j], s5_log_step[j],
                                       s5_b_re[j], s5_b_im[j], s5_c_re[j], s5_c_im[j])
            y_lat, y_ctx = _s5_scan(u.reshape(bsz, tot, d), kc, ec, cl, a, seq, ctx_len)
            h = _s5_glu(src, y_lat, y_ctx, mod, s5_d[j], s5_glu_w[j], s5_glu_b[j], seq)
        elif kind == 1:
            h = _pool(h, mod, pool_w[j], pool_b[j], pool_scale[j], seq)
        else:
            g = _fnet_fold(fnet_w[j])
            v3 = _fnet_proj(h, mod, g).reshape(bsz, tot, 2 * d)
            h3 = h.reshape(bsz, tot, d)
            mod4 = mod.reshape(bsz, tps, MOD_ROWS, d)
            part = _fnet_lat(_dft_matrix(seq), v3, h3, mod4, fnet_b[j], seq)
            cc, sc = _dft_tables(ctx_len)
            dftc = jnp.asarray(np.concatenate([cc, sc], axis=1), BF16)
            h = _fnet_ctx(dftc, v3, h3, mod4, fnet_b[j], part, seq).reshape(r, d)
        kk = i // 2
        if i % 2 == 0:
            h = _ffn(h, mod, ffn_w_gate[kk], ffn_w_up[kk], ffn_w_down[kk])
        else:
            ub, cw, slot, slott, counts = _router(h, mod, moe_router[kk], tps, route_ctx=i < depth - 1)
            h = _moe(h, ub, cw, slot, slott, counts, mod, moe_wg, moe_wu, moe_wd, kk)
    return _final_norm(h.reshape(bsz, tot, d), norm_final, seq)
```

```python
import functools
import math

import numpy as np
import jax
import jax.numpy as jnp
from jax import lax
from jax.experimental import pallas as pl
from jax.experimental.pallas import tpu as pltpu

F32 = jnp.float32
BF16 = jnp.bfloat16

EPS = 1e-6
GRID_W = 64
N_MOD = 6
S5_GROUP = 16
S5_STATE = 64
S5_CHUNK = 16
POOL_WINDOWS = (2, 4, 8, 16)
FNET_GROUPS = 4
N_EXPERTS = 8

LANES = 128
SUBLANES = 8
TILE = 256
ROW_BLOCK = 1024
VMEM_LIMIT = 56 * 1024 * 1024


def _cparams(sem, vmem=VMEM_LIMIT):
    return pltpu.CompilerParams(dimension_semantics=sem, vmem_limit_bytes=vmem)


def _dot(a, b):
    return jnp.dot(a, b, preferred_element_type=F32)


def _split(a):
    hi = a.astype(BF16)
    lo = (a - hi.astype(F32)).astype(BF16)
    return hi, lo


def _dot3(a, b):
    a_hi, a_lo = _split(a)
    b_hi, b_lo = _split(b)
    return _dot(a_hi, b_hi) + (_dot(a_lo, b_hi) + _dot(a_hi, b_lo))


def _normmod(h, gs, sh):
    inv = lax.rsqrt(jnp.mean(h * h, axis=-1, keepdims=True) + EPS)
    return h * inv * gs + sh


GS_M, SH_M, G_M, GS_F, SH_F, G_F = range(6)
MOD_ROWS = 8


def _ada_kernel(c_ref, w_ref, b_ref, o_ref):
    c = c_ref[...]
    s = c * (1.0 / (1.0 + jnp.exp(-c)))
    o_ref[0] = _dot3(s, w_ref[0]) + b_ref[0]


def _ada_all(cond, ada_w, ada_b):
    depth, d, n = ada_w.shape
    rows = cond.shape[0]
    tn = 1536
    return pl.pallas_call(
        _ada_kernel,
        out_shape=jax.ShapeDtypeStruct((depth, rows, n), F32),
        grid=(depth, n // tn),
        in_specs=[
            pl.BlockSpec((rows, d), lambda i, j: (0, 0)),
            pl.BlockSpec((1, d, tn), lambda i, j: (i, 0, j)),
            pl.BlockSpec((1, 1, tn), lambda i, j: (i, 0, j)),
        ],
        out_specs=pl.BlockSpec((1, rows, tn), lambda i, j: (i, 0, j)),
        compiler_params=_cparams(("arbitrary", "arbitrary")),
        name="ada_params",
    )(cond, ada_w, ada_b.reshape(depth, 1, n))


def _stream_specs(src, tps, d):
    if len(src) == 1:
        return [pl.BlockSpec((TILE, d), lambda i: (i, 0))]
    return [pl.BlockSpec((None, TILE, d), lambda i: (i // tps, jnp.minimum(i % tps, tps - 2), 0)),
            pl.BlockSpec((None, TILE, d), lambda i: (i // tps, 0, 0))]


def _stream_tile(refs, tps):
    if len(refs) == 1:
        return refs[0][...]
    is_ctx = (pl.program_id(0) % tps) == tps - 1
    return jnp.where(is_ctx, refs[1][...], refs[0][...])


def _s5_pre_kernel(n_src, tps, *refs):
    mod_ref, u_ref = refs[n_src:]
    if n_src == 1:
        for j in range(u_ref.shape[0] // TILE):
            rows = slice(j * TILE, (j + 1) * TILE)
            mod = mod_ref[j]
            u_ref[rows, :] = _normmod(refs[0][rows, :], mod[GS_M:GS_M + 1], mod[SH_M:SH_M + 1])
    else:
        mod = mod_ref[0]
        u_ref[...] = _normmod(_stream_tile(refs[:n_src], tps), mod[GS_M:GS_M + 1], mod[SH_M:SH_M + 1])


def _s5_pre(src, mod, tps):
    r, _, d = mod.shape
    r *= TILE
    if len(src) == 1:
        rows = ROW_BLOCK
        src_specs = [pl.BlockSpec((rows, d), lambda i: (i, 0))]
    else:
        rows = TILE
        src_specs = _stream_specs(src, tps, d)
    return pl.pallas_call(
        functools.partial(_s5_pre_kernel, len(src), tps),
        out_shape=jax.ShapeDtypeStruct((r, d), F32),
        grid=(r // rows,),
        in_specs=src_specs + [pl.BlockSpec((rows // TILE, MOD_ROWS, d), lambda i: (i, 0, 0))],
        out_specs=pl.BlockSpec((rows, d), lambda i: (i, 0)),
        compiler_params=_cparams(("arbitrary",)),
        name="s5_pre",
    )(*src, mod)


def _s5_params(lam_re, lam_im, log_step, b_re, b_im, c_re, c_im):
    hp = lax.Precision.HIGHEST
    L = S5_CHUNK
    n_dir, g, p = lam_re.shape
    gb = LANES // S5_GROUP
    nb = g // gb
    hch = S5_GROUP
    step = jnp.exp(log_step)[..., None]
    ar, ai = lam_re * step, lam_im * step
    k = jnp.arange(L + 1, dtype=F32)[None, :, None, None]
    mag = jnp.exp(ar[:, None] * k)
    ang = ai[:, None] * k
    pr, pi = mag * jnp.cos(ang), mag * jnp.sin(ang)
    lbr, lbi = pr[:, 1] - 1.0, pi[:, 1]
    den = lam_re * lam_re + lam_im * lam_im
    qr = (lbr * lam_re + lbi * lam_im) / den
    qi = (lbi * lam_re - lbr * lam_im) / den
    bbr = qr[..., None] * b_re - qi[..., None] * b_im
    bbi = qr[..., None] * b_im + qi[..., None] * b_re
    er = pr[..., None] * bbr[:, None] - pi[..., None] * bbi[:, None]
    ei = pr[..., None] * bbi[:, None] + pi[..., None] * bbr[:, None]
    kmat = (jnp.einsum('dgop,dkgpi->dgiko', c_re, er[:, :L], precision=hp)
            - jnp.einsum('dgop,dkgpi->dgiko', c_im, ei[:, :L], precision=hp))
    kc = kmat.reshape(n_dir, nb, gb * hch, L * hch)
    es = jnp.stack([er[:, L - 1::-1], ei[:, L - 1::-1]], axis=0)
    ec = jnp.transpose(es, (1, 3, 2, 5, 0, 4)).reshape(n_dir, nb, gb, L, hch, 2, p)
    ec = jnp.transpose(ec, (0, 1, 3, 2, 4, 5, 6)).reshape(n_dir, nb, L, gb * hch, 2 * p)
    clr = c_re[:, None] * pr[:, 1:, :, None, :] - c_im[:, None] * pi[:, 1:, :, None, :]
    cli = c_re[:, None] * pi[:, 1:, :, None, :] + c_im[:, None] * pr[:, 1:, :, None, :]
    cs = jnp.stack([clr, -cli], axis=0)
    cl = jnp.transpose(cs, (1, 3, 0, 5, 2, 4)).reshape(n_dir, nb, gb, 2, p, L, hch)
    cl = jnp.transpose(cl, (0, 1, 3, 2, 4, 5, 6)).reshape(n_dir, nb, 2 * gb * p, L * hch)
    a = jnp.stack([pr[:, L].reshape(n_dir, nb, 1, gb * p), pi[:, L].reshape(n_dir, nb, 1, gb * p)], axis=3)
    return kc.astype(BF16), ec.astype(BF16), cl.astype(BF16), a.reshape(n_dir, nb, 1, 2 * gb * p)


def _replicators():
    gb = LANES // S5_GROUP
    r16 = np.zeros((S5_CHUNK, S5_GROUP, S5_CHUNK, gb, S5_GROUP), np.float32)
    for k in range(S5_CHUNK):
        for o in range(S5_GROUP):
            r16[k, o, k, :, o] = 1.0
    r64 = np.zeros((2, S5_STATE, 2, gb, S5_STATE), np.float32)
    for part in range(2):
        for q in range(S5_STATE):
            r64[part, q, part, :, q] = 1.0
    return (jnp.asarray(r16.reshape(S5_CHUNK * S5_GROUP, S5_CHUNK * LANES), BF16),
            jnp.asarray(r64.reshape(2 * S5_STATE, 2 * gb * S5_STATE), BF16))


def _s5_scan_kernel(ul_ref, uc_ref, kc_ref, ec_ref, cl_ref, r16_ref, r64_ref, a_ref, yl_ref, yc_ref,
                    h_scr, s_scr, hs_scr, wts_scr, wy_scr):
    d = pl.program_id(1)
    t = pl.program_id(2)
    L = S5_CHUNK
    nstate = a_ref.shape[-1]
    half = nstate // 2
    nt = L * LANES
    blk = lambda i: slice(i * LANES, (i + 1) * LANES)

    def build_weights():
        row_gi = lax.broadcasted_iota(jnp.int32, (LANES, LANES), 0) // S5_GROUP
        lane_go = lax.broadcasted_iota(jnp.int32, (LANES, LANES), 1) // S5_GROUP
        toep = _dot(kc_ref[0, 0], r16_ref[...])
        zero = jnp.zeros((LANES, LANES), BF16)
        for lag in range(L):
            piece = jnp.where(row_gi == lane_go, toep[:, blk(lag)], 0.0).astype(BF16)
            for s in range(L - lag):
                wts_scr[blk(s), blk(s + lag)] = piece
        for tt in range(0, L, 2):
            wts_scr[blk(tt + 1), blk(tt)] = zero
        row_gi_w = lax.broadcasted_iota(jnp.int32, (LANES, nstate), 0) // S5_GROUP
        lane_gp = (lax.broadcasted_iota(jnp.int32, (LANES, nstate), 1) % half) // S5_STATE
        for s in range(L):
            e = _dot(ec_ref[0, 0, s], r64_ref[...])
            wts_scr[blk(s), nt:] = jnp.where(row_gi_w == lane_gp, e, 0.0).astype(BF16)
        row_gp = (lax.broadcasted_iota(jnp.int32, (nstate, LANES), 0) % half) // S5_STATE
        lane_go_t = lax.broadcasted_iota(jnp.int32, (nstate, LANES), 1) // S5_GROUP
        for tt in range(L):
            w = _dot(cl_ref[0, 0], r16_ref[:, blk(tt)])
            wy_scr[:, blk(tt)] = jnp.where(row_gp == lane_go_t, w, 0.0).astype(BF16)

    def process(x_ref, y_ref):
        bsz, tok, _ = x_ref.shape
        nc = tok // L
        m = bsz * nc
        flip = lambda i, n: i + d * (n - 1 - 2 * i)
        u = jnp.concatenate(
            [x_ref[:, pl.ds(flip(s, L), nc, stride=L), :].reshape(m, LANES) for s in range(L)], axis=1).astype(BF16)
        s_all = _dot(u, wts_scr[:, nt:])
        nq = nstate // LANES
        nh = nq // 2
        ncp = nc + SUBLANES
        for q in range(nq):
            for b in range(bsz):
                s_scr[q, b * ncp:b * ncp + nc, :] = s_all[b * nc:(b + 1) * nc, blk(q)]
        a = a_ref[0, 0]
        a_re = [jnp.broadcast_to(a[:, blk(q)], (bsz, LANES)) for q in range(nh)]
        a_im = [jnp.broadcast_to(a[:, blk(nh + q)], (bsz, LANES)) for q in range(nh)]
        h_re = [h_scr[q] for q in range(nh)]
        h_im = [h_scr[nh + q] for q in range(nh)]
        for c in range(nc):
            rows = pl.ds(flip(c, nc), bsz, stride=ncp)
            for q in range(nh):
                hs_scr.at[q][rows, :] = h_re[q]
                hs_scr.at[nh + q][rows, :] = h_im[q]
                s_re = s_scr.at[q][rows, :]
                s_im = s_scr.at[nh + q][rows, :]
                h_re[q], h_im[q] = (a_re[q] * h_re[q] - a_im[q] * h_im[q] + s_re,
                                    a_re[q] * h_im[q] + a_im[q] * h_re[q] + s_im)
        for q in range(nh):
            h_scr[q] = h_re[q]
            h_scr[nh + q] = h_im[q]
        hs = jnp.concatenate(
            [jnp.concatenate([hs_scr[q, b * ncp:b * ncp + nc, :] for b in range(bsz)], axis=0) for q in range(nq)],
            axis=1).astype(BF16)
        for n in range(L // 2):
            cols = slice(2 * n * LANES, 2 * (n + 1) * LANES)
            y = _dot(u[:, :cols.stop], wts_scr[:cols.stop, cols]) + _dot(hs, wy_scr[:, cols])
            for tt in (2 * n, 2 * n + 1):
                y_ref[:, pl.ds(flip(tt, L), nc, stride=L), :] = (
                    y[:, blk(tt - 2 * n)].reshape(bsz, nc, LANES))

    @pl.when(t == 0)
    def _():
        build_weights()
        h_scr[...] = jnp.zeros_like(h_scr)
        process(uc_ref, yc_ref)

    @pl.when(t > 0)
    def _():
        process(ul_ref, yl_ref)


def _s5_scan(u3, kc, ec, cl, a, seq, ctx_len):
    bsz, _, d = u3.shape
    tl = 1024
    ntl = seq // tl
    nb = d // LANES
    n_dir = 2
    ctx_blk = seq // ctx_len
    nstate = a.shape[-1]
    nt = S5_CHUNK * LANES
    r16, r64 = _replicators()

    def lat_idx(dd, t):
        i = jnp.maximum(t - 1, 0)
        return jnp.where(dd == 0, i, ntl - 1 - i)

    per_block = lambda arr: pl.BlockSpec((1, 1) + arr.shape[2:], lambda j, dd, t: (dd, j) + (0,) * (arr.ndim - 2))
    const = lambda arr: pl.BlockSpec(arr.shape, lambda j, dd, t: (0,) * arr.ndim)
    return pl.pallas_call(
        _s5_scan_kernel,
        out_shape=(jax.ShapeDtypeStruct((n_dir, bsz, seq, d), F32),
                   jax.ShapeDtypeStruct((n_dir, bsz, ctx_len, d), F32)),
        grid=(nb, n_dir, ntl + 1),
        in_specs=[
            pl.BlockSpec((bsz, tl, LANES), lambda j, dd, t: (0, lat_idx(dd, t), j)),
            pl.BlockSpec((bsz, ctx_len, LANES), lambda j, dd, t: (0, ctx_blk, j)),
            per_block(kc), per_block(ec), per_block(cl), const(r16), const(r64), per_block(a),
        ],
        out_specs=(
            pl.BlockSpec((None, bsz, tl, LANES), lambda j, dd, t: (dd, 0, lat_idx(dd, t), j)),
            pl.BlockSpec((None, bsz, ctx_len, LANES), lambda j, dd, t: (dd, 0, 0, j)),
        ),
        scratch_shapes=[
            pltpu.VMEM((nstate // LANES, bsz, LANES), F32),
            pltpu.VMEM((nstate // LANES, bsz * (tl // S5_CHUNK + SUBLANES), LANES), F32),
            pltpu.VMEM((nstate // LANES, bsz * (tl // S5_CHUNK + SUBLANES), LANES), F32),
            pltpu.VMEM((nt, nt + nstate), BF16),
            pltpu.VMEM((nstate, nt), BF16),
        ],
        compiler_params=_cparams(("arbitrary", "arbitrary", "arbitrary")),
        name="s5_scan",
    )(u3, u3, kc, ec, cl, r16, r64, a)


def _gelu_tanh(x):
    return 0.5 * x * (1.0 + jnp.tanh(math.sqrt(2.0 / math.pi) * (x + 0.044715 * (x * x * x))))


def _s5_glu_kernel(n_src, tiles_per_seq, *refs):
    ylf_ref, ylr_ref, ycf_ref, ycr_ref, mod_ref, dvec_ref, w_ref, b_ref, o_ref = refs[n_src:]
    is_ctx = (pl.program_id(0) % tiles_per_seq) == tiles_per_seq - 1
    y_scan = jnp.where(is_ctx, ycf_ref[0, 0] + ycr_ref[0, 0], ylf_ref[0, 0] + ylr_ref[0, 0])
    h = _stream_tile(refs[:n_src], tiles_per_seq)
    mod = mod_ref[0]
    y = dvec_ref[...] * _normmod(h, mod[GS_M:GS_M + 1], mod[SH_M:SH_M + 1]) + y_scan
    z = _gelu_tanh(y)
    gate = 1.0 / (1.0 + jnp.exp(-(_dot(z.astype(BF16), w_ref[...]) + b_ref[...])))
    o_ref[...] = h + mod[G_M:G_M + 1] * (z * gate)


def _s5_glu(src, y_lat, y_ctx, mod, dvec, glu_w, glu_b, seq):
    d = mod.shape[-1]
    r = mod.shape[0] * TILE
    lat_tiles = seq // TILE
    tps = lat_tiles + 1

    def lat_map(dd):
        return lambda i: (dd, i // tps, jnp.minimum(i % tps, lat_tiles - 1), 0)

    def ctx_map(dd):
        return lambda i: (dd, i // tps, 0, 0)

    row = pl.BlockSpec((TILE, d), lambda i: (i, 0))
    vec = pl.BlockSpec((1, d), lambda i: (0, 0))
    return pl.pallas_call(
        functools.partial(_s5_glu_kernel, len(src), tps),
        out_shape=jax.ShapeDtypeStruct((r, d), F32),
        grid=(r // TILE,),
        in_specs=_stream_specs(src, tps, d) + [
                  pl.BlockSpec((1, 1, TILE, d), lat_map(0)), pl.BlockSpec((1, 1, TILE, d), lat_map(1)),
                  pl.BlockSpec((1, 1, TILE, d), ctx_map(0)), pl.BlockSpec((1, 1, TILE, d), ctx_map(1)),
                  pl.BlockSpec((1, MOD_ROWS, d), lambda i: (i, 0, 0)),
                  vec, pl.BlockSpec((d, d), lambda i: (0, 0)), vec],
        out_specs=row,
        compiler_params=_cparams(("arbitrary",)),
        name="s5_glu",
    )(*src, y_lat, y_lat, y_ctx, y_ctx, mod, dvec.reshape(1, d), glu_w.astype(BF16), glu_b.reshape(1, d))


def _pool_matrices(width, tile):
    mats = np.zeros((len(POOL_WINDOWS), tile, tile), np.float32)
    pos = np.arange(width)
    for g, win in enumerate(POOL_WINDOWS):
        lo = np.clip(pos - win // 2, 0, width)
        hi = np.clip(pos + win // 2, 0, width)
        for base in range(0, tile, width):
            for j in range(width):
                mats[g, base + j, base + lo[j]:base + hi[j]] = 1.0 / float(hi[j] - lo[j])
    return mats


def _pool_kernel(tps, h_ref, mod_ref, a_ref, w_ref, b_ref, sc_ref, o_ref):
    gw = w_ref.shape[-1]
    nsub = h_ref.shape[0] // TILE
    for j in range(nsub):
        rows = slice(j * TILE, (j + 1) * TILE)
        mod = mod_ref[j]
        is_ctx = (pl.program_id(0) * nsub + j) % tps == tps - 1
        h = h_ref[rows, :]
        u = _normmod(h, mod[GS_M:GS_M + 1], mod[SH_M:SH_M + 1])
        ub = u.astype(BF16)
        for g in range(w_ref.shape[0]):
            cols = slice(g * gw, (g + 1) * gw)
            avg = jnp.where(is_ctx, a_ref[1, g], a_ref[0, g])
            res = _dot(avg, ub[:, cols]) - u[:, cols]
            mixed = _dot(res.astype(BF16), w_ref[g]) + b_ref[:, cols]
            o_ref[rows, cols] = h[:, cols] + mod[G_M:G_M + 1, cols] * (mixed * sc_ref[:, cols])


def _pool(h, mod, pool_w, pool_b, pool_scale, seq):
    r, d = h.shape
    tps = seq // TILE + 1
    ng, gw, _ = pool_w.shape
    mats = jnp.asarray(np.stack([_pool_matrices(GRID_W, TILE), _pool_matrices(TILE, TILE)]), BF16)
    row = pl.BlockSpec((ROW_BLOCK, d), lambda i: (i, 0))
    vec = pl.BlockSpec((1, d), lambda i: (0, 0))
    return pl.pallas_call(
        functools.partial(_pool_kernel, tps),
        out_shape=jax.ShapeDtypeStruct((r, d), F32),
        grid=(r // ROW_BLOCK,),
        in_specs=[row, pl.BlockSpec((ROW_BLOCK // TILE, MOD_ROWS, d), lambda i: (i, 0, 0)),
                  pl.BlockSpec((2, ng, TILE, TILE), lambda i: (0, 0, 0, 0)),
                  pl.BlockSpec((ng, gw, gw), lambda i: (0, 0, 0)), vec, vec],
        out_specs=row,
        compiler_params=_cparams(("arbitrary",)),
        name="pool_mixer",
    )(h, mod, mats, pool_w.astype(BF16), pool_b.reshape(1, d), pool_scale.reshape(1, d))


def _dft_tables(n):
    idx = np.arange(n)
    ang = 2.0 * np.pi * ((idx[:, None] * idx[None, :]) % n) / n
    scale = 1.0 / math.sqrt(n)
    return (np.cos(ang) * scale).astype(np.float32), (np.sin(ang) * scale).astype(np.float32)


def _fnet_fold_kernel(cc_ref, sc_ref, w_ref, o_ref):
    w = w_ref[...]
    d = w.shape[-1]
    o_ref[:, :d] = _dot3(cc_ref[...], w).astype(o_ref.dtype)
    o_ref[:, d:] = (-_dot3(sc_ref[...], w)).astype(o_ref.dtype)


def _fnet_fold(fnet_w):
    d = fnet_w.shape[0]
    gw = d // FNET_GROUPS
    cc, sc = _dft_tables(gw)
    sq = pl.BlockSpec((gw, gw), lambda i: (0, 0))
    return pl.pallas_call(
        _fnet_fold_kernel,
        out_shape=jax.ShapeDtypeStruct((d, 2 * d), BF16),
        grid=(FNET_GROUPS,),
        in_specs=[sq, sq, pl.BlockSpec((gw, d), lambda i: (i, 0))],
        out_specs=pl.BlockSpec((gw, 2 * d), lambda i: (i, 0)),
        compiler_params=_cparams(("arbitrary",)),
        name="fnet_fold",
    )(jnp.asarray(cc), jnp.asarray(sc), fnet_w)


def _fnet_proj_kernel(h_ref, mod_ref, g_ref, v_ref):
    for j in range(h_ref.shape[0] // TILE):
        rows = slice(j * TILE, (j + 1) * TILE)
        mod = mod_ref[j]
        u = _normmod(h_ref[rows, :], mod[GS_M:GS_M + 1], mod[SH_M:SH_M + 1])
        v_ref[rows, :] = _dot(u.astype(BF16), g_ref[...]).astype(v_ref.dtype)


def _fnet_proj(h, mod, g):
    r, d = h.shape
    sub = ROW_BLOCK // TILE
    return pl.pallas_call(
        _fnet_proj_kernel,
        out_shape=jax.ShapeDtypeStruct((r, 2 * d), BF16),
        grid=(r // ROW_BLOCK,),
        in_specs=[pl.BlockSpec((ROW_BLOCK, d), lambda i: (i, 0)),
                  pl.BlockSpec((sub, MOD_ROWS, d), lambda i: (i, 0, 0)),
                  pl.BlockSpec((d, 2 * d), lambda i: (0, 0))],
        out_specs=pl.BlockSpec((ROW_BLOCK, 2 * d), lambda i: (i, 0)),
        compiler_params=_cparams(("arbitrary",)),
        name="fnet_proj",
    )(h, mod, g)


def _dft_matrix(n):
    n1 = int(round(math.sqrt(n)))
    assert n1 * n1 == n
    hi = jnp.arange(n1, dtype=jnp.int32)[:, None]
    k = jnp.arange(n, dtype=jnp.int32)[None, :]
    ang_a = (2.0 * math.pi / n1) * ((hi * k) % n1).astype(F32)
    ang_b = (2.0 * math.pi / n) * ((hi * k) % n).astype(F32)
    ca, sa, cb, sb = jnp.cos(ang_a), jnp.sin(ang_a), jnp.cos(ang_b), jnp.sin(ang_b)
    scale = 1.0 / math.sqrt(n)
    c = (ca[:, None] * cb[None] - sa[:, None] * sb[None]).reshape(n, n) * scale
    s = (sa[:, None] * cb[None] + ca[:, None] * sb[None]).reshape(n, n) * scale
    return jnp.concatenate([c, s], axis=1).astype(BF16)


def _fnet_lat_kernel(f_ref, v_ref, h_ref, mod_ref, b_ref, o_ref, acc_ref):
    k = pl.program_id(2)

    @pl.when(k == 0)
    def _():
        acc_ref[...] = jnp.zeros_like(acc_ref)

    acc_ref[...] += _dot(f_ref[...], v_ref[0])

    @pl.when(k == pl.num_programs(2) - 1)
    def _():
        for j in range(h_ref.shape[1] // TILE):
            rows = slice(j * TILE, (j + 1) * TILE)
            o_ref[0, rows, :] = h_ref[0, rows, :] + mod_ref[0, j, G_M:G_M + 1] * (acc_ref[rows, :] + b_ref[...])


def _fnet_lat(dft, v3, h3, mod4, fnet_b, seq):
    bsz, tot, d = h3.shape
    tm, tk = ROW_BLOCK, min(2048, seq)
    kh = seq // tk
    sub = tm // TILE
    return pl.pallas_call(
        _fnet_lat_kernel,
        out_shape=jax.ShapeDtypeStruct((bsz, tot, d), F32),
        grid=(bsz, seq // tm, 2 * kh),
        in_specs=[pl.BlockSpec((tm, tk), lambda b, m, k: (m, k)),
                  pl.BlockSpec((1, tk, d), lambda b, m, k: (b, k % kh, k // kh)),
                  pl.BlockSpec((1, tm, d), lambda b, m, k: (b, m, 0)),
                  pl.BlockSpec((1, sub, MOD_ROWS, d), lambda b, m, k: (b, m, 0, 0)),
                  pl.BlockSpec((1, d), lambda b, m, k: (0, 0))],
        out_specs=pl.BlockSpec((1, tm, d), lambda b, m, k: (b, m, 0)),
        scratch_shapes=[pltpu.VMEM((tm, d), F32)],
        compiler_params=_cparams(("arbitrary", "arbitrary", "arbitrary")),
        name="fnet_dft_latent",
    )(dft, v3, h3, mod4, fnet_b.reshape(1, d))


def _fnet_ctx_kernel(f_ref, v_ref, h_ref, mod_ref, b_ref, prev_ref, o_ref):
    del prev_ref
    n = f_ref.shape[0]
    d = h_ref.shape[-1]
    acc = _dot(f_ref[:, :n], v_ref[0, :, :d]) + _dot(f_ref[:, n:], v_ref[0, :, d:])
    o_ref[0] = h_ref[0] + mod_ref[0, 0, G_M:G_M + 1] * (acc + b_ref[...])


def _fnet_ctx(dftc, v3, h3, mod4, fnet_b, partial, seq):
    bsz, tot, d = h3.shape
    n = tot - seq
    blk = seq // n
    return pl.pallas_call(
        _fnet_ctx_kernel,
        out_shape=jax.ShapeDtypeStruct((bsz, tot, d), F32),
        grid=(bsz,),
        in_specs=[pl.BlockSpec((n, 2 * n), lambda b: (0, 0)),
                  pl.BlockSpec((1, n, 2 * d), lambda b: (b, blk, 0)),
                  pl.BlockSpec((1, n, d), lambda b: (b, blk, 0)),
                  pl.BlockSpec((1, 1, MOD_ROWS, d), lambda b: (b, blk, 0, 0)),
                  pl.BlockSpec((1, d), lambda b: (0, 0)),
                  pl.BlockSpec(memory_space=pl.ANY)],
        out_specs=pl.BlockSpec((1, n, d), lambda b: (b, blk, 0)),
        input_output_aliases={5: 0},
        compiler_params=_cparams(("arbitrary",)),
        name="fnet_dft_context",
    )(dftc, v3, h3, mod4, fnet_b.reshape(1, d), partial)


def _silu(x):
    return x * (1.0 / (1.0 + jnp.exp(-x)))


def _ffn_kernel(h_ref, mod_ref, wg_ref, wu_ref, wd_ref, o_ref, u_scr, acc_ref):
    f = pl.program_id(1)
    sub = h_ref.shape[0] // TILE

    @pl.when(f == 0)
    def _():
        for j in range(sub):
            rows = slice(j * TILE, (j + 1) * TILE)
            mod = mod_ref[j]
            u_scr[rows, :] = _normmod(h_ref[rows, :], mod[GS_F:GS_F + 1], mod[SH_F:SH_F + 1]).astype(BF16)
        acc_ref[...] = jnp.zeros_like(acc_ref)

    u = u_scr[...]
    a = _silu(_dot(u, wg_ref[...])) * _dot(u, wu_ref[...])
    acc_ref[...] += _dot(a.astype(BF16), wd_ref[...])

    @pl.when(f == pl.num_programs(1) - 1)
    def _():
        for j in range(sub):
            rows = slice(j * TILE, (j + 1) * TILE)
            o_ref[rows, :] = h_ref[rows, :] + mod_ref[j, G_F:G_F + 1] * acc_ref[rows, :]


def _ffn(h, mod, w_gate, w_up, w_down):
    r, d = h.shape
    ff = w_gate.shape[1]
    tm, tf = ROW_BLOCK, 256
    sub = tm // TILE
    return pl.pallas_call(
        _ffn_kernel,
        out_shape=jax.ShapeDtypeStruct((r, d), F32),
        grid=(r // tm, ff // tf),
        in_specs=[pl.BlockSpec((tm, d), lambda i, f: (i, 0)),
                  pl.BlockSpec((sub, MOD_ROWS, d), lambda i, f: (i, 0, 0)),
                  pl.BlockSpec((d, tf), lambda i, f: (0, f)),
                  pl.BlockSpec((d, tf), lambda i, f: (0, f)),
                  pl.BlockSpec((tf, d), lambda i, f: (f, 0))],
        out_specs=pl.BlockSpec((tm, d), lambda i, f: (i, 0)),
        scratch_shapes=[pltpu.VMEM((tm, d), BF16), pltpu.VMEM((tm, d), F32)],
        compiler_params=_cparams(("arbitrary", "arbitrary")),
        name="ffn_swiglu",
    )(h, mod, w_gate.astype(BF16), w_up.astype(BF16), w_down.astype(BF16))


MOE_TILE = 2048
MOE_CHUNK = 128


def _router_kernel(tiles_per_seq, route_ctx, h_ref, mod_ref, r_ref, u_ref, cw_ref, slot_ref, slott_ref, cnt_ref):
    nsub = h_ref.shape[0] // TILE
    lane = lax.broadcasted_iota(jnp.int32, (TILE, LANES), 1).astype(F32)
    earlier = jnp.where(lax.broadcasted_iota(jnp.int32, (TILE, TILE), 1)
                        < lax.broadcasted_iota(jnp.int32, (TILE, TILE), 0), 1.0, 0.0).astype(BF16)
    neg = jnp.float32(-jnp.inf)
    count = jnp.zeros((1, LANES), F32)
    for j in range(nsub):
        rows = slice(j * TILE, (j + 1) * TILE)
        mod = mod_ref[j]
        u = _normmod(h_ref[rows, :], mod[GS_F:GS_F + 1], mod[SH_F:SH_F + 1])
        u_ref[rows, :] = u.astype(BF16)
        logits = _dot3(u, r_ref[...])
        logits = jnp.where(lane < N_EXPERTS, logits, neg)
        m1 = jnp.max(logits, axis=-1, keepdims=True)
        i1 = jnp.min(jnp.where(logits == m1, lane, float(LANES)), axis=-1, keepdims=True)
        rest = jnp.where(lane == i1, neg, logits)
        m2 = jnp.max(rest, axis=-1, keepdims=True)
        i2 = jnp.min(jnp.where(rest == m2, lane, float(LANES)), axis=-1, keepdims=True)
        e2 = jnp.exp(m2 - m1)
        w1 = 1.0 / (1.0 + e2)
        w2 = e2 / (1.0 + e2)
        cw = jnp.where(lane == i1, w1, jnp.where(lane == i2, w2, 0.0))
        sel = jnp.where(lane == i1, 1.0, jnp.where(lane == i2, 1.0, 0.0))
        if not route_ctx:
            is_ctx = (pl.program_id(0) * nsub + j) % tiles_per_seq == tiles_per_seq - 1
            cw = jnp.where(is_ctx, 0.0, cw)
            sel = jnp.where(is_ctx, 0.0, sel)
        cw_ref[rows, :] = cw
        slot = jnp.where(sel > 0.0, _dot(earlier, sel.astype(BF16)) + count, -1.0)
        slot_ref[rows, :] = slot
        slott_ref[:, rows] = slot.T[:N_EXPERTS, :]
        count = count + jnp.sum(sel, axis=0, keepdims=True)
    cnt_ref[0] = jnp.broadcast_to(count, (N_EXPERTS, LANES))


def _router(h, mod, router, tiles_per_seq, route_ctx):
    r, d = h.shape
    rp = jnp.zeros((d, LANES), F32).at[:, :N_EXPERTS].set(router)
    sub = MOE_TILE // TILE
    row = lambda w: pl.BlockSpec((MOE_TILE, w), lambda i: (i, 0))
    return pl.pallas_call(
        functools.partial(_router_kernel, tiles_per_seq, route_ctx),
        out_shape=(jax.ShapeDtypeStruct((r, d), BF16), jax.ShapeDtypeStruct((r, LANES), F32),
                   jax.ShapeDtypeStruct((r, LANES), F32), jax.ShapeDtypeStruct((N_EXPERTS, r), F32),
                   jax.ShapeDtypeStruct((r // MOE_TILE, N_EXPERTS, LANES), F32)),
        grid=(r // MOE_TILE,),
        in_specs=[row(d),
                  pl.BlockSpec((sub, MOD_ROWS, d), lambda i: (i, 0, 0)),
                  pl.BlockSpec((d, LANES), lambda i: (0, 0))],
        out_specs=(row(d), row(LANES), row(LANES),
                   pl.BlockSpec((N_EXPERTS, MOE_TILE), lambda i: (0, i)),
                   pl.BlockSpec((1, N_EXPERTS, LANES), lambda i: (i, 0, 0))),
        compiler_params=_cparams(("arbitrary",)),
        name="moe_router",
    )(h, mod, rp)


def _moe_kernel(nch_ref, h_ref, u_ref, cw_ref, slot_ref, slott_ref, mod_ref, wg_ref, wu_ref, wd_ref, o_ref,
                xs_scr, y_scr):
    i = pl.program_id(0)
    e = pl.program_id(1)
    f = pl.program_id(2)
    last_f = pl.num_programs(2) - 1
    tile = u_ref.shape[0]
    nch = nch_ref[i * N_EXPERTS + e]
    block = 4 * MOE_CHUNK

    @pl.when((e == 0) & (f == 0))
    def _():
        o_ref[...] = jnp.zeros_like(o_ref)

    nblk = lax.shift_right_logical(nch, 2)
    tail2 = pl.multiple_of(nblk * block, MOE_CHUNK)
    tail1 = pl.multiple_of(tail2 + (nch & 2) * MOE_CHUNK, MOE_CHUNK)

    def run_block(r0, rn):
        rows = pl.ds(r0, rn)

        @pl.when(f == 0)
        def _():
            rid = (lax.broadcasted_iota(jnp.int32, (rn, 1), 0) + r0).astype(F32)
            pick = jnp.where(slott_ref[pl.ds(e, 1), :] == rid, 1.0, 0.0).astype(BF16)
            xs_scr[rows, :] = _dot(pick, u_ref[...]).astype(BF16)

        xs = xs_scr[rows, :]
        a = _silu(_dot(xs, wg_ref[0])) * _dot(xs, wu_ref[0])
        part = _dot(a.astype(BF16), wd_ref[0])

        @pl.when(f == 0)
        def _():
            y_scr[rows, :] = part

        @pl.when(f > 0)
        def _():
            y_scr[rows, :] += part

    def loop_blocks(fn):
        def body(b, carry):
            fn(pl.multiple_of(b * block, block), block)
            return carry
        lax.fori_loop(0, nblk, body, 0)
        pl.when((nch & 2) != 0)(lambda: fn(tail2, 2 * MOE_CHUNK))
        pl.when((nch & 1) != 0)(lambda: fn(tail1, MOE_CHUNK))

    loop_blocks(run_block)

    @pl.when((f == last_f) & (nch > 0))
    def _():
        mine = lax.broadcasted_iota(jnp.int32, (tile, LANES), 1) == e
        slot_col = jnp.sum(jnp.where(mine, slot_ref[...], 0.0), axis=-1, keepdims=True)
        cw_col = jnp.sum(jnp.where(mine, cw_ref[...], 0.0), axis=-1, keepdims=True)

        def scatter_block(r0, rn):
            cid = (lax.broadcasted_iota(jnp.int32, (1, rn), 1) + r0).astype(F32)
            put = jnp.where(slot_col == cid, 1.0, 0.0).astype(BF16)
            o_ref[...] += cw_col * _dot(put, y_scr[pl.ds(r0, rn), :].astype(BF16))

        loop_blocks(scatter_block)

    @pl.when((e == pl.num_programs(1) - 1) & (f == last_f))
    def _():
        for j in range(tile // TILE):
            rows = slice(j * TILE, (j + 1) * TILE)
            o_ref[rows, :] = h_ref[rows, :] + mod_ref[j, G_F:G_F + 1] * o_ref[rows, :]


def _moe(h, u, cw, slot, slott, counts, mod, w_gate, w_up, w_down, layer):
    r, d = h.shape
    _, ne, _, ff = w_gate.shape
    tm, tf = MOE_TILE, 512
    sub = tm // TILE
    nch = ((counts[:, 0, :ne].astype(jnp.int32) + (MOE_CHUNK - 1)) // MOE_CHUNK).reshape(-1)
    once = pl.Buffered(1)
    row = lambda w: pl.BlockSpec((tm, w), lambda i, e, f, n: (i, 0), pipeline_mode=once)
    return pl.pallas_call(
        _moe_kernel,
        out_shape=jax.ShapeDtypeStruct((r, d), F32),
        grid_spec=pltpu.PrefetchScalarGridSpec(
            num_scalar_prefetch=1,
            grid=(r // tm, ne, ff // tf),
            in_specs=[row(d), row(d), row(LANES), row(LANES),
                      pl.BlockSpec((ne, tm), lambda i, e, f, n: (0, i), pipeline_mode=once),
                      pl.BlockSpec((sub, MOD_ROWS, d), lambda i, e, f, n: (i, 0, 0)),
                      pl.BlockSpec((None, 1, d, tf), lambda i, e, f, n: (layer, e, 0, f)),
                      pl.BlockSpec((None, 1, d, tf), lambda i, e, f, n: (layer, e, 0, f)),
                      pl.BlockSpec((None, 1, tf, d), lambda i, e, f, n: (layer, e, f, 0))],
            out_specs=pl.BlockSpec((tm, d), lambda i, e, f, n: (i, 0), pipeline_mode=once),
            scratch_shapes=[pltpu.VMEM((tm, d), BF16), pltpu.VMEM((tm, d), F32)]),
        compiler_params=_cparams(("arbitrary", "arbitrary", "arbitrary")),
        name="moe_experts",
    )(nch, h, u, cw, slot, slott, mod, w_gate, w_up, w_down)


def _final_kernel(h_ref, g_ref, o_ref):
    h = h_ref[0]
    inv = lax.rsqrt(jnp.mean(h * h, axis=-1, keepdims=True) + EPS)
    o_ref[0] = h * inv * g_ref[...]


def _final_norm(h3, gain, seq):
    bsz, _, d = h3.shape
    return pl.pallas_call(
        _final_kernel,
        out_shape=jax.ShapeDtypeStruct((bsz, seq, d), F32),
        grid=(bsz, seq // ROW_BLOCK),
        in_specs=[pl.BlockSpec((1, ROW_BLOCK, d), lambda b, t: (b, t, 0)),
                  pl.BlockSpec((1, d), lambda b, t: (0, 0))],
        out_specs=pl.BlockSpec((1, ROW_BLOCK, d), lambda b, t: (b, t, 0)),
        compiler_params=_cparams(("arbitrary", "arbitrary")),
        name="final_norm",
    )(h3, gain.reshape(1, d))


def _tile_mods(mods, gain_mix, gain_ffn, bsz, lat_tiles):
    sh_m, sc_m, g_m, sh_f, sc_f, g_f = (mods[:, i] for i in range(N_MOD))
    rows = jnp.stack([gain_mix * (1.0 + sc_m), sh_m, g_m, gain_ffn * (1.0 + sc_f), sh_f, g_f,
                      jnp.zeros_like(g_f), jnp.zeros_like(g_f)], axis=1)
    lat = jnp.broadcast_to(rows[:bsz, None], (bsz, lat_tiles) + rows.shape[1:])
    ctx = jnp.broadcast_to(rows[bsz:, None], (bsz, 1) + rows.shape[1:])
    return jnp.concatenate([lat, ctx], axis=1).reshape(bsz * (lat_tiles + 1), MOD_ROWS, rows.shape[-1])


def kernel(x, c, ctx, c_ctx, ada_w, ada_b, norm_mix, norm_ffn, norm_final, s5_lambda_re, s5_lambda_im, s5_log_step, s5_b_re, s5_b_im, s5_c_re, s5_c_im, s5_d, s5_glu_w, s5_glu_b, pool_w, pool_b, pool_scale, fnet_w, fnet_b, ffn_w_gate, ffn_w_up, ffn_w_down, moe_router, moe_w_gate, moe_w_up, moe_w_down):
    bsz, seq, d = x.shape
    ctx_len = ctx.shape[1]
    depth = ada_w.shape[0]
    assert ctx_len == TILE and seq % ROW_BLOCK == 0 and d % LANES == 0
    tot = seq + ctx_len
    lat_tiles = seq // TILE
    tps = lat_tiles + 1
    r = bsz * tot
    assert r % ROW_BLOCK == 0

    cond = jnp.zeros((16, d), F32).at[:bsz].set(c).at[bsz].set(c_ctx)
    mods_all = _ada_all(cond, ada_w, ada_b)[:, :bsz + 1].reshape(depth, bsz + 1, N_MOD, d)

    moe_wg, moe_wu, moe_wd = (w.astype(BF16) for w in (moe_w_gate, moe_w_up, moe_w_down))
    mods_tiled = jax.vmap(lambda m, gm, gf: _tile_mods(m, gm, gf, bsz, lat_tiles))(mods_all, norm_mix, norm_ffn)
    s5_tables = jax.vmap(_s5_params)(s5_lambda_re, s5_lambda_im, s5_log_step, s5_b_re, s5_b_im, s5_c_re, s5_c_im)
    h = None
    for i in range(depth):
        kind, j = i % 3, i // 3
        mod = mods_tiled[i]
        if kind == 0:
            src = (x, ctx) if h is None else (h,)
            u = _s5_pre(src, mod, tps)
            kc, ec, cl, a = (tbl[j] for tbl in s5_tables)
            y_lat, y_ctx = _s5_scan(u.reshape(bsz, tot, d), kc, ec, cl, a, seq, ctx_len)
            h = _s5_glu(src, y_lat, y_ctx, mod, s5_d[j], s5_glu_w[j], s5_glu_b[j], seq)
        elif kind == 1:
            h = _pool(h, mod, pool_w[j], pool_b[j], pool_scale[j], seq)
        else:
            g = _fnet_fold(fnet_w[j])
            v3 = _fnet_proj(h, mod, g).reshape(bsz, tot, 2 * d)
            h3 = h.reshape(bsz, tot, d)
            mod4 = mod.reshape(bsz, tps, MOD_ROWS, d)
            part = _fnet_lat(_dft_matrix(seq), v3, h3, mod4, fnet_b[j], seq)
            cc, sc = _dft_tables(ctx_len)
            dftc = jnp.asarray(np.concatenate([cc, sc], axis=1), BF16)
            h = _fnet_ctx(dftc, v3, h3, mod4, fnet_b[j], part, seq).reshape(r, d)
        kk = i // 2
        if i % 2 == 0:
            h = _ffn(h, mod, ffn_w_gate[kk], ffn_w_up[kk], ffn_w_down[kk])
        else:
            ub, cw, slot, slott, counts = _router(h, mod, moe_router[kk], tps, route_ctx=i < depth - 1)
            h = _moe(h, ub, cw, slot, slott, counts, mod, moe_wg, moe_wu, moe_wd, kk)
    return _final_norm(h.reshape(bsz, tot, d), norm_final, seq)
```

```python
import functools
import math

import numpy as np
import jax
import jax.numpy as jnp
from jax import lax
from jax.experimental import pallas as pl
from jax.experimental.pallas import tpu as pltpu

F32 = jnp.float32
BF16 = jnp.bfloat16

EPS = 1e-6
GRID_W = 64
N_MOD = 6
S5_GROUP = 16
S5_STATE = 64
S5_CHUNK = 16
POOL_WINDOWS = (2, 4, 8, 16)
FNET_GROUPS = 4
N_EXPERTS = 8

LANES = 128
SUBLANES = 8
TILE = 256
ROW_BLOCK = 1024
VMEM_LIMIT = 56 * 1024 * 1024


def _cparams(sem, vmem=VMEM_LIMIT):
    return pltpu.CompilerParams(dimension_semantics=sem, vmem_limit_bytes=vmem)


def _dot(a, b):
    return jnp.dot(a, b, preferred_element_type=F32)


def _split(a):
    hi = a.astype(BF16)
    lo = (a - hi.astype(F32)).astype(BF16)
    return hi, lo


def _dot3(a, b):
    a_hi, a_lo = _split(a)
    b_hi, b_lo = _split(b)
    return _dot(a_hi, b_hi) + (_dot(a_lo, b_hi) + _dot(a_hi, b_lo))


def _normmod(h, gs, sh):
    inv = lax.rsqrt(jnp.mean(h * h, axis=-1, keepdims=True) + EPS)
    return h * inv * gs + sh


GS_M, SH_M, G_M, GS_F, SH_F, G_F = range(6)
MOD_ROWS = 8


def _ada_kernel(c_ref, w_ref, b_ref, o_ref):
    c = c_ref[...]
    s = c * (1.0 / (1.0 + jnp.exp(-c)))
    o_ref[0] = _dot3(s, w_ref[0]) + b_ref[0]


def _ada_all(cond, ada_w, ada_b):
    depth, d, n = ada_w.shape
    rows = cond.shape[0]
    tn = 1536
    return pl.pallas_call(
        _ada_kernel,
        out_shape=jax.ShapeDtypeStruct((depth, rows, n), F32),
        grid=(depth, n // tn),
        in_specs=[
            pl.BlockSpec((rows, d), lambda i, j: (0, 0)),
            pl.BlockSpec((1, d, tn), lambda i, j: (i, 0, j)),
            pl.BlockSpec((1, 1, tn), lambda i, j: (i, 0, j)),
        ],
        out_specs=pl.BlockSpec((1, rows, tn), lambda i, j: (i, 0, j)),
        compiler_params=_cparams(("arbitrary", "arbitrary")),
        name="ada_params",
    )(cond, ada_w, ada_b.reshape(depth, 1, n))


def _stream_specs(src, tps, d):
    if len(src) == 1:
        return [pl.BlockSpec((TILE, d), lambda i: (i, 0))]
    return [pl.BlockSpec((None, TILE, d), lambda i: (i // tps, jnp.minimum(i % tps, tps - 2), 0)),
            pl.BlockSpec((None, TILE, d), lambda i: (i // tps, 0, 0))]


def _stream_tile(refs, tps):
    if len(refs) == 1:
        return refs[0][...]
    is_ctx = (pl.program_id(0) % tps) == tps - 1
    return jnp.where(is_ctx, refs[1][...], refs[0][...])


def _s5_pre_kernel(n_src, tps, *refs):
    mod_ref, u_ref = refs[n_src:]
    if n_src == 1:
        for j in range(u_ref.shape[0] // TILE):
            rows = slice(j * TILE, (j + 1) * TILE)
            mod = mod_ref[j]
            u_ref[rows, :] = _normmod(refs[0][rows, :], mod[GS_M:GS_M + 1], mod[SH_M:SH_M + 1])
    else:
        mod = mod_ref[0]
        u_ref[...] = _normmod(_stream_tile(refs[:n_src], tps), mod[GS_M:GS_M + 1], mod[SH_M:SH_M + 1])


def _s5_pre(src, mod, tps):
    r, _, d = mod.shape
    r *= TILE
    if len(src) == 1:
        rows = ROW_BLOCK
        src_specs = [pl.BlockSpec((rows, d), lambda i: (i, 0))]
    else:
        rows = TILE
        src_specs = _stream_specs(src, tps, d)
    return pl.pallas_call(
        functools.partial(_s5_pre_kernel, len(src), tps),
        out_shape=jax.ShapeDtypeStruct((r, d), F32),
        grid=(r // rows,),
        in_specs=src_specs + [pl.BlockSpec((rows // TILE, MOD_ROWS, d), lambda i: (i, 0, 0))],
        out_specs=pl.BlockSpec((rows, d), lambda i: (i, 0)),
        compiler_params=_cparams(("arbitrary",)),
        name="s5_pre",
    )(*src, mod)


def _s5_params(lam_re, lam_im, log_step, b_re, b_im, c_re, c_im):
    hp = lax.Precision.HIGHEST
    L = S5_CHUNK
    n_dir, g, p = lam_re.shape
    gb = LANES // S5_GROUP
    nb = g // gb
    hch = S5_GROUP
    step = jnp.exp(log_step)[..., None]
    ar, ai = lam_re * step, lam_im * step
    k = jnp.arange(L + 1, dtype=F32)[None, :, None, None]
    mag = jnp.exp(ar[:, None] * k)
    ang = ai[:, None] * k
    pr, pi = mag * jnp.cos(ang), mag * jnp.sin(ang)
    lbr, lbi = pr[:, 1] - 1.0, pi[:, 1]
    den = lam_re * lam_re + lam_im * lam_im
    qr = (lbr * lam_re + lbi * lam_im) / den
    qi = (lbi * lam_re - lbr * lam_im) / den
    bbr = qr[..., None] * b_re - qi[..., None] * b_im
    bbi = qr[..., None] * b_im + qi[..., None] * b_re
    er = pr[..., None] * bbr[:, None] - pi[..., None] * bbi[:, None]
    ei = pr[..., None] * bbi[:, None] + pi[..., None] * bbr[:, None]
    kmat = (jnp.einsum('dgop,dkgpi->dgiko', c_re, er[:, :L], precision=hp)
            - jnp.einsum('dgop,dkgpi->dgiko', c_im, ei[:, :L], precision=hp))
    kc = kmat.reshape(n_dir, nb, gb * hch, L * hch)
    es = jnp.stack([er[:, L - 1::-1], ei[:, L - 1::-1]], axis=0)
    ec = jnp.transpose(es, (1, 3, 2, 5, 0, 4)).reshape(n_dir, nb, gb, L, hch, 2, p)
    ec = jnp.transpose(ec, (0, 1, 3, 2, 4, 5, 6)).reshape(n_dir, nb, L, gb * hch, 2 * p)
    clr = c_re[:, None] * pr[:, 1:, :, None, :] - c_im[:, None] * pi[:, 1:, :, None, :]
    cli = c_re[:, None] * pi[:, 1:, :, None, :] + c_im[:, None] * pr[:, 1:, :, None, :]
    cs = jnp.stack([clr, -cli], axis=0)
    cl = jnp.transpose(cs, (1, 3, 0, 5, 2, 4)).reshape(n_dir, nb, gb, 2, p, L, hch)
    cl = jnp.transpose(cl, (0, 1, 3, 2, 4, 5, 6)).reshape(n_dir, nb, 2 * gb * p, L * hch)
    a = jnp.stack([pr[:, L].reshape(n_dir, nb, 1, gb * p), pi[:, L].reshape(n_dir, nb, 1, gb * p)], axis=3)
    return kc.astype(BF16), ec.astype(BF16), cl.astype(BF16), a.reshape(n_dir, nb, 1, 2 * gb * p)


def _replicators():
    gb = LANES // S5_GROUP
    r16 = np.zeros((S5_CHUNK, S5_GROUP, S5_CHUNK, gb, S5_GROUP), np.float32)
    for k in range(S5_CHUNK):
        for o in range(S5_GROUP):
            r16[k, o, k, :, o] = 1.0
    r64 = np.zeros((2, S5_STATE, 2, gb, S5_STATE), np.float32)
    for part in range(2):
        for q in range(S5_STATE):
            r64[part, q, part, :, q] = 1.0
    return (jnp.asarray(r16.reshape(S5_CHUNK * S5_GROUP, S5_CHUNK * LANES), BF16),
            jnp.asarray(r64.reshape(2 * S5_STATE, 2 * gb * S5_STATE), BF16))


def _s5_scan_kernel(ul_ref, uc_ref, kc_ref, ec_ref, cl_ref, r16_ref, r64_ref, a_ref, yl_ref, yc_ref,
                    h_scr, s_scr, hs_scr, wts_scr, wy_scr):
    d = pl.program_id(1)
    t = pl.program_id(2)
    L = S5_CHUNK
    nstate = a_ref.shape[-1]
    half = nstate // 2
    nt = L * LANES
    blk = lambda i: slice(i * LANES, (i + 1) * LANES)

    def build_weights():
        row_gi = lax.broadcasted_iota(jnp.int32, (LANES, LANES), 0) // S5_GROUP
        lane_go = lax.broadcasted_iota(jnp.int32, (LANES, LANES), 1) // S5_GROUP
        toep = _dot(kc_ref[0, 0], r16_ref[...])
        zero = jnp.zeros((LANES, LANES), BF16)
        for lag in range(L):
            piece = jnp.where(row_gi == lane_go, toep[:, blk(lag)], 0.0).astype(BF16)
            for s in range(L - lag):
                wts_scr[blk(s), blk(s + lag)] = piece
        for tt in range(0, L, 2):
            wts_scr[blk(tt + 1), blk(tt)] = zero
        row_gi_w = lax.broadcasted_iota(jnp.int32, (LANES, nstate), 0) // S5_GROUP
        lane_gp = (lax.broadcasted_iota(jnp.int32, (LANES, nstate), 1) % half) // S5_STATE
        for s in range(L):
            e = _dot(ec_ref[0, 0, s], r64_ref[...])
            wts_scr[blk(s), nt:] = jnp.where(row_gi_w == lane_gp, e, 0.0).astype(BF16)
        row_gp = (lax.broadcasted_iota(jnp.int32, (nstate, LANES), 0) % half) // S5_STATE
        lane_go_t = lax.broadcasted_iota(jnp.int32, (nstate, LANES), 1) // S5_GROUP
        for tt in range(L):
            w = _dot(cl_ref[0, 0], r16_ref[:, blk(tt)])
            wy_scr[:, blk(tt)] = jnp.where(row_gp == lane_go_t, w, 0.0).astype(BF16)

    def process(x_ref, y_ref):
        bsz, tok, _ = x_ref.shape
        nc = tok // L
        m = bsz * nc
        flip = lambda i, n: i + d * (n - 1 - 2 * i)
        u = jnp.concatenate(
            [x_ref[:, pl.ds(flip(s, L), nc, stride=L), :].reshape(m, LANES) for s in range(L)], axis=1).astype(BF16)
        s_all = _dot(u, wts_scr[:, nt:])
        nq = nstate // LANES
        nh = nq // 2
        ncp = nc + SUBLANES
        for q in range(nq):
            for b in range(bsz):
                s_scr[q, b * ncp:b * ncp + nc, :] = s_all[b * nc:(b + 1) * nc, blk(q)]
        a = a_ref[0, 0]
        a_re = [jnp.broadcast_to(a[:, blk(q)], (bsz, LANES)) for q in range(nh)]
        a_im = [jnp.broadcast_to(a[:, blk(nh + q)], (bsz, LANES)) for q in range(nh)]
        h_re = [h_scr[q] for q in range(nh)]
        h_im = [h_scr[nh + q] for q in range(nh)]
        for c in range(nc):
            rows = pl.ds(flip(c, nc), bsz, stride=ncp)
            for q in range(nh):
                hs_scr.at[q][rows, :] = h_re[q]
                hs_scr.at[nh + q][rows, :] = h_im[q]
                s_re = s_scr.at[q][rows, :]
                s_im = s_scr.at[nh + q][rows, :]
                h_re[q], h_im[q] = (a_re[q] * h_re[q] - a_im[q] * h_im[q] + s_re,
                                    a_re[q] * h_im[q] + a_im[q] * h_re[q] + s_im)
        for q in range(nh):
            h_scr[q] = h_re[q]
            h_scr[nh + q] = h_im[q]
        hs = jnp.concatenate(
            [jnp.concatenate([hs_scr[q, b * ncp:b * ncp + nc, :] for b in range(bsz)], axis=0) for q in range(nq)],
            axis=1).astype(BF16)
        for n in range(L // 2):
            cols = slice(2 * n * LANES, 2 * (n + 1) * LANES)
            y = _dot(u[:, :cols.stop], wts_scr[:cols.stop, cols]) + _dot(hs, wy_scr[:, cols])
            for tt in (2 * n, 2 * n + 1):
                y_ref[:, pl.ds(flip(tt, L), nc, stride=L), :] = (
                    y[:, blk(tt - 2 * n)].reshape(bsz, nc, LANES))

    @pl.when(t == 0)
    def _():
        build_weights()
        h_scr[...] = jnp.zeros_like(h_scr)
        process(uc_ref, yc_ref)

    @pl.when(t > 0)
    def _():
        process(ul_ref, yl_ref)


def _s5_scan(u3, kc, ec, cl, a, layer, seq, ctx_len):
    bsz, _, d = u3.shape
    tl = 1024
    ntl = seq // tl
    nb = d // LANES
    n_dir = 2
    ctx_blk = seq // ctx_len
    nstate = a.shape[-1]
    nt = S5_CHUNK * LANES
    r16, r64 = _replicators()

    def lat_idx(dd, t):
        i = jnp.maximum(t - 1, 0)
        return jnp.where(dd == 0, i, ntl - 1 - i)

    per_block = lambda arr: pl.BlockSpec((None, 1, 1) + arr.shape[3:],
                                         lambda j, dd, t: (layer, dd, j) + (0,) * (arr.ndim - 3))
    const = lambda arr: pl.BlockSpec(arr.shape, lambda j, dd, t: (0,) * arr.ndim)
    return pl.pallas_call(
        _s5_scan_kernel,
        out_shape=(jax.ShapeDtypeStruct((n_dir, bsz, seq, d), F32),
                   jax.ShapeDtypeStruct((n_dir, bsz, ctx_len, d), F32)),
        grid=(nb, n_dir, ntl + 1),
        in_specs=[
            pl.BlockSpec((bsz, tl, LANES), lambda j, dd, t: (0, lat_idx(dd, t), j)),
            pl.BlockSpec((bsz, ctx_len, LANES), lambda j, dd, t: (0, ctx_blk, j)),
            per_block(kc), per_block(ec), per_block(cl), const(r16), const(r64), per_block(a),
        ],
        out_specs=(
            pl.BlockSpec((None, bsz, tl, LANES), lambda j, dd, t: (dd, 0, lat_idx(dd, t), j)),
            pl.BlockSpec((None, bsz, ctx_len, LANES), lambda j, dd, t: (dd, 0, 0, j)),
        ),
        scratch_shapes=[
            pltpu.VMEM((nstate // LANES, bsz, LANES), F32),
            pltpu.VMEM((nstate // LANES, bsz * (tl // S5_CHUNK + SUBLANES), LANES), F32),
            pltpu.VMEM((nstate // LANES, bsz * (tl // S5_CHUNK + SUBLANES), LANES), F32),
            pltpu.VMEM((nt, nt + nstate), BF16),
            pltpu.VMEM((nstate, nt), BF16),
        ],
        compiler_params=_cparams(("arbitrary", "arbitrary", "arbitrary")),
        name="s5_scan",
    )(u3, u3, kc, ec, cl, r16, r64, a)


def _gelu_tanh(x):
    return 0.5 * x * (1.0 + jnp.tanh(math.sqrt(2.0 / math.pi) * (x + 0.044715 * (x * x * x))))


def _s5_glu_kernel(n_src, tiles_per_seq, *refs):
    ylf_ref, ylr_ref, ycf_ref, ycr_ref, mod_ref, dvec_ref, w_ref, b_ref, o_ref = refs[n_src:]
    is_ctx = (pl.program_id(0) % tiles_per_seq) == tiles_per_seq - 1
    y_scan = jnp.where(is_ctx, ycf_ref[0, 0] + ycr_ref[0, 0], ylf_ref[0, 0] + ylr_ref[0, 0])
    h = _stream_tile(refs[:n_src], tiles_per_seq)
    mod = mod_ref[0]
    y = dvec_ref[...] * _normmod(h, mod[GS_M:GS_M + 1], mod[SH_M:SH_M + 1]) + y_scan
    z = _gelu_tanh(y)
    gate = 1.0 / (1.0 + jnp.exp(-(_dot(z.astype(BF16), w_ref[...]) + b_ref[...])))
    o_ref[...] = h + mod[G_M:G_M + 1] * (z * gate)


def _s5_glu(src, y_lat, y_ctx, mod, dvec, glu_w, glu_b, seq):
    d = mod.shape[-1]
    r = mod.shape[0] * TILE
    lat_tiles = seq // TILE
    tps = lat_tiles + 1

    def lat_map(dd):
        return lambda i: (dd, i // tps, jnp.minimum(i % tps, lat_tiles - 1), 0)

    def ctx_map(dd):
        return lambda i: (dd, i // tps, 0, 0)

    row = pl.BlockSpec((TILE, d), lambda i: (i, 0))
    vec = pl.BlockSpec((1, d), lambda i: (0, 0))
    return pl.pallas_call(
        functools.partial(_s5_glu_kernel, len(src), tps),
        out_shape=jax.ShapeDtypeStruct((r, d), F32),
        grid=(r // TILE,),
        in_specs=_stream_specs(src, tps, d) + [
                  pl.BlockSpec((1, 1, TILE, d), lat_map(0)), pl.BlockSpec((1, 1, TILE, d), lat_map(1)),
                  pl.BlockSpec((1, 1, TILE, d), ctx_map(0)), pl.BlockSpec((1, 1, TILE, d), ctx_map(1)),
                  pl.BlockSpec((1, MOD_ROWS, d), lambda i: (i, 0, 0)),
                  vec, pl.BlockSpec((d, d), lambda i: (0, 0)), vec],
        out_specs=row,
        compiler_params=_cparams(("arbitrary",)),
        name="s5_glu",
    )(*src, y_lat, y_lat, y_ctx, y_ctx, mod, dvec.reshape(1, d), glu_w.astype(BF16), glu_b.reshape(1, d))


def _pool_matrices(width, tile):
    mats = np.zeros((len(POOL_WINDOWS), tile, tile), np.float32)
    pos = np.arange(width)
    for g, win in enumerate(POOL_WINDOWS):
        lo = np.clip(pos - win // 2, 0, width)
        hi = np.clip(pos + win // 2, 0, width)
        for base in range(0, tile, width):
            for j in range(width):
                mats[g, base + j, base + lo[j]:base + hi[j]] = 1.0 / float(hi[j] - lo[j])
    return mats


def _pool_kernel(tps, h_ref, mod_ref, a_ref, w_ref, b_ref, sc_ref, o_ref):
    gw = w_ref.shape[-1]
    nsub = h_ref.shape[0] // TILE
    for j in range(nsub):
        rows = slice(j * TILE, (j + 1) * TILE)
        mod = mod_ref[j]
        is_ctx = (pl.program_id(0) * nsub + j) % tps == tps - 1
        h = h_ref[rows, :]
        u = _normmod(h, mod[GS_M:GS_M + 1], mod[SH_M:SH_M + 1])
        ub = u.astype(BF16)
        for g in range(w_ref.shape[0]):
            cols = slice(g * gw, (g + 1) * gw)
            avg = jnp.where(is_ctx, a_ref[1, g], a_ref[0, g])
            res = _dot(avg, ub[:, cols]) - u[:, cols]
            mixed = _dot(res.astype(BF16), w_ref[g]) + b_ref[:, cols]
            o_ref[rows, cols] = h[:, cols] + mod[G_M:G_M + 1, cols] * (mixed * sc_ref[:, cols])


def _pool(h, mod, pool_w, pool_b, pool_scale, seq):
    r, d = h.shape
    tps = seq // TILE + 1
    ng, gw, _ = pool_w.shape
    mats = jnp.asarray(np.stack([_pool_matrices(GRID_W, TILE), _pool_matrices(TILE, TILE)]), BF16)
    row = pl.BlockSpec((ROW_BLOCK, d), lambda i: (i, 0))
    vec = pl.BlockSpec((1, d), lambda i: (0, 0))
    return pl.pallas_call(
        functools.partial(_pool_kernel, tps),
        out_shape=jax.ShapeDtypeStruct((r, d), F32),
        grid=(r // ROW_BLOCK,),
        in_specs=[row, pl.BlockSpec((ROW_BLOCK // TILE, MOD_ROWS, d), lambda i: (i, 0, 0)),
                  pl.BlockSpec((2, ng, TILE, TILE), lambda i: (0, 0, 0, 0)),
                  pl.BlockSpec((ng, gw, gw), lambda i: (0, 0, 0)), vec, vec],
        out_specs=row,
        compiler_params=_cparams(("arbitrary",)),
        name="pool_mixer",
    )(h, mod, mats, pool_w.astype(BF16), pool_b.reshape(1, d), pool_scale.reshape(1, d))


def _dft_tables(n):
    idx = np.arange(n)
    ang = 2.0 * np.pi * ((idx[:, None] * idx[None, :]) % n) / n
    scale = 1.0 / math.sqrt(n)
    return (np.cos(ang) * scale).astype(np.float32), (np.sin(ang) * scale).astype(np.float32)


def _fnet_fold_kernel(cc_ref, sc_ref, w_ref, o_ref):
    w = w_ref[...]
    d = w.shape[-1]
    o_ref[:, :d] = _dot3(cc_ref[...], w).astype(o_ref.dtype)
    o_ref[:, d:] = (-_dot3(sc_ref[...], w)).astype(o_ref.dtype)


def _fnet_fold(fnet_w):
    d = fnet_w.shape[0]
    gw = d // FNET_GROUPS
    cc, sc = _dft_tables(gw)
    sq = pl.BlockSpec((gw, gw), lambda i: (0, 0))
    return pl.pallas_call(
        _fnet_fold_kernel,
        out_shape=jax.ShapeDtypeStruct((d, 2 * d), BF16),
        grid=(FNET_GROUPS,),
        in_specs=[sq, sq, pl.BlockSpec((gw, d), lambda i: (i, 0))],
        out_specs=pl.BlockSpec((gw, 2 * d), lambda i: (i, 0)),
        compiler_params=_cparams(("arbitrary",)),
        name="fnet_fold",
    )(jnp.asarray(cc), jnp.asarray(sc), fnet_w)


def _fnet_proj_kernel(h_ref, mod_ref, g_ref, v_ref):
    for j in range(h_ref.shape[0] // TILE):
        rows = slice(j * TILE, (j + 1) * TILE)
        mod = mod_ref[j]
        u = _normmod(h_ref[rows, :], mod[GS_M:GS_M + 1], mod[SH_M:SH_M + 1])
        v_ref[rows, :] = _dot(u.astype(BF16), g_ref[...]).astype(v_ref.dtype)


def _fnet_proj(h, mod, g):
    r, d = h.shape
    sub = ROW_BLOCK // TILE
    return pl.pallas_call(
        _fnet_proj_kernel,
        out_shape=jax.ShapeDtypeStruct((r, 2 * d), BF16),
        grid=(r // ROW_BLOCK,),
        in_specs=[pl.BlockSpec((ROW_BLOCK, d), lambda i: (i, 0)),
                  pl.BlockSpec((sub, MOD_ROWS, d), lambda i: (i, 0, 0)),
                  pl.BlockSpec((d, 2 * d), lambda i: (0, 0))],
        out_specs=pl.BlockSpec((ROW_BLOCK, 2 * d), lambda i: (i, 0)),
        compiler_params=_cparams(("arbitrary",)),
        name="fnet_proj",
    )(h, mod, g)


def _dft_matrix(n):
    n1 = int(round(math.sqrt(n)))
    assert n1 * n1 == n
    hi = jnp.arange(n1, dtype=jnp.int32)[:, None]
    k = jnp.arange(n, dtype=jnp.int32)[None, :]
    ang_a = (2.0 * math.pi / n1) * ((hi * k) % n1).astype(F32)
    ang_b = (2.0 * math.pi / n) * ((hi * k) % n).astype(F32)
    ca, sa, cb, sb = jnp.cos(ang_a), jnp.sin(ang_a), jnp.cos(ang_b), jnp.sin(ang_b)
    scale = 1.0 / math.sqrt(n)
    c = (ca[:, None] * cb[None] - sa[:, None] * sb[None]).reshape(n, n) * scale
    s = (sa[:, None] * cb[None] + ca[:, None] * sb[None]).reshape(n, n) * scale
    return jnp.concatenate([c, s], axis=1).astype(BF16)


def _fnet_lat_kernel(f_ref, v_ref, h_ref, mod_ref, b_ref, o_ref, acc_ref):
    k = pl.program_id(2)

    @pl.when(k == 0)
    def _():
        acc_ref[...] = jnp.zeros_like(acc_ref)

    acc_ref[...] += _dot(f_ref[...], v_ref[0])

    @pl.when(k == pl.num_programs(2) - 1)
    def _():
        for j in range(h_ref.shape[1] // TILE):
            rows = slice(j * TILE, (j + 1) * TILE)
            o_ref[0, rows, :] = h_ref[0, rows, :] + mod_ref[0, j, G_M:G_M + 1] * (acc_ref[rows, :] + b_ref[...])


def _fnet_lat(dft, v3, h3, mod4, fnet_b, seq):
    bsz, tot, d = h3.shape
    tm, tk = ROW_BLOCK, min(2048, seq)
    kh = seq // tk
    sub = tm // TILE
    return pl.pallas_call(
        _fnet_lat_kernel,
        out_shape=jax.ShapeDtypeStruct((bsz, tot, d), F32),
        grid=(bsz, seq // tm, 2 * kh),
        in_specs=[pl.BlockSpec((tm, tk), lambda b, m, k: (m, k)),
                  pl.BlockSpec((1, tk, d), lambda b, m, k: (b, k % kh, k // kh)),
                  pl.BlockSpec((1, tm, d), lambda b, m, k: (b, m, 0)),
                  pl.BlockSpec((1, sub, MOD_ROWS, d), lambda b, m, k: (b, m, 0, 0)),
                  pl.BlockSpec((1, d), lambda b, m, k: (0, 0))],
        out_specs=pl.BlockSpec((1, tm, d), lambda b, m, k: (b, m, 0)),
        scratch_shapes=[pltpu.VMEM((tm, d), F32)],
        compiler_params=_cparams(("arbitrary", "arbitrary", "arbitrary")),
        name="fnet_dft_latent",
    )(dft, v3, h3, mod4, fnet_b.reshape(1, d))


def _fnet_ctx_kernel(f_ref, v_ref, h_ref, mod_ref, b_ref, prev_ref, o_ref):
    del prev_ref
    n = f_ref.shape[0]
    d = h_ref.shape[-1]
    acc = _dot(f_ref[:, :n], v_ref[0, :, :d]) + _dot(f_ref[:, n:], v_ref[0, :, d:])
    o_ref[0] = h_ref[0] + mod_ref[0, 0, G_M:G_M + 1] * (acc + b_ref[...])


def _fnet_ctx(dftc, v3, h3, mod4, fnet_b, partial, seq):
    bsz, tot, d = h3.shape
    n = tot - seq
    blk = seq // n
    return pl.pallas_call(
        _fnet_ctx_kernel,
        out_shape=jax.ShapeDtypeStruct((bsz, tot, d), F32),
        grid=(bsz,),
        in_specs=[pl.BlockSpec((n, 2 * n), lambda b: (0, 0)),
                  pl.BlockSpec((1, n, 2 * d), lambda b: (b, blk, 0)),
                  pl.BlockSpec((1, n, d), lambda b: (b, blk, 0)),
                  pl.BlockSpec((1, 1, MOD_ROWS, d), lambda b: (b, blk, 0, 0)),
                  pl.BlockSpec((1, d), lambda b: (0, 0)),
                  pl.BlockSpec(memory_space=pl.ANY)],
        out_specs=pl.BlockSpec((1, n, d), lambda b: (b, blk, 0)),
        input_output_aliases={5: 0},
        compiler_params=_cparams(("arbitrary",)),
        name="fnet_dft_context",
    )(dftc, v3, h3, mod4, fnet_b.reshape(1, d), partial)


def _silu(x):
    return x * (1.0 / (1.0 + jnp.exp(-x)))


def _ffn_kernel(h_ref, mod_ref, wg_ref, wu_ref, wd_ref, o_ref, u_scr, acc_ref):
    f = pl.program_id(1)
    sub = h_ref.shape[0] // TILE

    @pl.when(f == 0)
    def _():
        for j in range(sub):
            rows = slice(j * TILE, (j + 1) * TILE)
            mod = mod_ref[j]
            u_scr[rows, :] = _normmod(h_ref[rows, :], mod[GS_F:GS_F + 1], mod[SH_F:SH_F + 1]).astype(BF16)
        acc_ref[...] = jnp.zeros_like(acc_ref)

    u = u_scr[...]
    a = _silu(_dot(u, wg_ref[...])) * _dot(u, wu_ref[...])
    acc_ref[...] += _dot(a.astype(BF16), wd_ref[...])

    @pl.when(f == pl.num_programs(1) - 1)
    def _():
        for j in range(sub):
            rows = slice(j * TILE, (j + 1) * TILE)
            o_ref[rows, :] = h_ref[rows, :] + mod_ref[j, G_F:G_F + 1] * acc_ref[rows, :]


def _ffn(h, mod, w_gate, w_up, w_down):
    r, d = h.shape
    ff = w_gate.shape[1]
    tm, tf = ROW_BLOCK, 256
    sub = tm // TILE
    return pl.pallas_call(
        _ffn_kernel,
        out_shape=jax.ShapeDtypeStruct((r, d), F32),
        grid=(r // tm, ff // tf),
        in_specs=[pl.BlockSpec((tm, d), lambda i, f: (i, 0)),
                  pl.BlockSpec((sub, MOD_ROWS, d), lambda i, f: (i, 0, 0)),
                  pl.BlockSpec((d, tf), lambda i, f: (0, f)),
                  pl.BlockSpec((d, tf), lambda i, f: (0, f)),
                  pl.BlockSpec((tf, d), lambda i, f: (f, 0))],
        out_specs=pl.BlockSpec((tm, d), lambda i, f: (i, 0)),
        scratch_shapes=[pltpu.VMEM((tm, d), BF16), pltpu.VMEM((tm, d), F32)],
        compiler_params=_cparams(("arbitrary", "arbitrary")),
        name="ffn_swiglu",
    )(h, mod, w_gate.astype(BF16), w_up.astype(BF16), w_down.astype(BF16))


MOE_TILE = 2048
MOE_CHUNK = 128


def _router_kernel(tiles_per_seq, route_ctx, h_ref, mod_ref, r_ref, u_ref, cw_ref, slot_ref, slott_ref, cnt_ref):
    nsub = h_ref.shape[0] // TILE
    lane = lax.broadcasted_iota(jnp.int32, (TILE, LANES), 1).astype(F32)
    earlier = jnp.where(lax.broadcasted_iota(jnp.int32, (TILE, TILE), 1)
                        < lax.broadcasted_iota(jnp.int32, (TILE, TILE), 0), 1.0, 0.0).astype(BF16)
    neg = jnp.float32(-jnp.inf)
    count = jnp.zeros((1, LANES), F32)
    for j in range(nsub):
        rows = slice(j * TILE, (j + 1) * TILE)
        mod = mod_ref[j]
        u = _normmod(h_ref[rows, :], mod[GS_F:GS_F + 1], mod[SH_F:SH_F + 1])
        u_ref[rows, :] = u.astype(BF16)
        logits = _dot3(u, r_ref[...])
        logits = jnp.where(lane < N_EXPERTS, logits, neg)
        m1 = jnp.max(logits, axis=-1, keepdims=True)
        i1 = jnp.min(jnp.where(logits == m1, lane, float(LANES)), axis=-1, keepdims=True)
        rest = jnp.where(lane == i1, neg, logits)
        m2 = jnp.max(rest, axis=-1, keepdims=True)
        i2 = jnp.min(jnp.where(rest == m2, lane, float(LANES)), axis=-1, keepdims=True)
        e2 = jnp.exp(m2 - m1)
        w1 = 1.0 / (1.0 + e2)
        w2 = e2 / (1.0 + e2)
        cw = jnp.where(lane == i1, w1, jnp.where(lane == i2, w2, 0.0))
        sel = jnp.where(lane == i1, 1.0, jnp.where(lane == i2, 1.0, 0.0))
        if not route_ctx:
            is_ctx = (pl.program_id(0) * nsub + j) % tiles_per_seq == tiles_per_seq - 1
            cw = jnp.where(is_ctx, 0.0, cw)
            sel = jnp.where(is_ctx, 0.0, sel)
        cw_ref[rows, :] = cw
        slot = jnp.where(sel > 0.0, _dot(earlier, sel.astype(BF16)) + count, -1.0)
        slot_ref[rows, :] = slot
        slott_ref[:, rows] = slot.T[:N_EXPERTS, :]
        count = count + jnp.sum(sel, axis=0, keepdims=True)
    cnt_ref[0] = jnp.broadcast_to(count, (N_EXPERTS, LANES))


def _router(h, mod, router, tiles_per_seq, route_ctx):
    r, d = h.shape
    rp = jnp.zeros((d, LANES), F32).at[:, :N_EXPERTS].set(router)
    sub = MOE_TILE // TILE
    row = lambda w: pl.BlockSpec((MOE_TILE, w), lambda i: (i, 0))
    return pl.pallas_call(
        functools.partial(_router_kernel, tiles_per_seq, route_ctx),
        out_shape=(jax.ShapeDtypeStruct((r, d), BF16), jax.ShapeDtypeStruct((r, LANES), F32),
                   jax.ShapeDtypeStruct((r, LANES), F32), jax.ShapeDtypeStruct((N_EXPERTS, r), F32),
                   jax.ShapeDtypeStruct((r // MOE_TILE, N_EXPERTS, LANES), F32)),
        grid=(r // MOE_TILE,),
        in_specs=[row(d),
                  pl.BlockSpec((sub, MOD_ROWS, d), lambda i: (i, 0, 0)),
                  pl.BlockSpec((d, LANES), lambda i: (0, 0))],
        out_specs=(row(d), row(LANES), row(LANES),
                   pl.BlockSpec((N_EXPERTS, MOE_TILE), lambda i: (0, i)),
                   pl.BlockSpec((1, N_EXPERTS, LANES), lambda i: (i, 0, 0))),
        compiler_params=_cparams(("arbitrary",)),
        name="moe_router",
    )(h, mod, rp)


def _moe_kernel(nch_ref, h_ref, u_ref, cw_ref, slot_ref, slott_ref, mod_ref, wg_ref, wu_ref, wd_ref, o_ref,
                xs_scr, y_scr):
    i = pl.program_id(0)
    e = pl.program_id(1)
    f = pl.program_id(2)
    last_f = pl.num_programs(2) - 1
    tile = u_ref.shape[0]
    nch = nch_ref[i * N_EXPERTS + e]
    block = 4 * MOE_CHUNK

    @pl.when((e == 0) & (f == 0))
    def _():
        o_ref[...] = jnp.zeros_like(o_ref)

    nblk = lax.shift_right_logical(nch, 2)
    tail2 = pl.multiple_of(nblk * block, MOE_CHUNK)
    tail1 = pl.multiple_of(tail2 + (nch & 2) * MOE_CHUNK, MOE_CHUNK)

    def run_block(r0, rn):
        rows = pl.ds(r0, rn)

        @pl.when(f == 0)
        def _():
            rid = (lax.broadcasted_iota(jnp.int32, (rn, 1), 0) + r0).astype(F32)
            pick = jnp.where(slott_ref[pl.ds(e, 1), :] == rid, 1.0, 0.0).astype(BF16)
            xs_scr[rows, :] = _dot(pick, u_ref[...]).astype(BF16)

        xs = xs_scr[rows, :]
        a = _silu(_dot(xs, wg_ref[0])) * _dot(xs, wu_ref[0])
        part = _dot(a.astype(BF16), wd_ref[0])

        @pl.when(f == 0)
        def _():
            y_scr[rows, :] = part

        @pl.when(f > 0)
        def _():
            y_scr[rows, :] += part

    def loop_blocks(fn):
        def body(b, carry):
            fn(pl.multiple_of(b * block, block), block)
            return carry
        lax.fori_loop(0, nblk, body, 0)
        pl.when((nch & 2) != 0)(lambda: fn(tail2, 2 * MOE_CHUNK))
        pl.when((nch & 1) != 0)(lambda: fn(tail1, MOE_CHUNK))

    loop_blocks(run_block)

    @pl.when((f == last_f) & (nch > 0))
    def _():
        mine = lax.broadcasted_iota(jnp.int32, (tile, LANES), 1) == e
        slot_col = jnp.sum(jnp.where(mine, slot_ref[...], 0.0), axis=-1, keepdims=True)
        cw_col = jnp.sum(jnp.where(mine, cw_ref[...], 0.0), axis=-1, keepdims=True)

        def scatter_block(r0, rn):
            cid = (lax.broadcasted_iota(jnp.int32, (1, rn), 1) + r0).astype(F32)
            put = jnp.where(slot_col == cid, 1.0, 0.0).astype(BF16)
            o_ref[...] += cw_col * _dot(put, y_scr[pl.ds(r0, rn), :].astype(BF16))

        loop_blocks(scatter_block)

    @pl.when((e == pl.num_programs(1) - 1) & (f == last_f))
    def _():
        for j in range(tile // TILE):
            rows = slice(j * TILE, (j + 1) * TILE)
            o_ref[rows, :] = h_ref[rows, :] + mod_ref[j, G_F:G_F + 1] * o_ref[rows, :]


def _moe(h, u, cw, slot, slott, counts, mod, w_gate, w_up, w_down, layer):
    r, d = h.shape
    _, ne, _, ff = w_gate.shape
    tm, tf = MOE_TILE, 512
    sub = tm // TILE
    nch = ((counts[:, 0, :ne].astype(jnp.int32) + (MOE_CHUNK - 1)) // MOE_CHUNK).reshape(-1)
    once = pl.Buffered(1)
    row = lambda w: pl.BlockSpec((tm, w), lambda i, e, f, n: (i, 0), pipeline_mode=once)
    return pl.pallas_call(
        _moe_kernel,
        out_shape=jax.ShapeDtypeStruct((r, d), F32),
        grid_spec=pltpu.PrefetchScalarGridSpec(
            num_scalar_prefetch=1,
            grid=(r // tm, ne, ff // tf),
            in_specs=[row(d), row(d), row(LANES), row(LANES),
                      pl.BlockSpec((ne, tm), lambda i, e, f, n: (0, i), pipeline_mode=once),
                      pl.BlockSpec((sub, MOD_ROWS, d), lambda i, e, f, n: (i, 0, 0)),
                      pl.BlockSpec((None, 1, d, tf), lambda i, e, f, n: (layer, e, 0, f)),
                      pl.BlockSpec((None, 1, d, tf), lambda i, e, f, n: (layer, e, 0, f)),
                      pl.BlockSpec((None, 1, tf, d), lambda i, e, f, n: (layer, e, f, 0))],
            out_specs=pl.BlockSpec((tm, d), lambda i, e, f, n: (i, 0), pipeline_mode=once),
            scratch_shapes=[pltpu.VMEM((tm, d), BF16), pltpu.VMEM((tm, d), F32)]),
        compiler_params=_cparams(("arbitrary", "arbitrary", "arbitrary")),
        name="moe_experts",
    )(nch, h, u, cw, slot, slott, mod, w_gate, w_up, w_down)


def _final_kernel(h_ref, g_ref, o_ref):
    h = h_ref[0]
    inv = lax.rsqrt(jnp.mean(h * h, axis=-1, keepdims=True) + EPS)
    o_ref[0] = h * inv * g_ref[...]


def _final_norm(h3, gain, seq):
    bsz, _, d = h3.shape
    return pl.pallas_call(
        _final_kernel,
        out_shape=jax.ShapeDtypeStruct((bsz, seq, d), F32),
        grid=(bsz, seq // ROW_BLOCK),
        in_specs=[pl.BlockSpec((1, ROW_BLOCK, d), lambda b, t: (b, t, 0)),
                  pl.BlockSpec((1, d), lambda b, t: (0, 0))],
        out_specs=pl.BlockSpec((1, ROW_BLOCK, d), lambda b, t: (b, t, 0)),
        compiler_params=_cparams(("arbitrary", "arbitrary")),
        name="final_norm",
    )(h3, gain.reshape(1, d))


def _tile_mods(mods, gain_mix, gain_ffn, bsz, lat_tiles):
    sh_m, sc_m, g_m, sh_f, sc_f, g_f = (mods[:, i] for i in range(N_MOD))
    rows = jnp.stack([gain_mix * (1.0 + sc_m), sh_m, g_m, gain_ffn * (1.0 + sc_f), sh_f, g_f,
                      jnp.zeros_like(g_f), jnp.zeros_like(g_f)], axis=1)
    lat = jnp.broadcast_to(rows[:bsz, None], (bsz, lat_tiles) + rows.shape[1:])
    ctx = jnp.broadcast_to(rows[bsz:, None], (bsz, 1) + rows.shape[1:])
    return jnp.concatenate([lat, ctx], axis=1).reshape(bsz * (lat_tiles + 1), MOD_ROWS, rows.shape[-1])


def kernel(x, c, ctx, c_ctx, ada_w, ada_b, norm_mix, norm_ffn, norm_final, s5_lambda_re, s5_lambda_im, s5_log_step, s5_b_re, s5_b_im, s5_c_re, s5_c_im, s5_d, s5_glu_w, s5_glu_b, pool_w, pool_b, pool_scale, fnet_w, fnet_b, ffn_w_gate, ffn_w_up, ffn_w_down, moe_router, moe_w_gate, moe_w_up, moe_w_down):
    bsz, seq, d = x.shape
    ctx_len = ctx.shape[1]
    depth = ada_w.shape[0]
    assert ctx_len == TILE and seq % ROW_BLOCK == 0 and d % LANES == 0
    tot = seq + ctx_len
    lat_tiles = seq // TILE
    tps = lat_tiles + 1
    r = bsz * tot
    assert r % ROW_BLOCK == 0

    cond = jnp.zeros((16, d), F32).at[:bsz].set(c).at[bsz].set(c_ctx)
    mods_all = _ada_all(cond, ada_w, ada_b)[:, :bsz + 1].reshape(depth, bsz + 1, N_MOD, d)

    moe_wg, moe_wu, moe_wd = (w.astype(BF16) for w in (moe_w_gate, moe_w_up, moe_w_down))
    mods_tiled = jax.vmap(lambda m, gm, gf: _tile_mods(m, gm, gf, bsz, lat_tiles))(mods_all, norm_mix, norm_ffn)
    s5_tables = jax.vmap(_s5_params)(s5_lambda_re, s5_lambda_im, s5_log_step, s5_b_re, s5_b_im, s5_c_re, s5_c_im)
    h = None
    for i in range(depth):
        kind, j = i % 3, i // 3
        mod = mods_tiled[i]
        if kind == 0:
            src = (x, ctx) if h is None else (h,)
            u = _s5_pre(src, mod, tps)
            y_lat, y_ctx = _s5_scan(u.reshape(bsz, tot, d), *s5_tables, j, seq, ctx_len)
            h = _s5_glu(src, y_lat, y_ctx, mod, s5_d[j], s5_glu_w[j], s5_glu_b[j], seq)
        elif kind == 1:
            h = _pool(h, mod, pool_w[j], pool_b[j], pool_scale[j], seq)
        else:
            g = _fnet_fold(fnet_w[j])
            v3 = _fnet_proj(h, mod, g).reshape(bsz, tot, 2 * d)
            h3 = h.reshape(bsz, tot, d)
            mod4 = mod.reshape(bsz, tps, MOD_ROWS, d)
            part = _fnet_lat(_dft_matrix(seq), v3, h3, mod4, fnet_b[j], seq)
            cc, sc = _dft_tables(ctx_len)
            dftc = jnp.asarray(np.concatenate([cc, sc], axis=1), BF16)
            h = _fnet_ctx(dftc, v3, h3, mod4, fnet_b[j], part, seq).reshape(r, d)
        kk = i // 2
        if i % 2 == 0:
            h = _ffn(h, mod, ffn_w_gate[kk], ffn_w_up[kk], ffn_w_down[kk])
        else:
            ub, cw, slot, slott, counts = _router(h, mod, moe_router[kk], tps, route_ctx=i < depth - 1)
            h = _moe(h, ub, cw, slot, slott, counts, mod, moe_wg, moe_wu, moe_wd, kk)
    return _final_norm(h.reshape(bsz, tot, d), norm_final, seq)
```

```python
import functools
import math

import numpy as np
import jax
import jax.numpy as jnp
from jax import lax
from jax.experimental import pallas as pl
from jax.experimental.pallas import tpu as pltpu

F32 = jnp.float32
BF16 = jnp.bfloat16

EPS = 1e-6
GRID_W = 64
N_MOD = 6
S5_GROUP = 16
S5_STATE = 64
S5_CHUNK = 16
POOL_WINDOWS = (2, 4, 8, 16)
FNET_GROUPS = 4
N_EXPERTS = 8

LANES = 128
SUBLANES = 8
TILE = 256
ROW_BLOCK = 1024
VMEM_LIMIT = 56 * 1024 * 1024


def _cparams(sem, vmem=VMEM_LIMIT):
    return pltpu.CompilerParams(dimension_semantics=sem, vmem_limit_bytes=vmem)


def _dot(a, b):
    return jnp.dot(a, b, preferred_element_type=F32)


def _split(a):
    hi = a.astype(BF16)
    lo = (a - hi.astype(F32)).astype(BF16)
    return hi, lo


def _dot3(a, b):
    a_hi, a_lo = _split(a)
    b_hi, b_lo = _split(b)
    return _dot(a_hi, b_hi) + (_dot(a_lo, b_hi) + _dot(a_hi, b_lo))


def _normmod(h, gs, sh):
    inv = lax.rsqrt(jnp.mean(h * h, axis=-1, keepdims=True) + EPS)
    return h * inv * gs + sh


GS_M, SH_M, G_M, GS_F, SH_F, G_F = range(6)
MOD_ROWS = 8


def _ada_kernel(c_ref, w_ref, b_ref, o_ref):
    c = c_ref[...]
    s = c * (1.0 / (1.0 + jnp.exp(-c)))
    o_ref[0] = _dot3(s, w_ref[0]) + b_ref[0]


def _ada_all(cond, ada_w, ada_b):
    depth, d, n = ada_w.shape
    rows = cond.shape[0]
    tn = 1536
    return pl.pallas_call(
        _ada_kernel,
        out_shape=jax.ShapeDtypeStruct((depth, rows, n), F32),
        grid=(depth, n // tn),
        in_specs=[
            pl.BlockSpec((rows, d), lambda i, j: (0, 0)),
            pl.BlockSpec((1, d, tn), lambda i, j: (i, 0, j)),
            pl.BlockSpec((1, 1, tn), lambda i, j: (i, 0, j)),
        ],
        out_specs=pl.BlockSpec((1, rows, tn), lambda i, j: (i, 0, j)),
        compiler_params=_cparams(("arbitrary", "arbitrary")),
        name="ada_params",
    )(cond, ada_w, ada_b.reshape(depth, 1, n))


def _stream_specs(src, tps, d):
    if len(src) == 1:
        return [pl.BlockSpec((TILE, d), lambda i: (i, 0))]
    return [pl.BlockSpec((None, TILE, d), lambda i: (i // tps, jnp.minimum(i % tps, tps - 2), 0)),
            pl.BlockSpec((None, TILE, d), lambda i: (i // tps, 0, 0))]


def _stream_tile(refs, tps):
    if len(refs) == 1:
        return refs[0][...]
    is_ctx = (pl.program_id(0) % tps) == tps - 1
    return jnp.where(is_ctx, refs[1][...], refs[0][...])


def _s5_pre_kernel(n_src, tps, *refs):
    mod_ref, u_ref = refs[n_src:]
    if n_src == 1:
        for j in range(u_ref.shape[0] // TILE):
            rows = slice(j * TILE, (j + 1) * TILE)
            mod = mod_ref[j]
            u_ref[rows, :] = _normmod(refs[0][rows, :], mod[GS_M:GS_M + 1], mod[SH_M:SH_M + 1])
    else:
        mod = mod_ref[0]
        u_ref[...] = _normmod(_stream_tile(refs[:n_src], tps), mod[GS_M:GS_M + 1], mod[SH_M:SH_M + 1])


def _s5_pre(src, mod, tps):
    r, _, d = mod.shape
    r *= TILE
    if len(src) == 1:
        rows = ROW_BLOCK
        src_specs = [pl.BlockSpec((rows, d), lambda i: (i, 0))]
    else:
        rows = TILE
        src_specs = _stream_specs(src, tps, d)
    return pl.pallas_call(
        functools.partial(_s5_pre_kernel, len(src), tps),
        out_shape=jax.ShapeDtypeStruct((r, d), F32),
        grid=(r // rows,),
        in_specs=src_specs + [pl.BlockSpec((rows // TILE, MOD_ROWS, d), lambda i: (i, 0, 0))],
        out_specs=pl.BlockSpec((rows, d), lambda i: (i, 0)),
        compiler_params=_cparams(("arbitrary",)),
        name="s5_pre",
    )(*src, mod)


def _s5_params(lam_re, lam_im, log_step, b_re, b_im, c_re, c_im):
    hp = lax.Precision.HIGHEST
    L = S5_CHUNK
    n_dir, g, p = lam_re.shape
    gb = LANES // S5_GROUP
    nb = g // gb
    hch = S5_GROUP
    step = jnp.exp(log_step)[..., None]
    ar, ai = lam_re * step, lam_im * step
    k = jnp.arange(L + 1, dtype=F32)[None, :, None, None]
    mag = jnp.exp(ar[:, None] * k)
    ang = ai[:, None] * k
    pr, pi = mag * jnp.cos(ang), mag * jnp.sin(ang)
    lbr, lbi = pr[:, 1] - 1.0, pi[:, 1]
    den = lam_re * lam_re + lam_im * lam_im
    qr = (lbr * lam_re + lbi * lam_im) / den
    qi = (lbi * lam_re - lbr * lam_im) / den
    bbr = qr[..., None] * b_re - qi[..., None] * b_im
    bbi = qr[..., None] * b_im + qi[..., None] * b_re
    idx = np.arange(L * hch)
    spread_o = jnp.asarray((idx[None, :] % hch) == np.arange(hch)[:, None], F32)
    spread_k = jnp.asarray((idx[None, :] // hch) == np.arange(L)[:, None], F32)
    spread = lambda x, m: jnp.einsum('...a,an->...n', x, m, precision=hp)
    pr_t, pi_t = jnp.moveaxis(pr, 1, 3), jnp.moveaxis(pi, 1, 3)
    cr = spread(jnp.swapaxes(c_re, 2, 3), spread_o)
    ci = spread(jnp.swapaxes(c_im, 2, 3), spread_o)

    def c_lam(k0):
        pk_r = spread(pr_t[..., k0:k0 + L], spread_k)
        pk_i = spread(pi_t[..., k0:k0 + L], spread_k)
        return cr * pk_r - ci * pk_i, cr * pk_i + ci * pk_r

    bbr_t, bbi_t = jnp.swapaxes(bbr, 2, 3), jnp.swapaxes(bbi, 2, 3)
    cl0r, cl0i = c_lam(0)
    kmat = (jnp.einsum('dgip,dgpn->dgin', bbr_t, cl0r, precision=hp)
            - jnp.einsum('dgip,dgpn->dgin', bbi_t, cl0i, precision=hp))
    kc = kmat.reshape(n_dir, nb, gb * hch, L * hch)
    dup = lambda x: jnp.concatenate([x, x], axis=-1)
    pw_r, pw_i = dup(pr[:, L - 1::-1]), dup(pi[:, L - 1::-1])
    bb = jnp.concatenate([bbr_t, bbi_t], axis=-1)
    bb_rot = jnp.concatenate([-bbi_t, bbr_t], axis=-1)
    e = pw_r[:, :, :, None, :] * bb[:, None] + pw_i[:, :, :, None, :] * bb_rot[:, None]
    ec = jnp.moveaxis(e.reshape(n_dir, L, nb, gb * hch, 2 * p), 1, 2)
    cl1r, cl1i = c_lam(1)
    cl = jnp.stack([cl1r, -cl1i], axis=1).reshape(n_dir, 2, nb, gb * p, L * hch)
    cl = jnp.moveaxis(cl, 1, 2).reshape(n_dir, nb, 2 * gb * p, L * hch)
    a = jnp.stack([pr[:, L].reshape(n_dir, nb, 1, gb * p), pi[:, L].reshape(n_dir, nb, 1, gb * p)], axis=3)
    return kc.astype(BF16), ec.astype(BF16), cl.astype(BF16), a.reshape(n_dir, nb, 1, 2 * gb * p)


def _replicators():
    gb = LANES // S5_GROUP
    r16 = np.zeros((S5_CHUNK, S5_GROUP, S5_CHUNK, gb, S5_GROUP), np.float32)
    for k in range(S5_CHUNK):
        for o in range(S5_GROUP):
            r16[k, o, k, :, o] = 1.0
    r64 = np.zeros((2, S5_STATE, 2, gb, S5_STATE), np.float32)
    for part in range(2):
        for q in range(S5_STATE):
            r64[part, q, part, :, q] = 1.0
    return (jnp.asarray(r16.reshape(S5_CHUNK * S5_GROUP, S5_CHUNK * LANES), BF16),
            jnp.asarray(r64.reshape(2 * S5_STATE, 2 * gb * S5_STATE), BF16))


def _s5_scan_kernel(ul_ref, uc_ref, kc_ref, ec_ref, cl_ref, r16_ref, r64_ref, a_ref, yl_ref, yc_ref,
                    h_scr, s_scr, hs_scr, wts_scr, wy_scr):
    d = pl.program_id(1)
    t = pl.program_id(2)
    L = S5_CHUNK
    nstate = a_ref.shape[-1]
    half = nstate // 2
    nt = L * LANES
    blk = lambda i: slice(i * LANES, (i + 1) * LANES)

    def build_weights():
        row_gi = lax.broadcasted_iota(jnp.int32, (LANES, LANES), 0) // S5_GROUP
        lane_go = lax.broadcasted_iota(jnp.int32, (LANES, LANES), 1) // S5_GROUP
        toep = _dot(kc_ref[0, 0], r16_ref[...])
        zero = jnp.zeros((LANES, LANES), BF16)
        for lag in range(L):
            piece = jnp.where(row_gi == lane_go, toep[:, blk(lag)], 0.0).astype(BF16)
            for s in range(L - lag):
                wts_scr[blk(s), blk(s + lag)] = piece
        for tt in range(0, L, 2):
            wts_scr[blk(tt + 1), blk(tt)] = zero
        row_gi_w = lax.broadcasted_iota(jnp.int32, (LANES, nstate), 0) // S5_GROUP
        lane_gp = (lax.broadcasted_iota(jnp.int32, (LANES, nstate), 1) % half) // S5_STATE
        for s in range(L):
            e = _dot(ec_ref[0, 0, s], r64_ref[...])
            wts_scr[blk(s), nt:] = jnp.where(row_gi_w == lane_gp, e, 0.0).astype(BF16)
        row_gp = (lax.broadcasted_iota(jnp.int32, (nstate, LANES), 0) % half) // S5_STATE
        lane_go_t = lax.broadcasted_iota(jnp.int32, (nstate, LANES), 1) // S5_GROUP
        for tt in range(L):
            w = _dot(cl_ref[0, 0], r16_ref[:, blk(tt)])
            wy_scr[:, blk(tt)] = jnp.where(row_gp == lane_go_t, w, 0.0).astype(BF16)

    def process(x_ref, y_ref):
        bsz, tok, _ = x_ref.shape
        nc = tok // L
        m = bsz * nc
        flip = lambda i, n: i + d * (n - 1 - 2 * i)
        u = jnp.concatenate(
            [x_ref[:, pl.ds(flip(s, L), nc, stride=L), :].reshape(m, LANES) for s in range(L)], axis=1).astype(BF16)
        s_all = _dot(u, wts_scr[:, nt:])
        nq = nstate // LANES
        nh = nq // 2
        ncp = nc + SUBLANES
        for q in range(nq):
            for b in range(bsz):
                s_scr[q, b * ncp:b * ncp + nc, :] = s_all[b * nc:(b + 1) * nc, blk(q)]
        a = a_ref[0, 0]
        a_re = [jnp.broadcast_to(a[:, blk(q)], (bsz, LANES)) for q in range(nh)]
        a_im = [jnp.broadcast_to(a[:, blk(nh + q)], (bsz, LANES)) for q in range(nh)]
        h_re = [h_scr[q] for q in range(nh)]
        h_im = [h_scr[nh + q] for q in range(nh)]
        for c in range(nc):
            rows = pl.ds(flip(c, nc), bsz, stride=ncp)
            for q in range(nh):
                hs_scr.at[q][rows, :] = h_re[q]
                hs_scr.at[nh + q][rows, :] = h_im[q]
                s_re = s_scr.at[q][rows, :]
                s_im = s_scr.at[nh + q][rows, :]
                h_re[q], h_im[q] = (a_re[q] * h_re[q] - a_im[q] * h_im[q] + s_re,
                                    a_re[q] * h_im[q] + a_im[q] * h_re[q] + s_im)
        for q in range(nh):
            h_scr[q] = h_re[q]
            h_scr[nh + q] = h_im[q]
        hs = jnp.concatenate(
            [jnp.concatenate([hs_scr[q, b * ncp:b * ncp + nc, :] for b in range(bsz)], axis=0) for q in range(nq)],
            axis=1).astype(BF16)
        for n in range(L // 2):
            cols = slice(2 * n * LANES, 2 * (n + 1) * LANES)
            y = _dot(u[:, :cols.stop], wts_scr[:cols.stop, cols]) + _dot(hs, wy_scr[:, cols])
            for tt in (2 * n, 2 * n + 1):
                y_ref[:, pl.ds(flip(tt, L), nc, stride=L), :] = (
                    y[:, blk(tt - 2 * n)].reshape(bsz, nc, LANES))

    @pl.when(t == 0)
    def _():
        build_weights()
        h_scr[...] = jnp.zeros_like(h_scr)
        process(uc_ref, yc_ref)

    @pl.when(t > 0)
    def _():
        process(ul_ref, yl_ref)


def _s5_scan(u3, kc, ec, cl, a, layer, seq, ctx_len):
    bsz, _, d = u3.shape
    tl = 1024
    ntl = seq // tl
    nb = d // LANES
    n_dir = 2
    ctx_blk = seq // ctx_len
    nstate = a.shape[-1]
    nt = S5_CHUNK * LANES
    r16, r64 = _replicators()

    def lat_idx(dd, t):
        i = jnp.maximum(t - 1, 0)
        return jnp.where(dd == 0, i, ntl - 1 - i)

    per_block = lambda arr: pl.BlockSpec((None, 1, 1) + arr.shape[3:],
                                         lambda j, dd, t: (layer, dd, j) + (0,) * (arr.ndim - 3))
    const = lambda arr: pl.BlockSpec(arr.shape, lambda j, dd, t: (0,) * arr.ndim)
    return pl.pallas_call(
        _s5_scan_kernel,
        out_shape=(jax.ShapeDtypeStruct((n_dir, bsz, seq, d), F32),
                   jax.ShapeDtypeStruct((n_dir, bsz, ctx_len, d), F32)),
        grid=(nb, n_dir, ntl + 1),
        in_specs=[
            pl.BlockSpec((bsz, tl, LANES), lambda j, dd, t: (0, lat_idx(dd, t), j)),
            pl.BlockSpec((bsz, ctx_len, LANES), lambda j, dd, t: (0, ctx_blk, j)),
            per_block(kc), per_block(ec), per_block(cl), const(r16), const(r64), per_block(a),
        ],
        out_specs=(
            pl.BlockSpec((None, bsz, tl, LANES), lambda j, dd, t: (dd, 0, lat_idx(dd, t), j)),
            pl.BlockSpec((None, bsz, ctx_len, LANES), lambda j, dd, t: (dd, 0, 0, j)),
        ),
        scratch_shapes=[
            pltpu.VMEM((nstate // LANES, bsz, LANES), F32),
            pltpu.VMEM((nstate // LANES, bsz * (tl // S5_CHUNK + SUBLANES), LANES), F32),
            pltpu.VMEM((nstate // LANES, bsz * (tl // S5_CHUNK + SUBLANES), LANES), F32),
            pltpu.VMEM((nt, nt + nstate), BF16),
            pltpu.VMEM((nstate, nt), BF16),
        ],
        compiler_params=_cparams(("arbitrary", "arbitrary", "arbitrary")),
        name="s5_scan",
    )(u3, u3, kc, ec, cl, r16, r64, a)


def _gelu_tanh(x):
    return 0.5 * x * (1.0 + jnp.tanh(math.sqrt(2.0 / math.pi) * (x + 0.044715 * (x * x * x))))


def _s5_glu_kernel(n_src, tiles_per_seq, *refs):
    ylf_ref, ylr_ref, ycf_ref, ycr_ref, mod_ref, dvec_ref, w_ref, b_ref, o_ref = refs[n_src:]
    is_ctx = (pl.program_id(0) % tiles_per_seq) == tiles_per_seq - 1
    y_scan = jnp.where(is_ctx, ycf_ref[0, 0] + ycr_ref[0, 0], ylf_ref[0, 0] + ylr_ref[0, 0])
    h = _stream_tile(refs[:n_src], tiles_per_seq)
    mod = mod_ref[0]
    y = dvec_ref[...] * _normmod(h, mod[GS_M:GS_M + 1], mod[SH_M:SH_M + 1]) + y_scan
    z = _gelu_tanh(y)
    gate = 1.0 / (1.0 + jnp.exp(-(_dot(z.astype(BF16), w_ref[...]) + b_ref[...])))
    o_ref[...] = h + mod[G_M:G_M + 1] * (z * gate)


def _s5_glu(src, y_lat, y_ctx, mod, dvec, glu_w, glu_b, seq):
    d = mod.shape[-1]
    r = mod.shape[0] * TILE
    lat_tiles = seq // TILE
    tps = lat_tiles + 1

    def lat_map(dd):
        return lambda i: (dd, i // tps, jnp.minimum(i % tps, lat_tiles - 1), 0)

    def ctx_map(dd):
        return lambda i: (dd, i // tps, 0, 0)

    row = pl.BlockSpec((TILE, d), lambda i: (i, 0))
    vec = pl.BlockSpec((1, d), lambda i: (0, 0))
    return pl.pallas_call(
        functools.partial(_s5_glu_kernel, len(src), tps),
        out_shape=jax.ShapeDtypeStruct((r, d), F32),
        grid=(r // TILE,),
        in_specs=_stream_specs(src, tps, d) + [
                  pl.BlockSpec((1, 1, TILE, d), lat_map(0)), pl.BlockSpec((1, 1, TILE, d), lat_map(1)),
                  pl.BlockSpec((1, 1, TILE, d), ctx_map(0)), pl.BlockSpec((1, 1, TILE, d), ctx_map(1)),
                  pl.BlockSpec((1, MOD_ROWS, d), lambda i: (i, 0, 0)),
                  vec, pl.BlockSpec((d, d), lambda i: (0, 0)), vec],
        out_specs=row,
        compiler_params=_cparams(("arbitrary",)),
        name="s5_glu",
    )(*src, y_lat, y_lat, y_ctx, y_ctx, mod, dvec.reshape(1, d), glu_w.astype(BF16), glu_b.reshape(1, d))


def _pool_matrices(width, tile):
    mats = np.zeros((len(POOL_WINDOWS), tile, tile), np.float32)
    pos = np.arange(width)
    for g, win in enumerate(POOL_WINDOWS):
        lo = np.clip(pos - win // 2, 0, width)
        hi = np.clip(pos + win // 2, 0, width)
        for base in range(0, tile, width):
            for j in range(width):
                mats[g, base + j, base + lo[j]:base + hi[j]] = 1.0 / float(hi[j] - lo[j])
    return mats


def _pool_kernel(tps, h_ref, mod_ref, a_ref, w_ref, b_ref, sc_ref, o_ref):
    gw = w_ref.shape[-1]
    nsub = h_ref.shape[0] // TILE
    for j in range(nsub):
        rows = slice(j * TILE, (j + 1) * TILE)
        mod = mod_ref[j]
        is_ctx = (pl.program_id(0) * nsub + j) % tps == tps - 1
        h = h_ref[rows, :]
        u = _normmod(h, mod[GS_M:GS_M + 1], mod[SH_M:SH_M + 1])
        ub = u.astype(BF16)
        for g in range(w_ref.shape[0]):
            cols = slice(g * gw, (g + 1) * gw)
            avg = jnp.where(is_ctx, a_ref[1, g], a_ref[0, g])
            res = _dot(avg, ub[:, cols]) - u[:, cols]
            mixed = _dot(res.astype(BF16), w_ref[g]) + b_ref[:, cols]
            o_ref[rows, cols] = h[:, cols] + mod[G_M:G_M + 1, cols] * (mixed * sc_ref[:, cols])


def _pool(h, mod, pool_w, pool_b, pool_scale, seq):
    r, d = h.shape
    tps = seq // TILE + 1
    ng, gw, _ = pool_w.shape
    mats = jnp.asarray(np.stack([_pool_matrices(GRID_W, TILE), _pool_matrices(TILE, TILE)]), BF16)
    row = pl.BlockSpec((ROW_BLOCK, d), lambda i: (i, 0))
    vec = pl.BlockSpec((1, d), lambda i: (0, 0))
    return pl.pallas_call(
        functools.partial(_pool_kernel, tps),
        out_shape=jax.ShapeDtypeStruct((r, d), F32),
        grid=(r // ROW_BLOCK,),
        in_specs=[row, pl.BlockSpec((ROW_BLOCK // TILE, MOD_ROWS, d), lambda i: (i, 0, 0)),
                  pl.BlockSpec((2, ng, TILE, TILE), lambda i: (0, 0, 0, 0)),
                  pl.BlockSpec((ng, gw, gw), lambda i: (0, 0, 0)), vec, vec],
        out_specs=row,
        compiler_params=_cparams(("arbitrary",)),
        name="pool_mixer",
    )(h, mod, mats, pool_w.astype(BF16), pool_b.reshape(1, d), pool_scale.reshape(1, d))


def _dft_tables(n):
    idx = np.arange(n)
    ang = 2.0 * np.pi * ((idx[:, None] * idx[None, :]) % n) / n
    scale = 1.0 / math.sqrt(n)
    return (np.cos(ang) * scale).astype(np.float32), (np.sin(ang) * scale).astype(np.float32)


def _fnet_fold_kernel(cc_ref, sc_ref, w_ref, o_ref):
    w = w_ref[...]
    d = w.shape[-1]
    o_ref[:, :d] = _dot3(cc_ref[...], w).astype(o_ref.dtype)
    o_ref[:, d:] = (-_dot3(sc_ref[...], w)).astype(o_ref.dtype)


def _fnet_fold(fnet_w):
    d = fnet_w.shape[0]
    gw = d // FNET_GROUPS
    cc, sc = _dft_tables(gw)
    sq = pl.BlockSpec((gw, gw), lambda i: (0, 0))
    return pl.pallas_call(
        _fnet_fold_kernel,
        out_shape=jax.ShapeDtypeStruct((d, 2 * d), BF16),
        grid=(FNET_GROUPS,),
        in_specs=[sq, sq, pl.BlockSpec((gw, d), lambda i: (i, 0))],
        out_specs=pl.BlockSpec((gw, 2 * d), lambda i: (i, 0)),
        compiler_params=_cparams(("arbitrary",)),
        name="fnet_fold",
    )(jnp.asarray(cc), jnp.asarray(sc), fnet_w)


def _fnet_proj_kernel(h_ref, mod_ref, g_ref, v_ref):
    for j in range(h_ref.shape[0] // TILE):
        rows = slice(j * TILE, (j + 1) * TILE)
        mod = mod_ref[j]
        u = _normmod(h_ref[rows, :], mod[GS_M:GS_M + 1], mod[SH_M:SH_M + 1])
        v_ref[rows, :] = _dot(u.astype(BF16), g_ref[...]).astype(v_ref.dtype)


def _fnet_proj(h, mod, g):
    r, d = h.shape
    sub = ROW_BLOCK // TILE
    return pl.pallas_call(
        _fnet_proj_kernel,
        out_shape=jax.ShapeDtypeStruct((r, 2 * d), BF16),
        grid=(r // ROW_BLOCK,),
        in_specs=[pl.BlockSpec((ROW_BLOCK, d), lambda i: (i, 0)),
                  pl.BlockSpec((sub, MOD_ROWS, d), lambda i: (i, 0, 0)),
                  pl.BlockSpec((d, 2 * d), lambda i: (0, 0))],
        out_specs=pl.BlockSpec((ROW_BLOCK, 2 * d), lambda i: (i, 0)),
        compiler_params=_cparams(("arbitrary",)),
        name="fnet_proj",
    )(h, mod, g)


def _dft_matrix(n):
    n1 = int(round(math.sqrt(n)))
    assert n1 * n1 == n
    hi = jnp.arange(n1, dtype=jnp.int32)[:, None]
    k = jnp.arange(n, dtype=jnp.int32)[None, :]
    ang_a = (2.0 * math.pi / n1) * ((hi * k) % n1).astype(F32)
    ang_b = (2.0 * math.pi / n) * ((hi * k) % n).astype(F32)
    ca, sa, cb, sb = jnp.cos(ang_a), jnp.sin(ang_a), jnp.cos(ang_b), jnp.sin(ang_b)
    scale = 1.0 / math.sqrt(n)
    c = (ca[:, None] * cb[None] - sa[:, None] * sb[None]).reshape(n, n) * scale
    s = (sa[:, None] * cb[None] + ca[:, None] * sb[None]).reshape(n, n) * scale
    return jnp.concatenate([c, s], axis=1).astype(BF16)


def _fnet_lat_kernel(f_ref, v_ref, h_ref, mod_ref, b_ref, o_ref, acc_ref):
    k = pl.program_id(2)

    @pl.when(k == 0)
    def _():
        acc_ref[...] = jnp.zeros_like(acc_ref)

    acc_ref[...] += _dot(f_ref[...], v_ref[0])

    @pl.when(k == pl.num_programs(2) - 1)
    def _():
        for j in range(h_ref.shape[1] // TILE):
            rows = slice(j * TILE, (j + 1) * TILE)
            o_ref[0, rows, :] = h_ref[0, rows, :] + mod_ref[0, j, G_M:G_M + 1] * (acc_ref[rows, :] + b_ref[...])


def _fnet_lat(dft, v3, h3, mod4, fnet_b, seq):
    bsz, tot, d = h3.shape
    tm, tk = ROW_BLOCK, min(2048, seq)
    kh = seq // tk
    sub = tm // TILE
    return pl.pallas_call(
        _fnet_lat_kernel,
        out_shape=jax.ShapeDtypeStruct((bsz, tot, d), F32),
        grid=(bsz, seq // tm, 2 * kh),
        in_specs=[pl.BlockSpec((tm, tk), lambda b, m, k: (m, k)),
                  pl.BlockSpec((1, tk, d), lambda b, m, k: (b, k % kh, k // kh)),
                  pl.BlockSpec((1, tm, d), lambda b, m, k: (b, m, 0)),
                  pl.BlockSpec((1, sub, MOD_ROWS, d), lambda b, m, k: (b, m, 0, 0)),
                  pl.BlockSpec((1, d), lambda b, m, k: (0, 0))],
        out_specs=pl.BlockSpec((1, tm, d), lambda b, m, k: (b, m, 0)),
        scratch_shapes=[pltpu.VMEM((tm, d), F32)],
        compiler_params=_cparams(("arbitrary", "arbitrary", "arbitrary")),
        name="fnet_dft_latent",
    )(dft, v3, h3, mod4, fnet_b.reshape(1, d))


def _fnet_ctx_kernel(f_ref, v_ref, h_ref, mod_ref, b_ref, prev_ref, o_ref):
    del prev_ref
    n = f_ref.shape[0]
    d = h_ref.shape[-1]
    acc = _dot(f_ref[:, :n], v_ref[0, :, :d]) + _dot(f_ref[:, n:], v_ref[0, :, d:])
    o_ref[0] = h_ref[0] + mod_ref[0, 0, G_M:G_M + 1] * (acc + b_ref[...])


def _fnet_ctx(dftc, v3, h3, mod4, fnet_b, partial, seq):
    bsz, tot, d = h3.shape
    n = tot - seq
    blk = seq // n
    return pl.pallas_call(
        _fnet_ctx_kernel,
        out_shape=jax.ShapeDtypeStruct((bsz, tot, d), F32),
        grid=(bsz,),
        in_specs=[pl.BlockSpec((n, 2 * n), lambda b: (0, 0)),
                  pl.BlockSpec((1, n, 2 * d), lambda b: (b, blk, 0)),
                  pl.BlockSpec((1, n, d), lambda b: (b, blk, 0)),
                  pl.BlockSpec((1, 1, MOD_ROWS, d), lambda b: (b, blk, 0, 0)),
                  pl.BlockSpec((1, d), lambda b: (0, 0)),
                  pl.BlockSpec(memory_space=pl.ANY)],
        out_specs=pl.BlockSpec((1, n, d), lambda b: (b, blk, 0)),
        input_output_aliases={5: 0},
        compiler_params=_cparams(("arbitrary",)),
        name="fnet_dft_context",
    )(dftc, v3, h3, mod4, fnet_b.reshape(1, d), partial)


def _silu(x):
    return x * (1.0 / (1.0 + jnp.exp(-x)))


def _ffn_kernel(h_ref, mod_ref, wg_ref, wu_ref, wd_ref, o_ref, u_scr, acc_ref):
    f = pl.program_id(1)
    sub = h_ref.shape[0] // TILE

    @pl.when(f == 0)
    def _():
        for j in range(sub):
            rows = slice(j * TILE, (j + 1) * TILE)
            mod = mod_ref[j]
            u_scr[rows, :] = _normmod(h_ref[rows, :], mod[GS_F:GS_F + 1], mod[SH_F:SH_F + 1]).astype(BF16)
        acc_ref[...] = jnp.zeros_like(acc_ref)

    u = u_scr[...]
    a = _silu(_dot(u, wg_ref[...])) * _dot(u, wu_ref[...])
    acc_ref[...] += _dot(a.astype(BF16), wd_ref[...])

    @pl.when(f == pl.num_programs(1) - 1)
    def _():
        for j in range(sub):
            rows = slice(j * TILE, (j + 1) * TILE)
            o_ref[rows, :] = h_ref[rows, :] + mod_ref[j, G_F:G_F + 1] * acc_ref[rows, :]


def _ffn(h, mod, w_gate, w_up, w_down):
    r, d = h.shape
    ff = w_gate.shape[1]
    tm, tf = ROW_BLOCK, 256
    sub = tm // TILE
    return pl.pallas_call(
        _ffn_kernel,
        out_shape=jax.ShapeDtypeStruct((r, d), F32),
        grid=(r // tm, ff // tf),
        in_specs=[pl.BlockSpec((tm, d), lambda i, f: (i, 0)),
                  pl.BlockSpec((sub, MOD_ROWS, d), lambda i, f: (i, 0, 0)),
                  pl.BlockSpec((d, tf), lambda i, f: (0, f)),
                  pl.BlockSpec((d, tf), lambda i, f: (0, f)),
                  pl.BlockSpec((tf, d), lambda i, f: (f, 0))],
        out_specs=pl.BlockSpec((tm, d), lambda i, f: (i, 0)),
        scratch_shapes=[pltpu.VMEM((tm, d), BF16), pltpu.VMEM((tm, d), F32)],
        compiler_params=_cparams(("arbitrary", "arbitrary")),
        name="ffn_swiglu",
    )(h, mod, w_gate.astype(BF16), w_up.astype(BF16), w_down.astype(BF16))


MOE_TILE = 2048
MOE_CHUNK = 128


def _router_kernel(tiles_per_seq, route_ctx, h_ref, mod_ref, r_ref, u_ref, cw_ref, slot_ref, slott_ref, cnt_ref):
    nsub = h_ref.shape[0] // TILE
    lane = lax.broadcasted_iota(jnp.int32, (TILE, LANES), 1).astype(F32)
    earlier = jnp.where(lax.broadcasted_iota(jnp.int32, (TILE, TILE), 1)
                        < lax.broadcasted_iota(jnp.int32, (TILE, TILE), 0), 1.0, 0.0).astype(BF16)
    neg = jnp.float32(-jnp.inf)
    count = jnp.zeros((1, LANES), F32)
    for j in range(nsub):
        rows = slice(j * TILE, (j + 1) * TILE)
        mod = mod_ref[j]
        u = _normmod(h_ref[rows, :], mod[GS_F:GS_F + 1], mod[SH_F:SH_F + 1])
        u_ref[rows, :] = u.astype(BF16)
        logits = _dot3(u, r_ref[...])
        logits = jnp.where(lane < N_EXPERTS, logits, neg)
        m1 = jnp.max(logits, axis=-1, keepdims=True)
        i1 = jnp.min(jnp.where(logits == m1, lane, float(LANES)), axis=-1, keepdims=True)
        rest = jnp.where(lane == i1, neg, logits)
        m2 = jnp.max(rest, axis=-1, keepdims=True)
        i2 = jnp.min(jnp.where(rest == m2, lane, float(LANES)), axis=-1, keepdims=True)
        e2 = jnp.exp(m2 - m1)
        w1 = 1.0 / (1.0 + e2)
        w2 = e2 / (1.0 + e2)
        cw = jnp.where(lane == i1, w1, jnp.where(lane == i2, w2, 0.0))
        sel = jnp.where(lane == i1, 1.0, jnp.where(lane == i2, 1.0, 0.0))
        if not route_ctx:
            is_ctx = (pl.program_id(0) * nsub + j) % tiles_per_seq == tiles_per_seq - 1
            cw = jnp.where(is_ctx, 0.0, cw)
            sel = jnp.where(is_ctx, 0.0, sel)
        cw_ref[rows, :] = cw
        slot = jnp.where(sel > 0.0, _dot(earlier, sel.astype(BF16)) + count, -1.0)
        slot_ref[rows, :] = slot
        slott_ref[:, rows] = slot.T[:N_EXPERTS, :]
        count = count + jnp.sum(sel, axis=0, keepdims=True)
    cnt_ref[0] = jnp.broadcast_to(count, (N_EXPERTS, LANES))


def _router(h, mod, router, tiles_per_seq, route_ctx):
    r, d = h.shape
    rp = jnp.zeros((d, LANES), F32).at[:, :N_EXPERTS].set(router)
    sub = MOE_TILE // TILE
    row = lambda w: pl.BlockSpec((MOE_TILE, w), lambda i: (i, 0))
    return pl.pallas_call(
        functools.partial(_router_kernel, tiles_per_seq, route_ctx),
        out_shape=(jax.ShapeDtypeStruct((r, d), BF16), jax.ShapeDtypeStruct((r, LANES), F32),
                   jax.ShapeDtypeStruct((r, LANES), F32), jax.ShapeDtypeStruct((N_EXPERTS, r), F32),
                   jax.ShapeDtypeStruct((r // MOE_TILE, N_EXPERTS, LANES), F32)),
        grid=(r // MOE_TILE,),
        in_specs=[row(d),
                  pl.BlockSpec((sub, MOD_ROWS, d), lambda i: (i, 0, 0)),
                  pl.BlockSpec((d, LANES), lambda i: (0, 0))],
        out_specs=(row(d), row(LANES), row(LANES),
                   pl.BlockSpec((N_EXPERTS, MOE_TILE), lambda i: (0, i)),
                   pl.BlockSpec((1, N_EXPERTS, LANES), lambda i: (i, 0, 0))),
        compiler_params=_cparams(("arbitrary",)),
        name="moe_router",
    )(h, mod, rp)


def _moe_kernel(nch_ref, h_ref, u_ref, cw_ref, slot_ref, slott_ref, mod_ref, wg_ref, wu_ref, wd_ref, o_ref,
                xs_scr, y_scr):
    i = pl.program_id(0)
    e = pl.program_id(1)
    f = pl.program_id(2)
    last_f = pl.num_programs(2) - 1
    tile = u_ref.shape[0]
    nch = nch_ref[i * N_EXPERTS + e]
    block = 4 * MOE_CHUNK

    @pl.when((e == 0) & (f == 0))
    def _():
        o_ref[...] = jnp.zeros_like(o_ref)

    nblk = lax.shift_right_logical(nch, 2)
    tail2 = pl.multiple_of(nblk * block, MOE_CHUNK)
    tail1 = pl.multiple_of(tail2 + (nch & 2) * MOE_CHUNK, MOE_CHUNK)

    def run_block(r0, rn):
        rows = pl.ds(r0, rn)

        @pl.when(f == 0)
        def _():
            rid = (lax.broadcasted_iota(jnp.int32, (rn, 1), 0) + r0).astype(F32)
            pick = jnp.where(slott_ref[pl.ds(e, 1), :] == rid, 1.0, 0.0).astype(BF16)
            xs_scr[rows, :] = _dot(pick, u_ref[...]).astype(BF16)

        xs = xs_scr[rows, :]
        a = _silu(_dot(xs, wg_ref[0])) * _dot(xs, wu_ref[0])
        part = _dot(a.astype(BF16), wd_ref[0])

        @pl.when(f == 0)
        def _():
            y_scr[rows, :] = part

        @pl.when(f > 0)
        def _():
            y_scr[rows, :] += part

    def loop_blocks(fn):
        def body(b, carry):
            fn(pl.multiple_of(b * block, block), block)
            return carry
        lax.fori_loop(0, nblk, body, 0)
        pl.when((nch & 2) != 0)(lambda: fn(tail2, 2 * MOE_CHUNK))
        pl.when((nch & 1) != 0)(lambda: fn(tail1, MOE_CHUNK))

    loop_blocks(run_block)

    @pl.when((f == last_f) & (nch > 0))
    def _():
        mine = lax.broadcasted_iota(jnp.int32, (tile, LANES), 1) == e
        slot_col = jnp.sum(jnp.where(mine, slot_ref[...], 0.0), axis=-1, keepdims=True)
        cw_col = jnp.sum(jnp.where(mine, cw_ref[...], 0.0), axis=-1, keepdims=True)

        def scatter_block(r0, rn):
            cid = (lax.broadcasted_iota(jnp.int32, (1, rn), 1) + r0).astype(F32)
            put = jnp.where(slot_col == cid, 1.0, 0.0).astype(BF16)
            o_ref[...] += cw_col * _dot(put, y_scr[pl.ds(r0, rn), :].astype(BF16))

        loop_blocks(scatter_block)

    @pl.when((e == pl.num_programs(1) - 1) & (f == last_f))
    def _():
        for j in range(tile // TILE):
            rows = slice(j * TILE, (j + 1) * TILE)
            o_ref[rows, :] = h_ref[rows, :] + mod_ref[j, G_F:G_F + 1] * o_ref[rows, :]


def _moe(h, u, cw, slot, slott, counts, mod, w_gate, w_up, w_down, layer):
    r, d = h.shape
    _, ne, _, ff = w_gate.shape
    tm, tf = MOE_TILE, 512
    sub = tm // TILE
    nch = ((counts[:, 0, :ne].astype(jnp.int32) + (MOE_CHUNK - 1)) // MOE_CHUNK).reshape(-1)
    once = pl.Buffered(1)
    row = lambda w: pl.BlockSpec((tm, w), lambda i, e, f, n: (i, 0), pipeline_mode=once)
    return pl.pallas_call(
        _moe_kernel,
        out_shape=jax.ShapeDtypeStruct((r, d), F32),
        grid_spec=pltpu.PrefetchScalarGridSpec(
            num_scalar_prefetch=1,
            grid=(r // tm, ne, ff // tf),
            in_specs=[row(d), row(d), row(LANES), row(LANES),
                      pl.BlockSpec((ne, tm), lambda i, e, f, n: (0, i), pipeline_mode=once),
                      pl.BlockSpec((sub, MOD_ROWS, d), lambda i, e, f, n: (i, 0, 0)),
                      pl.BlockSpec((None, 1, d, tf), lambda i, e, f, n: (layer, e, 0, f)),
                      pl.BlockSpec((None, 1, d, tf), lambda i, e, f, n: (layer, e, 0, f)),
                      pl.BlockSpec((None, 1, tf, d), lambda i, e, f, n: (layer, e, f, 0))],
            out_specs=pl.BlockSpec((tm, d), lambda i, e, f, n: (i, 0), pipeline_mode=once),
            scratch_shapes=[pltpu.VMEM((tm, d), BF16), pltpu.VMEM((tm, d), F32)]),
        compiler_params=_cparams(("arbitrary", "arbitrary", "arbitrary")),
        name="moe_experts",
    )(nch, h, u, cw, slot, slott, mod, w_gate, w_up, w_down)


def _final_kernel(h_ref, g_ref, o_ref):
    h = h_ref[0]
    inv = lax.rsqrt(jnp.mean(h * h, axis=-1, keepdims=True) + EPS)
    o_ref[0] = h * inv * g_ref[...]


def _final_norm(h3, gain, seq):
    bsz, _, d = h3.shape
    return pl.pallas_call(
        _final_kernel,
        out_shape=jax.ShapeDtypeStruct((bsz, seq, d), F32),
        grid=(bsz, seq // ROW_BLOCK),
        in_specs=[pl.BlockSpec((1, ROW_BLOCK, d), lambda b, t: (b, t, 0)),
                  pl.BlockSpec((1, d), lambda b, t: (0, 0))],
        out_specs=pl.BlockSpec((1, ROW_BLOCK, d), lambda b, t: (b, t, 0)),
        compiler_params=_cparams(("arbitrary", "arbitrary")),
        name="final_norm",
    )(h3, gain.reshape(1, d))


def _tile_mods(mods, gain_mix, gain_ffn, bsz, lat_tiles):
    sh_m, sc_m, g_m, sh_f, sc_f, g_f = (mods[:, i] for i in range(N_MOD))
    rows = jnp.stack([gain_mix * (1.0 + sc_m), sh_m, g_m, gain_ffn * (1.0 + sc_f), sh_f, g_f,
                      jnp.zeros_like(g_f), jnp.zeros_like(g_f)], axis=1)
    lat = jnp.broadcast_to(rows[:bsz, None], (bsz, lat_tiles) + rows.shape[1:])
    ctx = jnp.broadcast_to(rows[bsz:, None], (bsz, 1) + rows.shape[1:])
    return jnp.concatenate([lat, ctx], axis=1).reshape(bsz * (lat_tiles + 1), MOD_ROWS, rows.shape[-1])


def kernel(x, c, ctx, c_ctx, ada_w, ada_b, norm_mix, norm_ffn, norm_final, s5_lambda_re, s5_lambda_im, s5_log_step, s5_b_re, s5_b_im, s5_c_re, s5_c_im, s5_d, s5_glu_w, s5_glu_b, pool_w, pool_b, pool_scale, fnet_w, fnet_b, ffn_w_gate, ffn_w_up, ffn_w_down, moe_router, moe_w_gate, moe_w_up, moe_w_down):
    bsz, seq, d = x.shape
    ctx_len = ctx.shape[1]
    depth = ada_w.shape[0]
    assert ctx_len == TILE and seq % ROW_BLOCK == 0 and d % LANES == 0
    tot = seq + ctx_len
    lat_tiles = seq // TILE
    tps = lat_tiles + 1
    r = bsz * tot
    assert r % ROW_BLOCK == 0

    cond = jnp.zeros((16, d), F32).at[:bsz].set(c).at[bsz].set(c_ctx)
    mods_all = _ada_all(cond, ada_w, ada_b)[:, :bsz + 1].reshape(depth, bsz + 1, N_MOD, d)

    moe_wg, moe_wu, moe_wd = (w.astype(BF16) for w in (moe_w_gate, moe_w_up, moe_w_down))
    mods_tiled = jax.vmap(lambda m, gm, gf: _tile_mods(m, gm, gf, bsz, lat_tiles))(mods_all, norm_mix, norm_ffn)
    s5_tables = jax.vmap(_s5_params)(s5_lambda_re, s5_lambda_im, s5_log_step, s5_b_re, s5_b_im, s5_c_re, s5_c_im)
    h = None
    for i in range(depth):
        kind, j = i % 3, i // 3
        mod = mods_tiled[i]
        if kind == 0:
            src = (x, ctx) if h is None else (h,)
            u = _s5_pre(src, mod, tps)
            y_lat, y_ctx = _s5_scan(u.reshape(bsz, tot, d), *s5_tables, j, seq, ctx_len)
            h = _s5_glu(src, y_lat, y_ctx, mod, s5_d[j], s5_glu_w[j], s5_glu_b[j], seq)
        elif kind == 1:
            h = _pool(h, mod, pool_w[j], pool_b[j], pool_scale[j], seq)
        else:
            g = _fnet_fold(fnet_w[j])
            v3 = _fnet_proj(h, mod, g).reshape(bsz, tot, 2 * d)
            h3 = h.reshape(bsz, tot, d)
            mod4 = mod.reshape(bsz, tps, MOD_ROWS, d)
            part = _fnet_lat(_dft_matrix(seq), v3, h3, mod4, fnet_b[j], seq)
            cc, sc = _dft_tables(ctx_len)
            dftc = jnp.asarray(np.concatenate([cc, sc], axis=1), BF16)
            h = _fnet_ctx(dftc, v3, h3, mod4, fnet_b[j], part, seq).reshape(r, d)
        kk = i // 2
        if i % 2 == 0:
            h = _ffn(h, mod, ffn_w_gate[kk], ffn_w_up[kk], ffn_w_down[kk])
        else:
            ub, cw, slot, slott, counts = _router(h, mod, moe_router[kk], tps, route_ctx=i < depth - 1)
            h = _moe(h, ub, cw, slot, slott, counts, mod, moe_wg, moe_wu, moe_wd, kk)
    return _final_norm(h.reshape(bsz, tot, d), norm_final, seq)
```

```python
import functools
import math

import numpy as np
import jax
import jax.numpy as jnp
from jax import lax
from jax.experimental import pallas as pl
from jax.experimental.pallas import tpu as pltpu

F32 = jnp.float32
BF16 = jnp.bfloat16

EPS = 1e-6
GRID_W = 64
N_MOD = 6
S5_GROUP = 16
S5_STATE = 64
S5_CHUNK = 16
POOL_WINDOWS = (2, 4, 8, 16)
FNET_GROUPS = 4
N_EXPERTS = 8

LANES = 128
SUBLANES = 8
TILE = 256
ROW_BLOCK = 1024
VMEM_LIMIT = 56 * 1024 * 1024


def _cparams(sem, vmem=VMEM_LIMIT):
    return pltpu.CompilerParams(dimension_semantics=sem, vmem_limit_bytes=vmem)


def _dot(a, b):
    return jnp.dot(a, b, preferred_element_type=F32)


def _split(a):
    hi = a.astype(BF16)
    lo = (a - hi.astype(F32)).astype(BF16)
    return hi, lo


def _dot3(a, b):
    a_hi, a_lo = _split(a)
    b_hi, b_lo = _split(b)
    return _dot(a_hi, b_hi) + (_dot(a_lo, b_hi) + _dot(a_hi, b_lo))


def _normmod(h, gs, sh):
    inv = lax.rsqrt(jnp.mean(h * h, axis=-1, keepdims=True) + EPS)
    return h * inv * gs + sh


GS_M, SH_M, G_M, GS_F, SH_F, G_F = range(6)
MOD_ROWS = 8


def _ada_kernel(c_ref, w_ref, b_ref, o_ref):
    c = c_ref[...]
    s = c * (1.0 / (1.0 + jnp.exp(-c)))
    o_ref[0] = _dot3(s, w_ref[0]) + b_ref[0]


def _ada_all(cond, ada_w, ada_b):
    depth, d, n = ada_w.shape
    rows = cond.shape[0]
    tn = 1536
    return pl.pallas_call(
        _ada_kernel,
        out_shape=jax.ShapeDtypeStruct((depth, rows, n), F32),
        grid=(depth, n // tn),
        in_specs=[
            pl.BlockSpec((rows, d), lambda i, j: (0, 0)),
            pl.BlockSpec((1, d, tn), lambda i, j: (i, 0, j)),
            pl.BlockSpec((1, 1, tn), lambda i, j: (i, 0, j)),
        ],
        out_specs=pl.BlockSpec((1, rows, tn), lambda i, j: (i, 0, j)),
        compiler_params=_cparams(("arbitrary", "arbitrary")),
        name="ada_params",
    )(cond, ada_w, ada_b.reshape(depth, 1, n))


def _stream_specs(src, tps, d):
    if len(src) == 1:
        return [pl.BlockSpec((TILE, d), lambda i: (i, 0))]
    return [pl.BlockSpec((None, TILE, d), lambda i: (i // tps, jnp.minimum(i % tps, tps - 2), 0)),
            pl.BlockSpec((None, TILE, d), lambda i: (i // tps, 0, 0))]


def _stream_tile(refs, tps):
    if len(refs) == 1:
        return refs[0][...]
    is_ctx = (pl.program_id(0) % tps) == tps - 1
    return jnp.where(is_ctx, refs[1][...], refs[0][...])


def _s5_pre_kernel(n_src, tps, *refs):
    mod_ref, u_ref = refs[n_src:]
    if n_src == 1:
        for j in range(u_ref.shape[0] // TILE):
            rows = slice(j * TILE, (j + 1) * TILE)
            mod = mod_ref[j]
            u_ref[rows, :] = _normmod(refs[0][rows, :], mod[GS_M:GS_M + 1], mod[SH_M:SH_M + 1])
    else:
        mod = mod_ref[0]
        u_ref[...] = _normmod(_stream_tile(refs[:n_src], tps), mod[GS_M:GS_M + 1], mod[SH_M:SH_M + 1])


def _s5_pre(src, mod, tps):
    r, _, d = mod.shape
    r *= TILE
    if len(src) == 1:
        rows = ROW_BLOCK
        src_specs = [pl.BlockSpec((rows, d), lambda i: (i, 0))]
    else:
        rows = TILE
        src_specs = _stream_specs(src, tps, d)
    return pl.pallas_call(
        functools.partial(_s5_pre_kernel, len(src), tps),
        out_shape=jax.ShapeDtypeStruct((r, d), F32),
        grid=(r // rows,),
        in_specs=src_specs + [pl.BlockSpec((rows // TILE, MOD_ROWS, d), lambda i: (i, 0, 0))],
        out_specs=pl.BlockSpec((rows, d), lambda i: (i, 0)),
        compiler_params=_cparams(("arbitrary",)),
        name="s5_pre",
    )(*src, mod)


def _s5_params(lam_re, lam_im, log_step, b_re, b_im, c_re, c_im):
    hp = lax.Precision.HIGHEST
    L = S5_CHUNK
    n_dir, g, p = lam_re.shape
    gb = LANES // S5_GROUP
    nb = g // gb
    hch = S5_GROUP
    step = jnp.exp(log_step)[..., None]
    ar, ai = lam_re * step, lam_im * step
    k = jnp.arange(L + 1, dtype=F32)[None, :, None, None]
    mag = jnp.exp(ar[:, None] * k)
    ang = ai[:, None] * k
    pr, pi = mag * jnp.cos(ang), mag * jnp.sin(ang)
    lbr, lbi = pr[:, 1] - 1.0, pi[:, 1]
    den = lam_re * lam_re + lam_im * lam_im
    qr = (lbr * lam_re + lbi * lam_im) / den
    qi = (lbi * lam_re - lbr * lam_im) / den
    bbr = qr[..., None] * b_re - qi[..., None] * b_im
    bbi = qr[..., None] * b_im + qi[..., None] * b_re
    idx = np.arange(L * hch)
    spread_o = jnp.asarray((idx[None, :] % hch) == np.arange(hch)[:, None], F32)
    spread_k = jnp.asarray((idx[None, :] // hch) == np.arange(L)[:, None], F32)
    spread = lambda x, m: jnp.einsum('...a,an->...n', x, m, precision=hp)
    pr_t, pi_t = jnp.moveaxis(pr, 1, 3), jnp.moveaxis(pi, 1, 3)
    cr = spread(jnp.swapaxes(c_re, 2, 3), spread_o)
    ci = spread(jnp.swapaxes(c_im, 2, 3), spread_o)

    def c_lam(k0):
        pk_r = spread(pr_t[..., k0:k0 + L], spread_k)
        pk_i = spread(pi_t[..., k0:k0 + L], spread_k)
        return cr * pk_r - ci * pk_i, cr * pk_i + ci * pk_r

    bbr_t, bbi_t = jnp.swapaxes(bbr, 2, 3), jnp.swapaxes(bbi, 2, 3)
    cl0r, cl0i = c_lam(0)
    kmat = (jnp.einsum('dgip,dgpn->dgin', bbr_t, cl0r, precision=hp)
            - jnp.einsum('dgip,dgpn->dgin', bbi_t, cl0i, precision=hp))
    kc = kmat.reshape(n_dir, nb, gb * hch, L * hch)
    dup = lambda x: jnp.concatenate([x, x], axis=-1)
    pw_r, pw_i = dup(pr[:, L - 1::-1]), dup(pi[:, L - 1::-1])
    bb = jnp.concatenate([bbr_t, bbi_t], axis=-1)
    bb_rot = jnp.concatenate([-bbi_t, bbr_t], axis=-1)
    e = pw_r[:, :, :, None, :] * bb[:, None] + pw_i[:, :, :, None, :] * bb_rot[:, None]
    ec = jnp.moveaxis(e.reshape(n_dir, L, nb, gb * hch, 2 * p), 1, 2)
    cl1r, cl1i = c_lam(1)
    cl = jnp.stack([cl1r, -cl1i], axis=1).reshape(n_dir, 2, nb, gb * p, L * hch)
    cl = jnp.moveaxis(cl, 1, 2).reshape(n_dir, nb, 2 * gb * p, L * hch)
    a = jnp.stack([pr[:, L].reshape(n_dir, nb, 1, gb * p), pi[:, L].reshape(n_dir, nb, 1, gb * p)], axis=3)
    return kc.astype(BF16), ec.astype(BF16), cl.astype(BF16), a.reshape(n_dir, nb, 1, 2 * gb * p)


def _replicators():
    gb = LANES // S5_GROUP
    r16 = np.zeros((S5_CHUNK, S5_GROUP, S5_CHUNK, gb, S5_GROUP), np.float32)
    for k in range(S5_CHUNK):
        for o in range(S5_GROUP):
            r16[k, o, k, :, o] = 1.0
    r64 = np.zeros((2, S5_STATE, 2, gb, S5_STATE), np.float32)
    for part in range(2):
        for q in range(S5_STATE):
            r64[part, q, part, :, q] = 1.0
    return (jnp.asarray(r16.reshape(S5_CHUNK * S5_GROUP, S5_CHUNK * LANES), BF16),
            jnp.asarray(r64.reshape(2 * S5_STATE, 2 * gb * S5_STATE), BF16))


def _s5_scan_kernel(ul_ref, uc_ref, kc_ref, ec_ref, cl_ref, r16_ref, r64_ref, a_ref, yl_ref, yc_ref,
                    h_scr, s_scr, hs_scr, wts_scr, wy_scr, yt_scr):
    d = pl.program_id(1)
    t = pl.program_id(2)
    L = S5_CHUNK
    nstate = a_ref.shape[-1]
    half = nstate // 2
    nt = L * LANES
    blk = lambda i: slice(i * LANES, (i + 1) * LANES)

    def build_weights():
        row_gi = lax.broadcasted_iota(jnp.int32, (LANES, LANES), 0) // S5_GROUP
        lane_go = lax.broadcasted_iota(jnp.int32, (LANES, LANES), 1) // S5_GROUP
        toep = _dot(kc_ref[0, 0], r16_ref[...])
        zero = jnp.zeros((LANES, LANES), BF16)
        for lag in range(L):
            piece = jnp.where(row_gi == lane_go, toep[:, blk(lag)], 0.0).astype(BF16)
            for s in range(L - lag):
                wts_scr[blk(s), blk(s + lag)] = piece
        for tt in range(0, L, 2):
            wts_scr[blk(tt + 1), blk(tt)] = zero
        row_gi_w = lax.broadcasted_iota(jnp.int32, (LANES, nstate), 0) // S5_GROUP
        lane_gp = (lax.broadcasted_iota(jnp.int32, (LANES, nstate), 1) % half) // S5_STATE
        for s in range(L):
            e = _dot(ec_ref[0, 0, s], r64_ref[...])
            wts_scr[blk(s), nt:] = jnp.where(row_gi_w == lane_gp, e, 0.0).astype(BF16)
        row_gp = (lax.broadcasted_iota(jnp.int32, (nstate, LANES), 0) % half) // S5_STATE
        lane_go_t = lax.broadcasted_iota(jnp.int32, (nstate, LANES), 1) // S5_GROUP
        for tt in range(L):
            w = _dot(cl_ref[0, 0], r16_ref[:, blk(tt)])
            wy_scr[:, blk(tt)] = jnp.where(row_gp == lane_go_t, w, 0.0).astype(BF16)

    def process(x_ref, y_ref):
        bsz, tok, _ = x_ref.shape
        nc = tok // L
        m = bsz * nc
        flip = lambda i, n: i + d * (n - 1 - 2 * i)
        u = jnp.concatenate(
            [x_ref[:, pl.ds(flip(s, L), nc, stride=L), :].reshape(m, LANES) for s in range(L)], axis=1).astype(BF16)
        s_all = _dot(u, wts_scr[:, nt:])
        nq = nstate // LANES
        nh = nq // 2
        ncp = nc + SUBLANES
        for q in range(nq):
            for b in range(bsz):
                s_scr[q, b * ncp:b * ncp + nc, :] = s_all[b * nc:(b + 1) * nc, blk(q)]
        pair = lambda n: slice(2 * n * LANES, 2 * (n + 1) * LANES)
        for n in range(L // 2):
            yt_scr[0:m, pair(n)] = _dot(u[:, :pair(n).stop], wts_scr[:pair(n).stop, pair(n)])
        a = a_ref[0, 0]
        a_re = [jnp.broadcast_to(a[:, blk(q)], (bsz, LANES)) for q in range(nh)]
        a_im = [jnp.broadcast_to(a[:, blk(nh + q)], (bsz, LANES)) for q in range(nh)]
        h_re = [h_scr[q] for q in range(nh)]
        h_im = [h_scr[nh + q] for q in range(nh)]
        for c in range(nc):
            rows = pl.ds(flip(c, nc), bsz, stride=ncp)
            for q in range(nh):
                hs_scr.at[q][rows, :] = h_re[q]
                hs_scr.at[nh + q][rows, :] = h_im[q]
                s_re = s_scr.at[q][rows, :]
                s_im = s_scr.at[nh + q][rows, :]
                h_re[q], h_im[q] = (a_re[q] * h_re[q] - a_im[q] * h_im[q] + s_re,
                                    a_re[q] * h_im[q] + a_im[q] * h_re[q] + s_im)
        for q in range(nh):
            h_scr[q] = h_re[q]
            h_scr[nh + q] = h_im[q]
        hs = jnp.concatenate(
            [jnp.concatenate([hs_scr[q, b * ncp:b * ncp + nc, :] for b in range(bsz)], axis=0) for q in range(nq)],
            axis=1).astype(BF16)
        for n in range(L // 2):
            y = yt_scr[0:m, pair(n)] + _dot(hs, wy_scr[:, pair(n)])
            for tt in (2 * n, 2 * n + 1):
                y_ref[:, pl.ds(flip(tt, L), nc, stride=L), :] = (
                    y[:, blk(tt - 2 * n)].reshape(bsz, nc, LANES))

    @pl.when(t == 0)
    def _():
        build_weights()
        h_scr[...] = jnp.zeros_like(h_scr)
        process(uc_ref, yc_ref)

    @pl.when(t > 0)
    def _():
        process(ul_ref, yl_ref)


def _s5_scan(u3, kc, ec, cl, a, layer, seq, ctx_len):
    bsz, _, d = u3.shape
    tl = 1024
    ntl = seq // tl
    nb = d // LANES
    n_dir = 2
    ctx_blk = seq // ctx_len
    nstate = a.shape[-1]
    nt = S5_CHUNK * LANES
    r16, r64 = _replicators()

    def lat_idx(dd, t):
        i = jnp.maximum(t - 1, 0)
        return jnp.where(dd == 0, i, ntl - 1 - i)

    per_block = lambda arr: pl.BlockSpec((None, 1, 1) + arr.shape[3:],
                                         lambda j, dd, t: (layer, dd, j) + (0,) * (arr.ndim - 3))
    const = lambda arr: pl.BlockSpec(arr.shape, lambda j, dd, t: (0,) * arr.ndim)
    return pl.pallas_call(
        _s5_scan_kernel,
        out_shape=(jax.ShapeDtypeStruct((n_dir, bsz, seq, d), F32),
                   jax.ShapeDtypeStruct((n_dir, bsz, ctx_len, d), F32)),
        grid=(nb, n_dir, ntl + 1),
        in_specs=[
            pl.BlockSpec((bsz, tl, LANES), lambda j, dd, t: (0, lat_idx(dd, t), j)),
            pl.BlockSpec((bsz, ctx_len, LANES), lambda j, dd, t: (0, ctx_blk, j)),
            per_block(kc), per_block(ec), per_block(cl), const(r16), const(r64), per_block(a),
        ],
        out_specs=(
            pl.BlockSpec((None, bsz, tl, LANES), lambda j, dd, t: (dd, 0, lat_idx(dd, t), j)),
            pl.BlockSpec((None, bsz, ctx_len, LANES), lambda j, dd, t: (dd, 0, 0, j)),
        ),
        scratch_shapes=[
            pltpu.VMEM((nstate // LANES, bsz, LANES), F32),
            pltpu.VMEM((nstate // LANES, bsz * (tl // S5_CHUNK + SUBLANES), LANES), F32),
            pltpu.VMEM((nstate // LANES, bsz * (tl // S5_CHUNK + SUBLANES), LANES), F32),
            pltpu.VMEM((nt, nt + nstate), BF16),
            pltpu.VMEM((nstate, nt), BF16),
            pltpu.VMEM((bsz * tl // S5_CHUNK, nt), F32),
        ],
        compiler_params=_cparams(("arbitrary", "arbitrary", "arbitrary")),
        name="s5_scan",
    )(u3, u3, kc, ec, cl, r16, r64, a)


def _gelu_tanh(x):
    return 0.5 * x * (1.0 + jnp.tanh(math.sqrt(2.0 / math.pi) * (x + 0.044715 * (x * x * x))))


def _s5_glu_kernel(n_src, tiles_per_seq, *refs):
    ylf_ref, ylr_ref, ycf_ref, ycr_ref, mod_ref, dvec_ref, w_ref, b_ref, o_ref = refs[n_src:]
    is_ctx = (pl.program_id(0) % tiles_per_seq) == tiles_per_seq - 1
    y_scan = jnp.where(is_ctx, ycf_ref[0, 0] + ycr_ref[0, 0], ylf_ref[0, 0] + ylr_ref[0, 0])
    h = _stream_tile(refs[:n_src], tiles_per_seq)
    mod = mod_ref[0]
    y = dvec_ref[...] * _normmod(h, mod[GS_M:GS_M + 1], mod[SH_M:SH_M + 1]) + y_scan
    z = _gelu_tanh(y)
    gate = 1.0 / (1.0 + jnp.exp(-(_dot(z.astype(BF16), w_ref[...]) + b_ref[...])))
    o_ref[...] = h + mod[G_M:G_M + 1] * (z * gate)


def _s5_glu(src, y_lat, y_ctx, mod, dvec, glu_w, glu_b, seq):
    d = mod.shape[-1]
    r = mod.shape[0] * TILE
    lat_tiles = seq // TILE
    tps = lat_tiles + 1

    def lat_map(dd):
        return lambda i: (dd, i // tps, jnp.minimum(i % tps, lat_tiles - 1), 0)

    def ctx_map(dd):
        return lambda i: (dd, i // tps, 0, 0)

    row = pl.BlockSpec((TILE, d), lambda i: (i, 0))
    vec = pl.BlockSpec((1, d), lambda i: (0, 0))
    return pl.pallas_call(
        functools.partial(_s5_glu_kernel, len(src), tps),
        out_shape=jax.ShapeDtypeStruct((r, d), F32),
        grid=(r // TILE,),
        in_specs=_stream_specs(src, tps, d) + [
                  pl.BlockSpec((1, 1, TILE, d), lat_map(0)), pl.BlockSpec((1, 1, TILE, d), lat_map(1)),
                  pl.BlockSpec((1, 1, TILE, d), ctx_map(0)), pl.BlockSpec((1, 1, TILE, d), ctx_map(1)),
                  pl.BlockSpec((1, MOD_ROWS, d), lambda i: (i, 0, 0)),
                  vec, pl.BlockSpec((d, d), lambda i: (0, 0)), vec],
        out_specs=row,
        compiler_params=_cparams(("arbitrary",)),
        name="s5_glu",
    )(*src, y_lat, y_lat, y_ctx, y_ctx, mod, dvec.reshape(1, d), glu_w.astype(BF16), glu_b.reshape(1, d))


def _pool_matrices(width, tile):
    mats = np.zeros((len(POOL_WINDOWS), tile, tile), np.float32)
    pos = np.arange(width)
    for g, win in enumerate(POOL_WINDOWS):
        lo = np.clip(pos - win // 2, 0, width)
        hi = np.clip(pos + win // 2, 0, width)
        for base in range(0, tile, width):
            for j in range(width):
                mats[g, base + j, base + lo[j]:base + hi[j]] = 1.0 / float(hi[j] - lo[j])
    return mats


def _pool_kernel(tps, h_ref, mod_ref, a_ref, w_ref, b_ref, sc_ref, o_ref):
    gw = w_ref.shape[-1]
    nsub = h_ref.shape[0] // TILE
    for j in range(nsub):
        rows = slice(j * TILE, (j + 1) * TILE)
        mod = mod_ref[j]
        is_ctx = (pl.program_id(0) * nsub + j) % tps == tps - 1
        h = h_ref[rows, :]
        u = _normmod(h, mod[GS_M:GS_M + 1], mod[SH_M:SH_M + 1])
        ub = u.astype(BF16)
        for g in range(w_ref.shape[0]):
            cols = slice(g * gw, (g + 1) * gw)
            avg = jnp.where(is_ctx, a_ref[1, g], a_ref[0, g])
            res = _dot(avg, ub[:, cols]) - u[:, cols]
            mixed = _dot(res.astype(BF16), w_ref[g]) + b_ref[:, cols]
            o_ref[rows, cols] = h[:, cols] + mod[G_M:G_M + 1, cols] * (mixed * sc_ref[:, cols])


def _pool(h, mod, pool_w, pool_b, pool_scale, seq):
    r, d = h.shape
    tps = seq // TILE + 1
    ng, gw, _ = pool_w.shape
    mats = jnp.asarray(np.stack([_pool_matrices(GRID_W, TILE), _pool_matrices(TILE, TILE)]), BF16)
    row = pl.BlockSpec((ROW_BLOCK, d), lambda i: (i, 0))
    vec = pl.BlockSpec((1, d), lambda i: (0, 0))
    return pl.pallas_call(
        functools.partial(_pool_kernel, tps),
        out_shape=jax.ShapeDtypeStruct((r, d), F32),
        grid=(r // ROW_BLOCK,),
        in_specs=[row, pl.BlockSpec((ROW_BLOCK // TILE, MOD_ROWS, d), lambda i: (i, 0, 0)),
                  pl.BlockSpec((2, ng, TILE, TILE), lambda i: (0, 0, 0, 0)),
                  pl.BlockSpec((ng, gw, gw), lambda i: (0, 0, 0)), vec, vec],
        out_specs=row,
        compiler_params=_cparams(("arbitrary",)),
        name="pool_mixer",
    )(h, mod, mats, pool_w.astype(BF16), pool_b.reshape(1, d), pool_scale.reshape(1, d))


def _dft_tables(n):
    idx = np.arange(n)
    ang = 2.0 * np.pi * ((idx[:, None] * idx[None, :]) % n) / n
    scale = 1.0 / math.sqrt(n)
    return (np.cos(ang) * scale).astype(np.float32), (np.sin(ang) * scale).astype(np.float32)


def _fnet_fold_kernel(cc_ref, sc_ref, w_ref, o_ref):
    w = w_ref[...]
    d = w.shape[-1]
    o_ref[:, :d] = _dot3(cc_ref[...], w).astype(o_ref.dtype)
    o_ref[:, d:] = (-_dot3(sc_ref[...], w)).astype(o_ref.dtype)


def _fnet_fold(fnet_w):
    d = fnet_w.shape[0]
    gw = d // FNET_GROUPS
    cc, sc = _dft_tables(gw)
    sq = pl.BlockSpec((gw, gw), lambda i: (0, 0))
    return pl.pallas_call(
        _fnet_fold_kernel,
        out_shape=jax.ShapeDtypeStruct((d, 2 * d), BF16),
        grid=(FNET_GROUPS,),
        in_specs=[sq, sq, pl.BlockSpec((gw, d), lambda i: (i, 0))],
        out_specs=pl.BlockSpec((gw, 2 * d), lambda i: (i, 0)),
        compiler_params=_cparams(("arbitrary",)),
        name="fnet_fold",
    )(jnp.asarray(cc), jnp.asarray(sc), fnet_w)


def _fnet_proj_kernel(h_ref, mod_ref, g_ref, v_ref):
    for j in range(h_ref.shape[0] // TILE):
        rows = slice(j * TILE, (j + 1) * TILE)
        mod = mod_ref[j]
        u = _normmod(h_ref[rows, :], mod[GS_M:GS_M + 1], mod[SH_M:SH_M + 1])
        v_ref[rows, :] = _dot(u.astype(BF16), g_ref[...]).astype(v_ref.dtype)


def _fnet_proj(h, mod, g):
    r, d = h.shape
    sub = ROW_BLOCK // TILE
    return pl.pallas_call(
        _fnet_proj_kernel,
        out_shape=jax.ShapeDtypeStruct((r, 2 * d), BF16),
        grid=(r // ROW_BLOCK,),
        in_specs=[pl.BlockSpec((ROW_BLOCK, d), lambda i: (i, 0)),
                  pl.BlockSpec((sub, MOD_ROWS, d), lambda i: (i, 0, 0)),
                  pl.BlockSpec((d, 2 * d), lambda i: (0, 0))],
        out_specs=pl.BlockSpec((ROW_BLOCK, 2 * d), lambda i: (i, 0)),
        compiler_params=_cparams(("arbitrary",)),
        name="fnet_proj",
    )(h, mod, g)


def _dft_matrix(n):
    n1 = int(round(math.sqrt(n)))
    assert n1 * n1 == n
    hi = jnp.arange(n1, dtype=jnp.int32)[:, None]
    k = jnp.arange(n, dtype=jnp.int32)[None, :]
    ang_a = (2.0 * math.pi / n1) * ((hi * k) % n1).astype(F32)
    ang_b = (2.0 * math.pi / n) * ((hi * k) % n).astype(F32)
    ca, sa, cb, sb = jnp.cos(ang_a), jnp.sin(ang_a), jnp.cos(ang_b), jnp.sin(ang_b)
    scale = 1.0 / math.sqrt(n)
    c = (ca[:, None] * cb[None] - sa[:, None] * sb[None]).reshape(n, n) * scale
    s = (sa[:, None] * cb[None] + ca[:, None] * sb[None]).reshape(n, n) * scale
    return jnp.concatenate([c, s], axis=1).astype(BF16)


def _fnet_lat_kernel(f_ref, v_ref, h_ref, mod_ref, b_ref, o_ref, acc_ref):
    k = pl.program_id(2)

    @pl.when(k == 0)
    def _():
        acc_ref[...] = jnp.zeros_like(acc_ref)

    acc_ref[...] += _dot(f_ref[...], v_ref[0])

    @pl.when(k == pl.num_programs(2) - 1)
    def _():
        for j in range(h_ref.shape[1] // TILE):
            rows = slice(j * TILE, (j + 1) * TILE)
            o_ref[0, rows, :] = h_ref[0, rows, :] + mod_ref[0, j, G_M:G_M + 1] * (acc_ref[rows, :] + b_ref[...])


def _fnet_lat(dft, v3, h3, mod4, fnet_b, seq):
    bsz, tot, d = h3.shape
    tm, tk = ROW_BLOCK, min(2048, seq)
    kh = seq // tk
    sub = tm // TILE
    return pl.pallas_call(
        _fnet_lat_kernel,
        out_shape=jax.ShapeDtypeStruct((bsz, tot, d), F32),
        grid=(bsz, seq // tm, 2 * kh),
        in_specs=[pl.BlockSpec((tm, tk), lambda b, m, k: (m, k)),
                  pl.BlockSpec((1, tk, d), lambda b, m, k: (b, k % kh, k // kh)),
                  pl.BlockSpec((1, tm, d), lambda b, m, k: (b, m, 0)),
                  pl.BlockSpec((1, sub, MOD_ROWS, d), lambda b, m, k: (b, m, 0, 0)),
                  pl.BlockSpec((1, d), lambda b, m, k: (0, 0))],
        out_specs=pl.BlockSpec((1, tm, d), lambda b, m, k: (b, m, 0)),
        scratch_shapes=[pltpu.VMEM((tm, d), F32)],
        compiler_params=_cparams(("arbitrary", "arbitrary", "arbitrary")),
        name="fnet_dft_latent",
    )(dft, v3, h3, mod4, fnet_b.reshape(1, d))


def _fnet_ctx_kernel(f_ref, v_ref, h_ref, mod_ref, b_ref, prev_ref, o_ref):
    del prev_ref
    n = f_ref.shape[0]
    d = h_ref.shape[-1]
    acc = _dot(f_ref[:, :n], v_ref[0, :, :d]) + _dot(f_ref[:, n:], v_ref[0, :, d:])
    o_ref[0] = h_ref[0] + mod_ref[0, 0, G_M:G_M + 1] * (acc + b_ref[...])


def _fnet_ctx(dftc, v3, h3, mod4, fnet_b, partial, seq):
    bsz, tot, d = h3.shape
    n = tot - seq
    blk = seq // n
    return pl.pallas_call(
        _fnet_ctx_kernel,
        out_shape=jax.ShapeDtypeStruct((bsz, tot, d), F32),
        grid=(bsz,),
        in_specs=[pl.BlockSpec((n, 2 * n), lambda b: (0, 0)),
                  pl.BlockSpec((1, n, 2 * d), lambda b: (b, blk, 0)),
                  pl.BlockSpec((1, n, d), lambda b: (b, blk, 0)),
                  pl.BlockSpec((1, 1, MOD_ROWS, d), lambda b: (b, blk, 0, 0)),
                  pl.BlockSpec((1, d), lambda b: (0, 0)),
                  pl.BlockSpec(memory_space=pl.ANY)],
        out_specs=pl.BlockSpec((1, n, d), lambda b: (b, blk, 0)),
        input_output_aliases={5: 0},
        compiler_params=_cparams(("arbitrary",)),
        name="fnet_dft_context",
    )(dftc, v3, h3, mod4, fnet_b.reshape(1, d), partial)


def _silu(x):
    return x * (1.0 / (1.0 + jnp.exp(-x)))


def _ffn_kernel(h_ref, mod_ref, wg_ref, wu_ref, wd_ref, o_ref, u_scr, acc_ref):
    f = pl.program_id(1)
    sub = h_ref.shape[0] // TILE

    @pl.when(f == 0)
    def _():
        for j in range(sub):
            rows = slice(j * TILE, (j + 1) * TILE)
            mod = mod_ref[j]
            u_scr[rows, :] = _normmod(h_ref[rows, :], mod[GS_F:GS_F + 1], mod[SH_F:SH_F + 1]).astype(BF16)
        acc_ref[...] = jnp.zeros_like(acc_ref)

    u = u_scr[...]
    a = _silu(_dot(u, wg_ref[...])) * _dot(u, wu_ref[...])
    acc_ref[...] += _dot(a.astype(BF16), wd_ref[...])

    @pl.when(f == pl.num_programs(1) - 1)
    def _():
        for j in range(sub):
            rows = slice(j * TILE, (j + 1) * TILE)
            o_ref[rows, :] = h_ref[rows, :] + mod_ref[j, G_F:G_F + 1] * acc_ref[rows, :]


def _ffn(h, mod, w_gate, w_up, w_down):
    r, d = h.shape
    ff = w_gate.shape[1]
    tm, tf = ROW_BLOCK, 256
    sub = tm // TILE
    return pl.pallas_call(
        _ffn_kernel,
        out_shape=jax.ShapeDtypeStruct((r, d), F32),
        grid=(r // tm, ff // tf),
        in_specs=[pl.BlockSpec((tm, d), lambda i, f: (i, 0)),
                  pl.BlockSpec((sub, MOD_ROWS, d), lambda i, f: (i, 0, 0)),
                  pl.BlockSpec((d, tf), lambda i, f: (0, f)),
                  pl.BlockSpec((d, tf), lambda i, f: (0, f)),
                  pl.BlockSpec((tf, d), lambda i, f: (f, 0))],
        out_specs=pl.BlockSpec((tm, d), lambda i, f: (i, 0)),
        scratch_shapes=[pltpu.VMEM((tm, d), BF16), pltpu.VMEM((tm, d), F32)],
        compiler_params=_cparams(("arbitrary", "arbitrary")),
        name="ffn_swiglu",
    )(h, mod, w_gate.astype(BF16), w_up.astype(BF16), w_down.astype(BF16))


MOE_TILE = 2048
MOE_CHUNK = 128


def _router_kernel(tiles_per_seq, route_ctx, h_ref, mod_ref, r_ref, u_ref, cw_ref, slot_ref, slott_ref, cnt_ref):
    nsub = h_ref.shape[0] // TILE
    lane = lax.broadcasted_iota(jnp.int32, (TILE, LANES), 1).astype(F32)
    earlier = jnp.where(lax.broadcasted_iota(jnp.int32, (TILE, TILE), 1)
                        < lax.broadcasted_iota(jnp.int32, (TILE, TILE), 0), 1.0, 0.0).astype(BF16)
    neg = jnp.float32(-jnp.inf)
    count = jnp.zeros((1, LANES), F32)
    for j in range(nsub):
        rows = slice(j * TILE, (j + 1) * TILE)
        mod = mod_ref[j]
        u = _normmod(h_ref[rows, :], mod[GS_F:GS_F + 1], mod[SH_F:SH_F + 1])
        u_ref[rows, :] = u.astype(BF16)
        logits = _dot3(u, r_ref[...])
        logits = jnp.where(lane < N_EXPERTS, logits, neg)
        m1 = jnp.max(logits, axis=-1, keepdims=True)
        i1 = jnp.min(jnp.where(logits == m1, lane, float(LANES)), axis=-1, keepdims=True)
        rest = jnp.where(lane == i1, neg, logits)
        m2 = jnp.max(rest, axis=-1, keepdims=True)
        i2 = jnp.min(jnp.where(rest == m2, lane, float(LANES)), axis=-1, keepdims=True)
        e2 = jnp.exp(m2 - m1)
        w1 = 1.0 / (1.0 + e2)
        w2 = e2 / (1.0 + e2)
        cw = jnp.where(lane == i1, w1, jnp.where(lane == i2, w2, 0.0))
        sel = jnp.where(lane == i1, 1.0, jnp.where(lane == i2, 1.0, 0.0))
        if not route_ctx:
            is_ctx = (pl.program_id(0) * nsub + j) % tiles_per_seq == tiles_per_seq - 1
            cw = jnp.where(is_ctx, 0.0, cw)
            sel = jnp.where(is_ctx, 0.0, sel)
        cw_ref[rows, :] = cw
        slot = jnp.where(sel > 0.0, _dot(earlier, sel.astype(BF16)) + count, -1.0)
        slot_ref[rows, :] = slot
        slott_ref[:, rows] = slot.T[:N_EXPERTS, :]
        count = count + jnp.sum(sel, axis=0, keepdims=True)
    cnt_ref[0] = jnp.broadcast_to(count, (N_EXPERTS, LANES))


def _router(h, mod, router, tiles_per_seq, route_ctx):
    r, d = h.shape
    rp = jnp.zeros((d, LANES), F32).at[:, :N_EXPERTS].set(router)
    sub = MOE_TILE // TILE
    row = lambda w: pl.BlockSpec((MOE_TILE, w), lambda i: (i, 0))
    return pl.pallas_call(
        functools.partial(_router_kernel, tiles_per_seq, route_ctx),
        out_shape=(jax.ShapeDtypeStruct((r, d), BF16), jax.ShapeDtypeStruct((r, LANES), F32),
                   jax.ShapeDtypeStruct((r, LANES), F32), jax.ShapeDtypeStruct((N_EXPERTS, r), F32),
                   jax.ShapeDtypeStruct((r // MOE_TILE, N_EXPERTS, LANES), F32)),
        grid=(r // MOE_TILE,),
        in_specs=[row(d),
                  pl.BlockSpec((sub, MOD_ROWS, d), lambda i: (i, 0, 0)),
                  pl.BlockSpec((d, LANES), lambda i: (0, 0))],
        out_specs=(row(d), row(LANES), row(LANES),
                   pl.BlockSpec((N_EXPERTS, MOE_TILE), lambda i: (0, i)),
                   pl.BlockSpec((1, N_EXPERTS, LANES), lambda i: (i, 0, 0))),
        compiler_params=_cparams(("arbitrary",)),
        name="moe_router",
    )(h, mod, rp)


def _moe_kernel(nch_ref, h_ref, u_ref, cw_ref, slot_ref, slott_ref, mod_ref, wg_ref, wu_ref, wd_ref, o_ref,
                xs_scr, y_scr):
    i = pl.program_id(0)
    e = pl.program_id(1)
    f = pl.program_id(2)
    last_f = pl.num_programs(2) - 1
    tile = u_ref.shape[0]
    nch = nch_ref[i * N_EXPERTS + e]
    block = 4 * MOE_CHUNK

    @pl.when((e == 0) & (f == 0))
    def _():
        o_ref[...] = jnp.zeros_like(o_ref)

    nblk = lax.shift_right_logical(nch, 2)
    tail2 = pl.multiple_of(nblk * block, MOE_CHUNK)
    tail1 = pl.multiple_of(tail2 + (nch & 2) * MOE_CHUNK, MOE_CHUNK)

    def run_block(r0, rn):
        rows = pl.ds(r0, rn)

        @pl.when(f == 0)
        def _():
            rid = (lax.broadcasted_iota(jnp.int32, (rn, 1), 0) + r0).astype(F32)
            pick = jnp.where(slott_ref[pl.ds(e, 1), :] == rid, 1.0, 0.0).astype(BF16)
            xs_scr[rows, :] = _dot(pick, u_ref[...]).astype(BF16)

        xs = xs_scr[rows, :]
        a = _silu(_dot(xs, wg_ref[0])) * _dot(xs, wu_ref[0])
        part = _dot(a.astype(BF16), wd_ref[0])

        @pl.when(f == 0)
        def _():
            y_scr[rows, :] = part

        @pl.when(f > 0)
        def _():
            y_scr[rows, :] += part

    def loop_blocks(fn):
        def body(b, carry):
            fn(pl.multiple_of(b * block, block), block)
            return carry
        lax.fori_loop(0, nblk, body, 0)
        pl.when((nch & 2) != 0)(lambda: fn(tail2, 2 * MOE_CHUNK))
        pl.when((nch & 1) != 0)(lambda: fn(tail1, MOE_CHUNK))

    loop_blocks(run_block)

    @pl.when((f == last_f) & (nch > 0))
    def _():
        mine = lax.broadcasted_iota(jnp.int32, (tile, LANES), 1) == e
        slot_col = jnp.sum(jnp.where(mine, slot_ref[...], 0.0), axis=-1, keepdims=True)
        cw_col = jnp.sum(jnp.where(mine, cw_ref[...], 0.0), axis=-1, keepdims=True)

        def scatter_block(r0, rn):
            cid = (lax.broadcasted_iota(jnp.int32, (1, rn), 1) + r0).astype(F32)
            put = jnp.where(slot_col == cid, 1.0, 0.0).astype(BF16)
            o_ref[...] += cw_col * _dot(put, y_scr[pl.ds(r0, rn), :].astype(BF16))

        loop_blocks(scatter_block)

    @pl.when((e == pl.num_programs(1) - 1) & (f == last_f))
    def _():
        for j in range(tile // TILE):
            rows = slice(j * TILE, (j + 1) * TILE)
            o_ref[rows, :] = h_ref[rows, :] + mod_ref[j, G_F:G_F + 1] * o_ref[rows, :]


def _moe(h, u, cw, slot, slott, counts, mod, w_gate, w_up, w_down, layer):
    r, d = h.shape
    _, ne, _, ff = w_gate.shape
    tm, tf = MOE_TILE, 512
    sub = tm // TILE
    nch = ((counts[:, 0, :ne].astype(jnp.int32) + (MOE_CHUNK - 1)) // MOE_CHUNK).reshape(-1)
    once = pl.Buffered(1)
    row = lambda w: pl.BlockSpec((tm, w), lambda i, e, f, n: (i, 0), pipeline_mode=once)
    return pl.pallas_call(
        _moe_kernel,
        out_shape=jax.ShapeDtypeStruct((r, d), F32),
        grid_spec=pltpu.PrefetchScalarGridSpec(
            num_scalar_prefetch=1,
            grid=(r // tm, ne, ff // tf),
            in_specs=[row(d), row(d), row(LANES), row(LANES),
                      pl.BlockSpec((ne, tm), lambda i, e, f, n: (0, i), pipeline_mode=once),
                      pl.BlockSpec((sub, MOD_ROWS, d), lambda i, e, f, n: (i, 0, 0)),
                      pl.BlockSpec((None, 1, d, tf), lambda i, e, f, n: (layer, e, 0, f)),
                      pl.BlockSpec((None, 1, d, tf), lambda i, e, f, n: (layer, e, 0, f)),
                      pl.BlockSpec((None, 1, tf, d), lambda i, e, f, n: (layer, e, f, 0))],
            out_specs=pl.BlockSpec((tm, d), lambda i, e, f, n: (i, 0), pipeline_mode=once),
            scratch_shapes=[pltpu.VMEM((tm, d), BF16), pltpu.VMEM((tm, d), F32)]),
        compiler_params=_cparams(("arbitrary", "arbitrary", "arbitrary")),
        name="moe_experts",
    )(nch, h, u, cw, slot, slott, mod, w_gate, w_up, w_down)


def _final_kernel(h_ref, g_ref, o_ref):
    h = h_ref[0]
    inv = lax.rsqrt(jnp.mean(h * h, axis=-1, keepdims=True) + EPS)
    o_ref[0] = h * inv * g_ref[...]


def _final_norm(h3, gain, seq):
    bsz, _, d = h3.shape
    return pl.pallas_call(
        _final_kernel,
        out_shape=jax.ShapeDtypeStruct((bsz, seq, d), F32),
        grid=(bsz, seq // ROW_BLOCK),
        in_specs=[pl.BlockSpec((1, ROW_BLOCK, d), lambda b, t: (b, t, 0)),
                  pl.BlockSpec((1, d), lambda b, t: (0, 0))],
        out_specs=pl.BlockSpec((1, ROW_BLOCK, d), lambda b, t: (b, t, 0)),
        compiler_params=_cparams(("arbitrary", "arbitrary")),
        name="final_norm",
    )(h3, gain.reshape(1, d))


def _tile_mods(mods, gain_mix, gain_ffn, bsz, lat_tiles):
    sh_m, sc_m, g_m, sh_f, sc_f, g_f = (mods[:, i] for i in range(N_MOD))
    rows = jnp.stack([gain_mix * (1.0 + sc_m), sh_m, g_m, gain_ffn * (1.0 + sc_f), sh_f, g_f,
                      jnp.zeros_like(g_f), jnp.zeros_like(g_f)], axis=1)
    lat = jnp.broadcast_to(rows[:bsz, None], (bsz, lat_tiles) + rows.shape[1:])
    ctx = jnp.broadcast_to(rows[bsz:, None], (bsz, 1) + rows.shape[1:])
    return jnp.concatenate([lat, ctx], axis=1).reshape(bsz * (lat_tiles + 1), MOD_ROWS, rows.shape[-1])


def kernel(x, c, ctx, c_ctx, ada_w, ada_b, norm_mix, norm_ffn, norm_final, s5_lambda_re, s5_lambda_im, s5_log_step, s5_b_re, s5_b_im, s5_c_re, s5_c_im, s5_d, s5_glu_w, s5_glu_b, pool_w, pool_b, pool_scale, fnet_w, fnet_b, ffn_w_gate, ffn_w_up, ffn_w_down, moe_router, moe_w_gate, moe_w_up, moe_w_down):
    bsz, seq, d = x.shape
    ctx_len = ctx.shape[1]
    depth = ada_w.shape[0]
    assert ctx_len == TILE and seq % ROW_BLOCK == 0 and d % LANES == 0
    tot = seq + ctx_len
    lat_tiles = seq // TILE
    tps = lat_tiles + 1
    r = bsz * tot
    assert r % ROW_BLOCK == 0

    cond = jnp.zeros((16, d), F32).at[:bsz].set(c).at[bsz].set(c_ctx)
    mods_all = _ada_all(cond, ada_w, ada_b)[:, :bsz + 1].reshape(depth, bsz + 1, N_MOD, d)

    moe_wg, moe_wu, moe_wd = (w.astype(BF16) for w in (moe_w_gate, moe_w_up, moe_w_down))
    mods_tiled = jax.vmap(lambda m, gm, gf: _tile_mods(m, gm, gf, bsz, lat_tiles))(mods_all, norm_mix, norm_ffn)
    s5_tables = jax.vmap(_s5_params)(s5_lambda_re, s5_lambda_im, s5_log_step, s5_b_re, s5_b_im, s5_c_re, s5_c_im)
    h = None
    for i in range(depth):
        kind, j = i % 3, i // 3
        mod = mods_tiled[i]
        if kind == 0:
            src = (x, ctx) if h is None else (h,)
            u = _s5_pre(src, mod, tps)
            y_lat, y_ctx = _s5_scan(u.reshape(bsz, tot, d), *s5_tables, j, seq, ctx_len)
            h = _s5_glu(src, y_lat, y_ctx, mod, s5_d[j], s5_glu_w[j], s5_glu_b[j], seq)
        elif kind == 1:
            h = _pool(h, mod, pool_w[j], pool_b[j], pool_scale[j], seq)
        else:
            g = _fnet_fold(fnet_w[j])
            v3 = _fnet_proj(h, mod, g).reshape(bsz, tot, 2 * d)
            h3 = h.reshape(bsz, tot, d)
            mod4 = mod.reshape(bsz, tps, MOD_ROWS, d)
            part = _fnet_lat(_dft_matrix(seq), v3, h3, mod4, fnet_b[j], seq)
            cc, sc = _dft_tables(ctx_len)
            dftc = jnp.asarray(np.concatenate([cc, sc], axis=1), BF16)
            h = _fnet_ctx(dftc, v3, h3, mod4, fnet_b[j], part, seq).reshape(r, d)
        kk = i // 2
        if i % 2 == 0:
            h = _ffn(h, mod, ffn_w_gate[kk], ffn_w_up[kk], ffn_w_down[kk])
        else:
            ub, cw, slot, slott, counts = _router(h, mod, moe_router[kk], tps, route_ctx=i < depth - 1)
            h = _moe(h, ub, cw, slot, slott, counts, mod, moe_wg, moe_wu, moe_wd, kk)
    return _final_norm(h.reshape(bsz, tot, d), norm_final, seq)
```

```python
import functools
import math

import numpy as np
import jax
import jax.numpy as jnp
from jax import lax
from jax.experimental import pallas as pl
from jax.experimental.pallas import tpu as pltpu

F32 = jnp.float32
BF16 = jnp.bfloat16

EPS = 1e-6
GRID_W = 64
N_MOD = 6
S5_GROUP = 16
S5_STATE = 64
S5_CHUNK = 16
POOL_WINDOWS = (2, 4, 8, 16)
FNET_GROUPS = 4
N_EXPERTS = 8

LANES = 128
SUBLANES = 8
TILE = 256
ROW_BLOCK = 1024
VMEM_LIMIT = 56 * 1024 * 1024


def _cparams(sem, vmem=VMEM_LIMIT):
    return pltpu.CompilerParams(dimension_semantics=sem, vmem_limit_bytes=vmem)


def _dot(a, b):
    return jnp.dot(a, b, preferred_element_type=F32)


def _split(a):
    hi = a.astype(BF16)
    lo = (a - hi.astype(F32)).astype(BF16)
    return hi, lo


def _dot3(a, b):
    a_hi, a_lo = _split(a)
    b_hi, b_lo = _split(b)
    return _dot(a_hi, b_hi) + (_dot(a_lo, b_hi) + _dot(a_hi, b_lo))


def _normmod(h, gs, sh):
    inv = lax.rsqrt(jnp.mean(h * h, axis=-1, keepdims=True) + EPS)
    return h * inv * gs + sh


GS_M, SH_M, G_M, GS_F, SH_F, G_F = range(6)
MOD_ROWS = 8


def _ada_kernel(c_ref, w_ref, b_ref, o_ref):
    c = c_ref[...]
    s = c * (1.0 / (1.0 + jnp.exp(-c)))
    o_ref[0] = _dot3(s, w_ref[0]) + b_ref[0]


def _ada_all(cond, ada_w, ada_b):
    depth, d, n = ada_w.shape
    rows = cond.shape[0]
    tn = 1536
    return pl.pallas_call(
        _ada_kernel,
        out_shape=jax.ShapeDtypeStruct((depth, rows, n), F32),
        grid=(depth, n // tn),
        in_specs=[
            pl.BlockSpec((rows, d), lambda i, j: (0, 0)),
            pl.BlockSpec((1, d, tn), lambda i, j: (i, 0, j)),
            pl.BlockSpec((1, 1, tn), lambda i, j: (i, 0, j)),
        ],
        out_specs=pl.BlockSpec((1, rows, tn), lambda i, j: (i, 0, j)),
        compiler_params=_cparams(("arbitrary", "arbitrary")),
        name="ada_params",
    )(cond, ada_w, ada_b.reshape(depth, 1, n))


def _stream_specs(src, tps, d):
    if len(src) == 1:
        return [pl.BlockSpec((TILE, d), lambda i: (i, 0))]
    return [pl.BlockSpec((None, TILE, d), lambda i: (i // tps, jnp.minimum(i % tps, tps - 2), 0)),
            pl.BlockSpec((None, TILE, d), lambda i: (i // tps, 0, 0))]


def _stream_tile(refs, tps):
    if len(refs) == 1:
        return refs[0][...]
    is_ctx = (pl.program_id(0) % tps) == tps - 1
    return jnp.where(is_ctx, refs[1][...], refs[0][...])


def _s5_pre_kernel(n_src, tps, *refs):
    mod_ref, u_ref = refs[n_src:]
    if n_src == 1:
        for j in range(u_ref.shape[0] // TILE):
            rows = slice(j * TILE, (j + 1) * TILE)
            mod = mod_ref[j]
            u_ref[rows, :] = _normmod(refs[0][rows, :], mod[GS_M:GS_M + 1], mod[SH_M:SH_M + 1])
    else:
        mod = mod_ref[0]
        u_ref[...] = _normmod(_stream_tile(refs[:n_src], tps), mod[GS_M:GS_M + 1], mod[SH_M:SH_M + 1])


def _s5_pre(src, mod, tps):
    r, _, d = mod.shape
    r *= TILE
    if len(src) == 1:
        rows = ROW_BLOCK
        src_specs = [pl.BlockSpec((rows, d), lambda i: (i, 0))]
    else:
        rows = TILE
        src_specs = _stream_specs(src, tps, d)
    return pl.pallas_call(
        functools.partial(_s5_pre_kernel, len(src), tps),
        out_shape=jax.ShapeDtypeStruct((r, d), F32),
        grid=(r // rows,),
        in_specs=src_specs + [pl.BlockSpec((rows // TILE, MOD_ROWS, d), lambda i: (i, 0, 0))],
        out_specs=pl.BlockSpec((rows, d), lambda i: (i, 0)),
        compiler_params=_cparams(("arbitrary",)),
        name="s5_pre",
    )(*src, mod)


def _s5_params(lam_re, lam_im, log_step, b_re, b_im, c_re, c_im):
    hp = lax.Precision.HIGHEST
    L = S5_CHUNK
    n_dir, g, p = lam_re.shape
    gb = LANES // S5_GROUP
    nb = g // gb
    hch = S5_GROUP
    step = jnp.exp(log_step)[..., None]
    ar, ai = lam_re * step, lam_im * step
    k = jnp.arange(L + 1, dtype=F32)[None, :, None, None]
    mag = jnp.exp(ar[:, None] * k)
    ang = ai[:, None] * k
    pr, pi = mag * jnp.cos(ang), mag * jnp.sin(ang)
    lbr, lbi = pr[:, 1] - 1.0, pi[:, 1]
    den = lam_re * lam_re + lam_im * lam_im
    qr = (lbr * lam_re + lbi * lam_im) / den
    qi = (lbi * lam_re - lbr * lam_im) / den
    bbr = qr[..., None] * b_re - qi[..., None] * b_im
    bbi = qr[..., None] * b_im + qi[..., None] * b_re
    idx = np.arange(L * hch)
    spread_o = jnp.asarray((idx[None, :] % hch) == np.arange(hch)[:, None], F32)
    spread_k = jnp.asarray((idx[None, :] // hch) == np.arange(L)[:, None], F32)
    spread = lambda x, m: jnp.einsum('...a,an->...n', x, m, precision=hp)
    pr_t, pi_t = jnp.moveaxis(pr, 1, 3), jnp.moveaxis(pi, 1, 3)
    cr = spread(jnp.swapaxes(c_re, 2, 3), spread_o)
    ci = spread(jnp.swapaxes(c_im, 2, 3), spread_o)

    def c_lam(k0):
        pk_r = spread(pr_t[..., k0:k0 + L], spread_k)
        pk_i = spread(pi_t[..., k0:k0 + L], spread_k)
        return cr * pk_r - ci * pk_i, cr * pk_i + ci * pk_r

    bbr_t, bbi_t = jnp.swapaxes(bbr, 2, 3), jnp.swapaxes(bbi, 2, 3)
    cl0r, cl0i = c_lam(0)
    kmat = (jnp.einsum('dgip,dgpn->dgin', bbr_t, cl0r, precision=hp)
            - jnp.einsum('dgip,dgpn->dgin', bbi_t, cl0i, precision=hp))
    kc = kmat.reshape(n_dir, nb, gb * hch, L * hch)
    dup = lambda x: jnp.concatenate([x, x], axis=-1)
    pw_r, pw_i = dup(pr[:, L - 1::-1]), dup(pi[:, L - 1::-1])
    bb = jnp.concatenate([bbr_t, bbi_t], axis=-1)
    bb_rot = jnp.concatenate([-bbi_t, bbr_t], axis=-1)
    e = pw_r[:, :, :, None, :] * bb[:, None] + pw_i[:, :, :, None, :] * bb_rot[:, None]
    ec = jnp.moveaxis(e.reshape(n_dir, L, nb, gb * hch, 2 * p), 1, 2)
    cl1r, cl1i = c_lam(1)
    cl = jnp.stack([cl1r, -cl1i], axis=1).reshape(n_dir, 2, nb, gb * p, L * hch)
    cl = jnp.moveaxis(cl, 1, 2).reshape(n_dir, nb, 2 * gb * p, L * hch)
    a = jnp.stack([pr[:, L].reshape(n_dir, nb, 1, gb * p), pi[:, L].reshape(n_dir, nb, 1, gb * p)], axis=3)
    return kc.astype(BF16), ec.astype(BF16), cl.astype(BF16), a.reshape(n_dir, nb, 1, 2 * gb * p)


def _replicators():
    gb = LANES // S5_GROUP
    r16 = np.zeros((S5_CHUNK, S5_GROUP, S5_CHUNK, gb, S5_GROUP), np.float32)
    for k in range(S5_CHUNK):
        for o in range(S5_GROUP):
            r16[k, o, k, :, o] = 1.0
    r64 = np.zeros((2, S5_STATE, 2, gb, S5_STATE), np.float32)
    for part in range(2):
        for q in range(S5_STATE):
            r64[part, q, part, :, q] = 1.0
    return (jnp.asarray(r16.reshape(S5_CHUNK * S5_GROUP, S5_CHUNK * LANES), BF16),
            jnp.asarray(r64.reshape(2 * S5_STATE, 2 * gb * S5_STATE), BF16))


def _s5_scan_kernel(ul_ref, uc_ref, kc_ref, ec_ref, cl_ref, r16_ref, r64_ref, a_ref, yl_ref, yc_ref,
                    h_scr, s_scr, hs_scr, wts_scr, wy_scr, yt_scr):
    d = pl.program_id(1)
    t = pl.program_id(2)
    L = S5_CHUNK
    nstate = a_ref.shape[-1]
    half = nstate // 2
    nt = L * LANES
    blk = lambda i: slice(i * LANES, (i + 1) * LANES)

    def build_weights():
        row_gi = lax.broadcasted_iota(jnp.int32, (LANES, LANES), 0) // S5_GROUP
        lane_go = lax.broadcasted_iota(jnp.int32, (LANES, LANES), 1) // S5_GROUP
        toep = _dot(kc_ref[0, 0], r16_ref[...])
        zero = jnp.zeros((LANES, LANES), BF16)
        for lag in range(L):
            piece = jnp.where(row_gi == lane_go, toep[:, blk(lag)], 0.0).astype(BF16)
            for s in range(L - lag):
                wts_scr[blk(s), blk(s + lag)] = piece
        for tt in range(0, L, 2):
            wts_scr[blk(tt + 1), blk(tt)] = zero
        row_gi_w = lax.broadcasted_iota(jnp.int32, (LANES, nstate), 0) // S5_GROUP
        lane_gp = (lax.broadcasted_iota(jnp.int32, (LANES, nstate), 1) % half) // S5_STATE
        for s in range(L):
            e = _dot(ec_ref[0, 0, s], r64_ref[...])
            wts_scr[blk(s), nt:] = jnp.where(row_gi_w == lane_gp, e, 0.0).astype(BF16)
        row_gp = (lax.broadcasted_iota(jnp.int32, (nstate, LANES), 0) % half) // S5_STATE
        lane_go_t = lax.broadcasted_iota(jnp.int32, (nstate, LANES), 1) // S5_GROUP
        for tt in range(L):
            w = _dot(cl_ref[0, 0], r16_ref[:, blk(tt)])
            wy_scr[:, blk(tt)] = jnp.where(row_gp == lane_go_t, w, 0.0).astype(BF16)

    def process(x_ref, y_ref):
        bsz, tok, _ = x_ref.shape
        nc = tok // L
        m = bsz * nc
        flip = lambda i, n: i + d * (n - 1 - 2 * i)
        u = jnp.concatenate(
            [x_ref[:, pl.ds(flip(s, L), nc, stride=L), :].reshape(m, LANES) for s in range(L)], axis=1).astype(BF16)
        s_all = _dot(u, wts_scr[:, nt:])
        nq = nstate // LANES
        nh = nq // 2
        ncp = nc + SUBLANES
        for q in range(nq):
            for b in range(bsz):
                s_scr[q, b * ncp:b * ncp + nc, :] = s_all[b * nc:(b + 1) * nc, blk(q)]
        pair = lambda n: slice(2 * n * LANES, 2 * (n + 1) * LANES)
        for n in range(L // 2):
            yt_scr[0:m, pair(n)] = _dot(u[:, :pair(n).stop], wts_scr[:pair(n).stop, pair(n)])
        a = a_ref[0, 0]
        a_re = [jnp.broadcast_to(a[:, blk(q)], (bsz, LANES)) for q in range(nh)]
        a_im = [jnp.broadcast_to(a[:, blk(nh + q)], (bsz, LANES)) for q in range(nh)]
        h_re = [h_scr[q] for q in range(nh)]
        h_im = [h_scr[nh + q] for q in range(nh)]
        for c in range(nc):
            rows = pl.ds(flip(c, nc), bsz, stride=ncp)
            for q in range(nh):
                hs_scr.at[q][rows, :] = h_re[q]
                hs_scr.at[nh + q][rows, :] = h_im[q]
                s_re = s_scr.at[q][rows, :]
                s_im = s_scr.at[nh + q][rows, :]
                h_re[q], h_im[q] = (a_re[q] * h_re[q] - a_im[q] * h_im[q] + s_re,
                                    a_re[q] * h_im[q] + a_im[q] * h_re[q] + s_im)
        for q in range(nh):
            h_scr[q] = h_re[q]
            h_scr[nh + q] = h_im[q]
        hs = jnp.concatenate(
            [jnp.concatenate([hs_scr[q, b * ncp:b * ncp + nc, :] for b in range(bsz)], axis=0) for q in range(nq)],
            axis=1).astype(BF16)
        for n in range(L // 2):
            y = yt_scr[0:m, pair(n)] + _dot(hs, wy_scr[:, pair(n)])
            for tt in (2 * n, 2 * n + 1):
                y_ref[:, pl.ds(flip(tt, L), nc, stride=L), :] = (
                    y[:, blk(tt - 2 * n)].reshape(bsz, nc, LANES))

    @pl.when(t == 0)
    def _():
        build_weights()
        h_scr[...] = jnp.zeros_like(h_scr)
        process(uc_ref, yc_ref)

    @pl.when(t > 0)
    def _():
        process(ul_ref, yl_ref)


def _s5_scan(u3, kc, ec, cl, a, layer, seq, ctx_len):
    bsz, _, d = u3.shape
    tl = 1024
    ntl = seq // tl
    nb = d // LANES
    n_dir = 2
    ctx_blk = seq // ctx_len
    nstate = a.shape[-1]
    nt = S5_CHUNK * LANES
    r16, r64 = _replicators()

    def lat_idx(dd, t):
        i = jnp.maximum(t - 1, 0)
        return jnp.where(dd == 0, i, ntl - 1 - i)

    per_block = lambda arr: pl.BlockSpec((None, 1, 1) + arr.shape[3:],
                                         lambda j, dd, t: (layer, dd, j) + (0,) * (arr.ndim - 3))
    const = lambda arr: pl.BlockSpec(arr.shape, lambda j, dd, t: (0,) * arr.ndim)
    return pl.pallas_call(
        _s5_scan_kernel,
        out_shape=(jax.ShapeDtypeStruct((n_dir, bsz, seq, d), F32),
                   jax.ShapeDtypeStruct((n_dir, bsz, ctx_len, d), F32)),
        grid=(nb, n_dir, ntl + 1),
        in_specs=[
            pl.BlockSpec((bsz, tl, LANES), lambda j, dd, t: (0, lat_idx(dd, t), j)),
            pl.BlockSpec((bsz, ctx_len, LANES), lambda j, dd, t: (0, ctx_blk, j)),
            per_block(kc), per_block(ec), per_block(cl), const(r16), const(r64), per_block(a),
        ],
        out_specs=(
            pl.BlockSpec((None, bsz, tl, LANES), lambda j, dd, t: (dd, 0, lat_idx(dd, t), j)),
            pl.BlockSpec((None, bsz, ctx_len, LANES), lambda j, dd, t: (dd, 0, 0, j)),
        ),
        scratch_shapes=[
            pltpu.VMEM((nstate // LANES, bsz, LANES), F32),
            pltpu.VMEM((nstate // LANES, bsz * (tl // S5_CHUNK + SUBLANES), LANES), F32),
            pltpu.VMEM((nstate // LANES, bsz * (tl // S5_CHUNK + SUBLANES), LANES), F32),
            pltpu.VMEM((nt, nt + nstate), BF16),
            pltpu.VMEM((nstate, nt), BF16),
            pltpu.VMEM((bsz * tl // S5_CHUNK, nt), F32),
        ],
        compiler_params=_cparams(("arbitrary", "arbitrary", "arbitrary")),
        name="s5_scan",
    )(u3, u3, kc, ec, cl, r16, r64, a)


def _gelu_tanh(x):
    return 0.5 * x * (1.0 + jnp.tanh(math.sqrt(2.0 / math.pi) * (x + 0.044715 * (x * x * x))))


def _s5_glu_kernel(h_ref, yf_ref, yr_ref, mod_ref, dvec_ref, w_ref, b_ref, *rest):
    o_ref = rest[-1]
    h = h_ref[...]
    mod = mod_ref[0]
    y = dvec_ref[...] * _normmod(h, mod[GS_M:GS_M + 1], mod[SH_M:SH_M + 1]) + (yf_ref[...] + yr_ref[...])
    z = _gelu_tanh(y)
    gate = 1.0 / (1.0 + jnp.exp(-(_dot(z.astype(BF16), w_ref[...]) + b_ref[...])))
    o_ref[...] = h + mod[G_M:G_M + 1] * (z * gate)


def _s5_glu(lat_src, ctx_src, y_lat, y_ctx, mod4, dvec, glu_w, glu_b, seq):
    bsz, tps, _, d = mod4.shape
    ctx_len = y_ctx.shape[2]
    tot = seq + ctx_len
    wb = (dvec.reshape(1, d), glu_w.astype(BF16), glu_b.reshape(1, d))
    vec2 = lambda *_: (0, 0)
    wb_specs = [pl.BlockSpec((1, d), vec2), pl.BlockSpec((d, d), vec2), pl.BlockSpec((1, d), vec2)]
    lat_arr, _ = lat_src
    tm = ROW_BLOCK
    part = pl.pallas_call(
        _s5_glu_kernel,
        out_shape=jax.ShapeDtypeStruct((bsz, tot, d), F32),
        grid=(bsz, seq // tm),
        in_specs=[pl.BlockSpec((None, tm, d), lambda b, m: (b, m, 0)),
                  pl.BlockSpec((None, None, tm, d), lambda b, m: (0, b, m, 0)),
                  pl.BlockSpec((None, None, tm, d), lambda b, m: (1, b, m, 0)),
                  pl.BlockSpec((None, 1, MOD_ROWS, d), lambda b, m: (b, 0, 0, 0))] + wb_specs,
        out_specs=pl.BlockSpec((None, tm, d), lambda b, m: (b, m, 0)),
        compiler_params=_cparams(("arbitrary", "arbitrary")),
        name="s5_glu_latent",
    )(lat_arr, y_lat, y_lat, mod4, *wb)
    ctx_arr, ctx_blk = ctx_src
    out_blk = seq // ctx_len
    return pl.pallas_call(
        _s5_glu_kernel,
        out_shape=jax.ShapeDtypeStruct((bsz, tot, d), F32),
        grid=(bsz,),
        in_specs=[pl.BlockSpec((None, ctx_len, d), lambda b: (b, ctx_blk, 0)),
                  pl.BlockSpec((None, None, ctx_len, d), lambda b: (0, b, 0, 0)),
                  pl.BlockSpec((None, None, ctx_len, d), lambda b: (1, b, 0, 0)),
                  pl.BlockSpec((None, 1, MOD_ROWS, d), lambda b: (b, tps - 1, 0, 0))] + wb_specs + [
                  pl.BlockSpec(memory_space=pl.ANY)],
        out_specs=pl.BlockSpec((None, ctx_len, d), lambda b: (b, out_blk, 0)),
        input_output_aliases={7: 0},
        compiler_params=_cparams(("arbitrary",)),
        name="s5_glu_context",
    )(ctx_arr, y_ctx, y_ctx, mod4, *wb, part)


def _pool_matrices(width, tile):
    mats = np.zeros((len(POOL_WINDOWS), tile, tile), np.float32)
    pos = np.arange(width)
    for g, win in enumerate(POOL_WINDOWS):
        lo = np.clip(pos - win // 2, 0, width)
        hi = np.clip(pos + win // 2, 0, width)
        for base in range(0, tile, width):
            for j in range(width):
                mats[g, base + j, base + lo[j]:base + hi[j]] = 1.0 / float(hi[j] - lo[j])
    return mats


def _pool_kernel(tps, h_ref, mod_ref, a_ref, w_ref, b_ref, sc_ref, o_ref):
    gw = w_ref.shape[-1]
    nsub = h_ref.shape[0] // TILE
    for j in range(nsub):
        rows = slice(j * TILE, (j + 1) * TILE)
        mod = mod_ref[j]
        is_ctx = (pl.program_id(0) * nsub + j) % tps == tps - 1
        h = h_ref[rows, :]
        u = _normmod(h, mod[GS_M:GS_M + 1], mod[SH_M:SH_M + 1])
        ub = u.astype(BF16)
        for g in range(w_ref.shape[0]):
            cols = slice(g * gw, (g + 1) * gw)
            avg = jnp.where(is_ctx, a_ref[1, g], a_ref[0, g])
            res = _dot(avg, ub[:, cols]) - u[:, cols]
            mixed = _dot(res.astype(BF16), w_ref[g]) + b_ref[:, cols]
            o_ref[rows, cols] = h[:, cols] + mod[G_M:G_M + 1, cols] * (mixed * sc_ref[:, cols])


def _pool(h, mod, pool_w, pool_b, pool_scale, seq):
    r, d = h.shape
    tps = seq // TILE + 1
    ng, gw, _ = pool_w.shape
    mats = jnp.asarray(np.stack([_pool_matrices(GRID_W, TILE), _pool_matrices(TILE, TILE)]), BF16)
    row = pl.BlockSpec((ROW_BLOCK, d), lambda i: (i, 0))
    vec = pl.BlockSpec((1, d), lambda i: (0, 0))
    return pl.pallas_call(
        functools.partial(_pool_kernel, tps),
        out_shape=jax.ShapeDtypeStruct((r, d), F32),
        grid=(r // ROW_BLOCK,),
        in_specs=[row, pl.BlockSpec((ROW_BLOCK // TILE, MOD_ROWS, d), lambda i: (i, 0, 0)),
                  pl.BlockSpec((2, ng, TILE, TILE), lambda i: (0, 0, 0, 0)),
                  pl.BlockSpec((ng, gw, gw), lambda i: (0, 0, 0)), vec, vec],
        out_specs=row,
        compiler_params=_cparams(("arbitrary",)),
        name="pool_mixer",
    )(h, mod, mats, pool_w.astype(BF16), pool_b.reshape(1, d), pool_scale.reshape(1, d))


def _dft_tables(n):
    idx = np.arange(n)
    ang = 2.0 * np.pi * ((idx[:, None] * idx[None, :]) % n) / n
    scale = 1.0 / math.sqrt(n)
    return (np.cos(ang) * scale).astype(np.float32), (np.sin(ang) * scale).astype(np.float32)


def _fnet_fold_kernel(cc_ref, sc_ref, w_ref, o_ref):
    w = w_ref[...]
    d = w.shape[-1]
    o_ref[:, :d] = _dot3(cc_ref[...], w).astype(o_ref.dtype)
    o_ref[:, d:] = (-_dot3(sc_ref[...], w)).astype(o_ref.dtype)


def _fnet_fold(fnet_w):
    d = fnet_w.shape[0]
    gw = d // FNET_GROUPS
    cc, sc = _dft_tables(gw)
    sq = pl.BlockSpec((gw, gw), lambda i: (0, 0))
    return pl.pallas_call(
        _fnet_fold_kernel,
        out_shape=jax.ShapeDtypeStruct((d, 2 * d), BF16),
        grid=(FNET_GROUPS,),
        in_specs=[sq, sq, pl.BlockSpec((gw, d), lambda i: (i, 0))],
        out_specs=pl.BlockSpec((gw, 2 * d), lambda i: (i, 0)),
        compiler_params=_cparams(("arbitrary",)),
        name="fnet_fold",
    )(jnp.asarray(cc), jnp.asarray(sc), fnet_w)


def _fnet_proj_kernel(h_ref, mod_ref, g_ref, v_ref):
    for j in range(h_ref.shape[0] // TILE):
        rows = slice(j * TILE, (j + 1) * TILE)
        mod = mod_ref[j]
        u = _normmod(h_ref[rows, :], mod[GS_M:GS_M + 1], mod[SH_M:SH_M + 1])
        v_ref[rows, :] = _dot(u.astype(BF16), g_ref[...]).astype(v_ref.dtype)


def _fnet_proj(h, mod, g):
    r, d = h.shape
    sub = ROW_BLOCK // TILE
    return pl.pallas_call(
        _fnet_proj_kernel,
        out_shape=jax.ShapeDtypeStruct((r, 2 * d), BF16),
        grid=(r // ROW_BLOCK,),
        in_specs=[pl.BlockSpec((ROW_BLOCK, d), lambda i: (i, 0)),
                  pl.BlockSpec((sub, MOD_ROWS, d), lambda i: (i, 0, 0)),
                  pl.BlockSpec((d, 2 * d), lambda i: (0, 0))],
        out_specs=pl.BlockSpec((ROW_BLOCK, 2 * d), lambda i: (i, 0)),
        compiler_params=_cparams(("arbitrary",)),
        name="fnet_proj",
    )(h, mod, g)


def _dft_matrix(n):
    n1 = int(round(math.sqrt(n)))
    assert n1 * n1 == n
    hi = jnp.arange(n1, dtype=jnp.int32)[:, None]
    k = jnp.arange(n, dtype=jnp.int32)[None, :]
    ang_a = (2.0 * math.pi / n1) * ((hi * k) % n1).astype(F32)
    ang_b = (2.0 * math.pi / n) * ((hi * k) % n).astype(F32)
    ca, sa, cb, sb = jnp.cos(ang_a), jnp.sin(ang_a), jnp.cos(ang_b), jnp.sin(ang_b)
    scale = 1.0 / math.sqrt(n)
    c = (ca[:, None] * cb[None] - sa[:, None] * sb[None]).reshape(n, n) * scale
    s = (sa[:, None] * cb[None] + ca[:, None] * sb[None]).reshape(n, n) * scale
    return jnp.concatenate([c, s], axis=1).astype(BF16)


def _fnet_lat_kernel(f_ref, v_ref, h_ref, mod_ref, b_ref, o_ref, acc_ref):
    k = pl.program_id(2)

    @pl.when(k == 0)
    def _():
        acc_ref[...] = jnp.zeros_like(acc_ref)

    acc_ref[...] += _dot(f_ref[...], v_ref[0])

    @pl.when(k == pl.num_programs(2) - 1)
    def _():
        for j in range(h_ref.shape[1] // TILE):
            rows = slice(j * TILE, (j + 1) * TILE)
            o_ref[0, rows, :] = h_ref[0, rows, :] + mod_ref[0, j, G_M:G_M + 1] * (acc_ref[rows, :] + b_ref[...])


def _fnet_lat(dft, v3, h3, mod4, fnet_b, seq):
    bsz, tot, d = h3.shape
    tm, tk = ROW_BLOCK, min(2048, seq)
    kh = seq // tk
    sub = tm // TILE
    return pl.pallas_call(
        _fnet_lat_kernel,
        out_shape=jax.ShapeDtypeStruct((bsz, tot, d), F32),
        grid=(bsz, seq // tm, 2 * kh),
        in_specs=[pl.BlockSpec((tm, tk), lambda b, m, k: (m, k)),
                  pl.BlockSpec((1, tk, d), lambda b, m, k: (b, k % kh, k // kh)),
                  pl.BlockSpec((1, tm, d), lambda b, m, k: (b, m, 0)),
                  pl.BlockSpec((1, sub, MOD_ROWS, d), lambda b, m, k: (b, m, 0, 0)),
                  pl.BlockSpec((1, d), lambda b, m, k: (0, 0))],
        out_specs=pl.BlockSpec((1, tm, d), lambda b, m, k: (b, m, 0)),
        scratch_shapes=[pltpu.VMEM((tm, d), F32)],
        compiler_params=_cparams(("arbitrary", "arbitrary", "arbitrary")),
        name="fnet_dft_latent",
    )(dft, v3, h3, mod4, fnet_b.reshape(1, d))


def _fnet_ctx_kernel(f_ref, v_ref, h_ref, mod_ref, b_ref, prev_ref, o_ref):
    del prev_ref
    n = f_ref.shape[0]
    d = h_ref.shape[-1]
    acc = _dot(f_ref[:, :n], v_ref[0, :, :d]) + _dot(f_ref[:, n:], v_ref[0, :, d:])
    o_ref[0] = h_ref[0] + mod_ref[0, 0, G_M:G_M + 1] * (acc + b_ref[...])


def _fnet_ctx(dftc, v3, h3, mod4, fnet_b, partial, seq):
    bsz, tot, d = h3.shape
    n = tot - seq
    blk = seq // n
    return pl.pallas_call(
        _fnet_ctx_kernel,
        out_shape=jax.ShapeDtypeStruct((bsz, tot, d), F32),
        grid=(bsz,),
        in_specs=[pl.BlockSpec((n, 2 * n), lambda b: (0, 0)),
                  pl.BlockSpec((1, n, 2 * d), lambda b: (b, blk, 0)),
                  pl.BlockSpec((1, n, d), lambda b: (b, blk, 0)),
                  pl.BlockSpec((1, 1, MOD_ROWS, d), lambda b: (b, blk, 0, 0)),
                  pl.BlockSpec((1, d), lambda b: (0, 0)),
                  pl.BlockSpec(memory_space=pl.ANY)],
        out_specs=pl.BlockSpec((1, n, d), lambda b: (b, blk, 0)),
        input_output_aliases={5: 0},
        compiler_params=_cparams(("arbitrary",)),
        name="fnet_dft_context",
    )(dftc, v3, h3, mod4, fnet_b.reshape(1, d), partial)


def _silu(x):
    return x * (1.0 / (1.0 + jnp.exp(-x)))


def _ffn_kernel(h_ref, mod_ref, wg_ref, wu_ref, wd_ref, o_ref, u_scr, acc_ref):
    f = pl.program_id(1)
    sub = h_ref.shape[0] // TILE

    @pl.when(f == 0)
    def _():
        for j in range(sub):
            rows = slice(j * TILE, (j + 1) * TILE)
            mod = mod_ref[j]
            u_scr[rows, :] = _normmod(h_ref[rows, :], mod[GS_F:GS_F + 1], mod[SH_F:SH_F + 1]).astype(BF16)
        acc_ref[...] = jnp.zeros_like(acc_ref)

    u = u_scr[...]
    a = _silu(_dot(u, wg_ref[...])) * _dot(u, wu_ref[...])
    acc_ref[...] += _dot(a.astype(BF16), wd_ref[...])

    @pl.when(f == pl.num_programs(1) - 1)
    def _():
        for j in range(sub):
            rows = slice(j * TILE, (j + 1) * TILE)
            o_ref[rows, :] = h_ref[rows, :] + mod_ref[j, G_F:G_F + 1] * acc_ref[rows, :]


def _ffn(h, mod, w_gate, w_up, w_down):
    r, d = h.shape
    ff = w_gate.shape[1]
    tm, tf = ROW_BLOCK, 256
    sub = tm // TILE
    return pl.pallas_call(
        _ffn_kernel,
        out_shape=jax.ShapeDtypeStruct((r, d), F32),
        grid=(r // tm, ff // tf),
        in_specs=[pl.BlockSpec((tm, d), lambda i, f: (i, 0)),
                  pl.BlockSpec((sub, MOD_ROWS, d), lambda i, f: (i, 0, 0)),
                  pl.BlockSpec((d, tf), lambda i, f: (0, f)),
                  pl.BlockSpec((d, tf), lambda i, f: (0, f)),
                  pl.BlockSpec((tf, d), lambda i, f: (f, 0))],
        out_specs=pl.BlockSpec((tm, d), lambda i, f: (i, 0)),
        scratch_shapes=[pltpu.VMEM((tm, d), BF16), pltpu.VMEM((tm, d), F32)],
        compiler_params=_cparams(("arbitrary", "arbitrary")),
        name="ffn_swiglu",
    )(h, mod, w_gate.astype(BF16), w_up.astype(BF16), w_down.astype(BF16))


MOE_TILE = 2048
MOE_CHUNK = 128


def _router_kernel(tiles_per_seq, route_ctx, h_ref, mod_ref, r_ref, u_ref, cw_ref, slot_ref, slott_ref, cnt_ref):
    nsub = h_ref.shape[0] // TILE
    lane = lax.broadcasted_iota(jnp.int32, (TILE, LANES), 1).astype(F32)
    earlier = jnp.where(lax.broadcasted_iota(jnp.int32, (TILE, TILE), 1)
                        < lax.broadcasted_iota(jnp.int32, (TILE, TILE), 0), 1.0, 0.0).astype(BF16)
    neg = jnp.float32(-jnp.inf)
    count = jnp.zeros((1, LANES), F32)
    for j in range(nsub):
        rows = slice(j * TILE, (j + 1) * TILE)
        mod = mod_ref[j]
        u = _normmod(h_ref[rows, :], mod[GS_F:GS_F + 1], mod[SH_F:SH_F + 1])
        u_ref[rows, :] = u.astype(BF16)
        logits = _dot3(u, r_ref[...])
        logits = jnp.where(lane < N_EXPERTS, logits, neg)
        m1 = jnp.max(logits, axis=-1, keepdims=True)
        i1 = jnp.min(jnp.where(logits == m1, lane, float(LANES)), axis=-1, keepdims=True)
        rest = jnp.where(lane == i1, neg, logits)
        m2 = jnp.max(rest, axis=-1, keepdims=True)
        i2 = jnp.min(jnp.where(rest == m2, lane, float(LANES)), axis=-1, keepdims=True)
        e2 = jnp.exp(m2 - m1)
        w1 = 1.0 / (1.0 + e2)
        w2 = e2 / (1.0 + e2)
        cw = jnp.where(lane == i1, w1, jnp.where(lane == i2, w2, 0.0))
        sel = jnp.where(lane == i1, 1.0, jnp.where(lane == i2, 1.0, 0.0))
        if not route_ctx:
            is_ctx = (pl.program_id(0) * nsub + j) % tiles_per_seq == tiles_per_seq - 1
            cw = jnp.where(is_ctx, 0.0, cw)
            sel = jnp.where(is_ctx, 0.0, sel)
        cw_ref[rows, :] = cw
        slot = jnp.where(sel > 0.0, _dot(earlier, sel.astype(BF16)) + count, -1.0)
        slot_ref[rows, :] = slot
        slott_ref[:, rows] = slot.T[:N_EXPERTS, :]
        count = count + jnp.sum(sel, axis=0, keepdims=True)
    cnt_ref[0] = jnp.broadcast_to(count, (N_EXPERTS, LANES))


def _router(h, mod, router, tiles_per_seq, route_ctx):
    r, d = h.shape
    rp = jnp.zeros((d, LANES), F32).at[:, :N_EXPERTS].set(router)
    sub = MOE_TILE // TILE
    row = lambda w: pl.BlockSpec((MOE_TILE, w), lambda i: (i, 0))
    return pl.pallas_call(
        functools.partial(_router_kernel, tiles_per_seq, route_ctx),
        out_shape=(jax.ShapeDtypeStruct((r, d), BF16), jax.ShapeDtypeStruct((r, LANES), F32),
                   jax.ShapeDtypeStruct((r, LANES), F32), jax.ShapeDtypeStruct((N_EXPERTS, r), F32),
                   jax.ShapeDtypeStruct((r // MOE_TILE, N_EXPERTS, LANES), F32)),
        grid=(r // MOE_TILE,),
        in_specs=[row(d),
                  pl.BlockSpec((sub, MOD_ROWS, d), lambda i: (i, 0, 0)),
                  pl.BlockSpec((d, LANES), lambda i: (0, 0))],
        out_specs=(row(d), row(LANES), row(LANES),
                   pl.BlockSpec((N_EXPERTS, MOE_TILE), lambda i: (0, i)),
                   pl.BlockSpec((1, N_EXPERTS, LANES), lambda i: (i, 0, 0))),
        compiler_params=_cparams(("arbitrary",)),
        name="moe_router",
    )(h, mod, rp)


def _moe_kernel(nch_ref, h_ref, u_ref, cw_ref, slot_ref, slott_ref, mod_ref, wg_ref, wu_ref, wd_ref, o_ref,
                xs_scr, y_scr):
    i = pl.program_id(0)
    e = pl.program_id(1)
    f = pl.program_id(2)
    last_f = pl.num_programs(2) - 1
    tile = u_ref.shape[0]
    nch = nch_ref[i * N_EXPERTS + e]
    block = 4 * MOE_CHUNK

    @pl.when((e == 0) & (f == 0))
    def _():
        o_ref[...] = jnp.zeros_like(o_ref)

    nblk = lax.shift_right_logical(nch, 2)
    tail2 = pl.multiple_of(nblk * block, MOE_CHUNK)
    tail1 = pl.multiple_of(tail2 + (nch & 2) * MOE_CHUNK, MOE_CHUNK)

    def run_block(r0, rn):
        rows = pl.ds(r0, rn)

        @pl.when(f == 0)
        def _():
            rid = (lax.broadcasted_iota(jnp.int32, (rn, 1), 0) + r0).astype(F32)
            pick = jnp.where(slott_ref[pl.ds(e, 1), :] == rid, 1.0, 0.0).astype(BF16)
            xs_scr[rows, :] = _dot(pick, u_ref[...]).astype(BF16)

        xs = xs_scr[rows, :]
        a = _silu(_dot(xs, wg_ref[0])) * _dot(xs, wu_ref[0])
        part = _dot(a.astype(BF16), wd_ref[0])

        @pl.when(f == 0)
        def _():
            y_scr[rows, :] = part

        @pl.when(f > 0)
        def _():
            y_scr[rows, :] += part

    def loop_blocks(fn):
        def body(b, carry):
            fn(pl.multiple_of(b * block, block), block)
            return carry
        lax.fori_loop(0, nblk, body, 0)
        pl.when((nch & 2) != 0)(lambda: fn(tail2, 2 * MOE_CHUNK))
        pl.when((nch & 1) != 0)(lambda: fn(tail1, MOE_CHUNK))

    loop_blocks(run_block)

    @pl.when((f == last_f) & (nch > 0))
    def _():
        mine = lax.broadcasted_iota(jnp.int32, (tile, LANES), 1) == e
        slot_col = jnp.sum(jnp.where(mine, slot_ref[...], 0.0), axis=-1, keepdims=True)
        cw_col = jnp.sum(jnp.where(mine, cw_ref[...], 0.0), axis=-1, keepdims=True)

        def scatter_block(r0, rn):
            cid = (lax.broadcasted_iota(jnp.int32, (1, rn), 1) + r0).astype(F32)
            put = jnp.where(slot_col == cid, 1.0, 0.0).astype(BF16)
            o_ref[...] += cw_col * _dot(put, y_scr[pl.ds(r0, rn), :].astype(BF16))

        loop_blocks(scatter_block)

    @pl.when((e == pl.num_programs(1) - 1) & (f == last_f))
    def _():
        for j in range(tile // TILE):
            rows = slice(j * TILE, (j + 1) * TILE)
            o_ref[rows, :] = h_ref[rows, :] + mod_ref[j, G_F:G_F + 1] * o_ref[rows, :]


def _moe(h, u, cw, slot, slott, counts, mod, w_gate, w_up, w_down, layer):
    r, d = h.shape
    _, ne, _, ff = w_gate.shape
    tm, tf = MOE_TILE, 512
    sub = tm // TILE
    nch = ((counts[:, 0, :ne].astype(jnp.int32) + (MOE_CHUNK - 1)) // MOE_CHUNK).reshape(-1)
    once = pl.Buffered(1)
    row = lambda w: pl.BlockSpec((tm, w), lambda i, e, f, n: (i, 0), pipeline_mode=once)
    return pl.pallas_call(
        _moe_kernel,
        out_shape=jax.ShapeDtypeStruct((r, d), F32),
        grid_spec=pltpu.PrefetchScalarGridSpec(
            num_scalar_prefetch=1,
            grid=(r // tm, ne, ff // tf),
            in_specs=[row(d), row(d), row(LANES), row(LANES),
                      pl.BlockSpec((ne, tm), lambda i, e, f, n: (0, i), pipeline_mode=once),
                      pl.BlockSpec((sub, MOD_ROWS, d), lambda i, e, f, n: (i, 0, 0)),
                      pl.BlockSpec((None, 1, d, tf), lambda i, e, f, n: (layer, e, 0, f)),
                      pl.BlockSpec((None, 1, d, tf), lambda i, e, f, n: (layer, e, 0, f)),
                      pl.BlockSpec((None, 1, tf, d), lambda i, e, f, n: (layer, e, f, 0))],
            out_specs=pl.BlockSpec((tm, d), lambda i, e, f, n: (i, 0), pipeline_mode=once),
            scratch_shapes=[pltpu.VMEM((tm, d), BF16), pltpu.VMEM((tm, d), F32)]),
        compiler_params=_cparams(("arbitrary", "arbitrary", "arbitrary")),
        name="moe_experts",
    )(nch, h, u, cw, slot, slott, mod, w_gate, w_up, w_down)


def _final_kernel(h_ref, g_ref, o_ref):
    h = h_ref[0]
    inv = lax.rsqrt(jnp.mean(h * h, axis=-1, keepdims=True) + EPS)
    o_ref[0] = h * inv * g_ref[...]


def _final_norm(h3, gain, seq):
    bsz, _, d = h3.shape
    return pl.pallas_call(
        _final_kernel,
        out_shape=jax.ShapeDtypeStruct((bsz, seq, d), F32),
        grid=(bsz, seq // ROW_BLOCK),
        in_specs=[pl.BlockSpec((1, ROW_BLOCK, d), lambda b, t: (b, t, 0)),
                  pl.BlockSpec((1, d), lambda b, t: (0, 0))],
        out_specs=pl.BlockSpec((1, ROW_BLOCK, d), lambda b, t: (b, t, 0)),
        compiler_params=_cparams(("arbitrary", "arbitrary")),
        name="final_norm",
    )(h3, gain.reshape(1, d))


def _tile_mods(mods, gain_mix, gain_ffn, bsz, lat_tiles):
    sh_m, sc_m, g_m, sh_f, sc_f, g_f = (mods[:, i] for i in range(N_MOD))
    rows = jnp.stack([gain_mix * (1.0 + sc_m), sh_m, g_m, gain_ffn * (1.0 + sc_f), sh_f, g_f,
                      jnp.zeros_like(g_f), jnp.zeros_like(g_f)], axis=1)
    lat = jnp.broadcast_to(rows[:bsz, None], (bsz, lat_tiles) + rows.shape[1:])
    ctx = jnp.broadcast_to(rows[bsz:, None], (bsz, 1) + rows.shape[1:])
    return jnp.concatenate([lat, ctx], axis=1).reshape(bsz * (lat_tiles + 1), MOD_ROWS, rows.shape[-1])


def kernel(x, c, ctx, c_ctx, ada_w, ada_b, norm_mix, norm_ffn, norm_final, s5_lambda_re, s5_lambda_im, s5_log_step, s5_b_re, s5_b_im, s5_c_re, s5_c_im, s5_d, s5_glu_w, s5_glu_b, pool_w, pool_b, pool_scale, fnet_w, fnet_b, ffn_w_gate, ffn_w_up, ffn_w_down, moe_router, moe_w_gate, moe_w_up, moe_w_down):
    bsz, seq, d = x.shape
    ctx_len = ctx.shape[1]
    depth = ada_w.shape[0]
    assert ctx_len == TILE and seq % ROW_BLOCK == 0 and d % LANES == 0
    tot = seq + ctx_len
    lat_tiles = seq // TILE
    tps = lat_tiles + 1
    r = bsz * tot
    assert r % ROW_BLOCK == 0

    cond = jnp.zeros((16, d), F32).at[:bsz].set(c).at[bsz].set(c_ctx)
    mods_all = _ada_all(cond, ada_w, ada_b)[:, :bsz + 1].reshape(depth, bsz + 1, N_MOD, d)

    moe_wg, moe_wu, moe_wd = (w.astype(BF16) for w in (moe_w_gate, moe_w_up, moe_w_down))
    mods_tiled = jax.vmap(lambda m, gm, gf: _tile_mods(m, gm, gf, bsz, lat_tiles))(mods_all, norm_mix, norm_ffn)
    s5_tables = jax.vmap(_s5_params)(s5_lambda_re, s5_lambda_im, s5_log_step, s5_b_re, s5_b_im, s5_c_re, s5_c_im)
    h = None
    for i in range(depth):
        kind, j = i % 3, i // 3
        mod = mods_tiled[i]
        if kind == 0:
            if h is None:
                src, lat_src, ctx_src = (x, ctx), (x, 0), (ctx, 0)
            else:
                h3 = h.reshape(bsz, tot, d)
                src, lat_src, ctx_src = (h,), (h3, 0), (h3, seq // ctx_len)
            u = _s5_pre(src, mod, tps)
            y_lat, y_ctx = _s5_scan(u.reshape(bsz, tot, d), *s5_tables, j, seq, ctx_len)
            h = _s5_glu(lat_src, ctx_src, y_lat, y_ctx, mod.reshape(bsz, tps, MOD_ROWS, d),
                        s5_d[j], s5_glu_w[j], s5_glu_b[j], seq).reshape(r, d)
        elif kind == 1:
            h = _pool(h, mod, pool_w[j], pool_b[j], pool_scale[j], seq)
        else:
            g = _fnet_fold(fnet_w[j])
            v3 = _fnet_proj(h, mod, g).reshape(bsz, tot, 2 * d)
            h3 = h.reshape(bsz, tot, d)
            mod4 = mod.reshape(bsz, tps, MOD_ROWS, d)
            part = _fnet_lat(_dft_matrix(seq), v3, h3, mod4, fnet_b[j], seq)
            cc, sc = _dft_tables(ctx_len)
            dftc = jnp.asarray(np.concatenate([cc, sc], axis=1), BF16)
            h = _fnet_ctx(dftc, v3, h3, mod4, fnet_b[j], part, seq).reshape(r, d)
        kk = i // 2
        if i % 2 == 0:
            h = _ffn(h, mod, ffn_w_gate[kk], ffn_w_up[kk], ffn_w_down[kk])
        else:
            ub, cw, slot, slott, counts = _router(h, mod, moe_router[kk], tps, route_ctx=i < depth - 1)
            h = _moe(h, ub, cw, slot, slott, counts, mod, moe_wg, moe_wu, moe_wd, kk)
    return _final_norm(h.reshape(bsz, tot, d), norm_final, seq)
```

```python
import functools
import math

import numpy as np
import jax
import jax.numpy as jnp
from jax import lax
from jax.experimental import pallas as pl
from jax.experimental.pallas import tpu as pltpu

F32 = jnp.float32
BF16 = jnp.bfloat16

EPS = 1e-6
GRID_W = 64
N_MOD = 6
S5_GROUP = 16
S5_STATE = 64
S5_CHUNK = 16
POOL_WINDOWS = (2, 4, 8, 16)
FNET_GROUPS = 4
N_EXPERTS = 8

LANES = 128
SUBLANES = 8
TILE = 256
ROW_BLOCK = 1024
VMEM_LIMIT = 56 * 1024 * 1024


def _cparams(sem, vmem=VMEM_LIMIT):
    return pltpu.CompilerParams(dimension_semantics=sem, vmem_limit_bytes=vmem)


def _dot(a, b):
    return jnp.dot(a, b, preferred_element_type=F32)


def _split(a):
    hi = a.astype(BF16)
    lo = (a - hi.astype(F32)).astype(BF16)
    return hi, lo


def _dot3(a, b):
    a_hi, a_lo = _split(a)
    b_hi, b_lo = _split(b)
    return _dot(a_hi, b_hi) + (_dot(a_lo, b_hi) + _dot(a_hi, b_lo))


def _normmod(h, gs, sh):
    inv = lax.rsqrt(jnp.mean(h * h, axis=-1, keepdims=True) + EPS)
    return h * inv * gs + sh


GS_M, SH_M, G_M, GS_F, SH_F, G_F = range(6)
MOD_ROWS = 8


def _ada_kernel(c_ref, w_ref, b_ref, o_ref):
    c = c_ref[...]
    s = c * (1.0 / (1.0 + jnp.exp(-c)))
    o_ref[0] = _dot3(s, w_ref[0]) + b_ref[0]


def _ada_all(cond, ada_w, ada_b):
    depth, d, n = ada_w.shape
    rows = cond.shape[0]
    tn = 1536
    return pl.pallas_call(
        _ada_kernel,
        out_shape=jax.ShapeDtypeStruct((depth, rows, n), F32),
        grid=(depth, n // tn),
        in_specs=[
            pl.BlockSpec((rows, d), lambda i, j: (0, 0)),
            pl.BlockSpec((1, d, tn), lambda i, j: (i, 0, j)),
            pl.BlockSpec((1, 1, tn), lambda i, j: (i, 0, j)),
        ],
        out_specs=pl.BlockSpec((1, rows, tn), lambda i, j: (i, 0, j)),
        compiler_params=_cparams(("arbitrary", "arbitrary")),
        name="ada_params",
    )(cond, ada_w, ada_b.reshape(depth, 1, n))


def _stream_specs(src, tps, d):
    if len(src) == 1:
        return [pl.BlockSpec((TILE, d), lambda i: (i, 0))]
    return [pl.BlockSpec((None, TILE, d), lambda i: (i // tps, jnp.minimum(i % tps, tps - 2), 0)),
            pl.BlockSpec((None, TILE, d), lambda i: (i // tps, 0, 0))]


def _stream_tile(refs, tps):
    if len(refs) == 1:
        return refs[0][...]
    is_ctx = (pl.program_id(0) % tps) == tps - 1
    return jnp.where(is_ctx, refs[1][...], refs[0][...])


def _s5_pre_kernel(n_src, tps, *refs):
    mod_ref, u_ref = refs[n_src:]
    if n_src == 1:
        for j in range(u_ref.shape[0] // TILE):
            rows = slice(j * TILE, (j + 1) * TILE)
            mod = mod_ref[j]
            u_ref[rows, :] = _normmod(refs[0][rows, :], mod[GS_M:GS_M + 1], mod[SH_M:SH_M + 1])
    else:
        mod = mod_ref[0]
        u_ref[...] = _normmod(_stream_tile(refs[:n_src], tps), mod[GS_M:GS_M + 1], mod[SH_M:SH_M + 1])


def _s5_pre_block_kernel(h_ref, mod_ref, *rest):
    mod = mod_ref[0]
    rest[-1][...] = _normmod(h_ref[...], mod[GS_M:GS_M + 1], mod[SH_M:SH_M + 1])


def _s5_pre_split(x, ctx, mod4):
    bsz, seq, d = x.shape
    ctx_len = ctx.shape[1]
    tps = mod4.shape[1]
    part = pl.pallas_call(
        _s5_pre_block_kernel,
        out_shape=jax.ShapeDtypeStruct((bsz, seq + ctx_len, d), F32),
        grid=(bsz, seq // ROW_BLOCK),
        in_specs=[pl.BlockSpec((None, ROW_BLOCK, d), lambda b, m: (b, m, 0)),
                  pl.BlockSpec((None, 1, MOD_ROWS, d), lambda b, m: (b, 0, 0, 0))],
        out_specs=pl.BlockSpec((None, ROW_BLOCK, d), lambda b, m: (b, m, 0)),
        compiler_params=_cparams(("arbitrary", "arbitrary")),
        name="s5_pre_latent",
    )(x, mod4)
    return pl.pallas_call(
        _s5_pre_block_kernel,
        out_shape=jax.ShapeDtypeStruct((bsz, seq + ctx_len, d), F32),
        grid=(bsz,),
        in_specs=[pl.BlockSpec((None, ctx_len, d), lambda b: (b, 0, 0)),
                  pl.BlockSpec((None, 1, MOD_ROWS, d), lambda b: (b, tps - 1, 0, 0)),
                  pl.BlockSpec(memory_space=pl.ANY)],
        out_specs=pl.BlockSpec((None, ctx_len, d), lambda b: (b, seq // ctx_len, 0)),
        input_output_aliases={2: 0},
        compiler_params=_cparams(("arbitrary",)),
        name="s5_pre_context",
    )(ctx, mod4, part)


def _s5_pre(src, mod, tps):
    r, _, d = mod.shape
    r *= TILE
    if len(src) == 1:
        rows = ROW_BLOCK
        src_specs = [pl.BlockSpec((rows, d), lambda i: (i, 0))]
    else:
        rows = TILE
        src_specs = _stream_specs(src, tps, d)
    return pl.pallas_call(
        functools.partial(_s5_pre_kernel, len(src), tps),
        out_shape=jax.ShapeDtypeStruct((r, d), F32),
        grid=(r // rows,),
        in_specs=src_specs + [pl.BlockSpec((rows // TILE, MOD_ROWS, d), lambda i: (i, 0, 0))],
        out_specs=pl.BlockSpec((rows, d), lambda i: (i, 0)),
        compiler_params=_cparams(("arbitrary",)),
        name="s5_pre",
    )(*src, mod)


def _s5_params(lam_re, lam_im, log_step, b_re, b_im, c_re, c_im):
    hp = lax.Precision.HIGHEST
    L = S5_CHUNK
    n_dir, g, p = lam_re.shape
    gb = LANES // S5_GROUP
    nb = g // gb
    hch = S5_GROUP
    step = jnp.exp(log_step)[..., None]
    ar, ai = lam_re * step, lam_im * step
    k = jnp.arange(L + 1, dtype=F32)[None, :, None, None]
    mag = jnp.exp(ar[:, None] * k)
    ang = ai[:, None] * k
    pr, pi = mag * jnp.cos(ang), mag * jnp.sin(ang)
    lbr, lbi = pr[:, 1] - 1.0, pi[:, 1]
    den = lam_re * lam_re + lam_im * lam_im
    qr = (lbr * lam_re + lbi * lam_im) / den
    qi = (lbi * lam_re - lbr * lam_im) / den
    bbr = qr[..., None] * b_re - qi[..., None] * b_im
    bbi = qr[..., None] * b_im + qi[..., None] * b_re
    idx = np.arange(L * hch)
    spread_o = jnp.asarray((idx[None, :] % hch) == np.arange(hch)[:, None], F32)
    spread_k = jnp.asarray((idx[None, :] // hch) == np.arange(L)[:, None], F32)
    spread = lambda x, m: jnp.einsum('...a,an->...n', x, m, precision=hp)
    pr_t, pi_t = jnp.moveaxis(pr, 1, 3), jnp.moveaxis(pi, 1, 3)
    cr = spread(jnp.swapaxes(c_re, 2, 3), spread_o)
    ci = spread(jnp.swapaxes(c_im, 2, 3), spread_o)

    def c_lam(k0):
        pk_r = spread(pr_t[..., k0:k0 + L], spread_k)
        pk_i = spread(pi_t[..., k0:k0 + L], spread_k)
        return cr * pk_r - ci * pk_i, cr * pk_i + ci * pk_r

    bbr_t, bbi_t = jnp.swapaxes(bbr, 2, 3), jnp.swapaxes(bbi, 2, 3)
    cl0r, cl0i = c_lam(0)
    kmat = (jnp.einsum('dgip,dgpn->dgin', bbr_t, cl0r, precision=hp)
            - jnp.einsum('dgip,dgpn->dgin', bbi_t, cl0i, precision=hp))
    kc = kmat.reshape(n_dir, nb, gb * hch, L * hch)
    dup = lambda x: jnp.concatenate([x, x], axis=-1)
    pw_r, pw_i = dup(pr[:, L - 1::-1]), dup(pi[:, L - 1::-1])
    bb = jnp.concatenate([bbr_t, bbi_t], axis=-1)
    bb_rot = jnp.concatenate([-bbi_t, bbr_t], axis=-1)
    e = pw_r[:, :, :, None, :] * bb[:, None] + pw_i[:, :, :, None, :] * bb_rot[:, None]
    ec = jnp.moveaxis(e.reshape(n_dir, L, nb, gb * hch, 2 * p), 1, 2)
    cl1r, cl1i = c_lam(1)
    cl = jnp.stack([cl1r, -cl1i], axis=1).reshape(n_dir, 2, nb, gb * p, L * hch)
    cl = jnp.moveaxis(cl, 1, 2).reshape(n_dir, nb, 2 * gb * p, L * hch)
    a = jnp.stack([pr[:, L].reshape(n_dir, nb, 1, gb * p), pi[:, L].reshape(n_dir, nb, 1, gb * p)], axis=3)
    return kc.astype(BF16), ec.astype(BF16), cl.astype(BF16), a.reshape(n_dir, nb, 1, 2 * gb * p)


def _replicators():
    gb = LANES // S5_GROUP
    r16 = np.zeros((S5_CHUNK, S5_GROUP, S5_CHUNK, gb, S5_GROUP), np.float32)
    for k in range(S5_CHUNK):
        for o in range(S5_GROUP):
            r16[k, o, k, :, o] = 1.0
    r64 = np.zeros((2, S5_STATE, 2, gb, S5_STATE), np.float32)
    for part in range(2):
        for q in range(S5_STATE):
            r64[part, q, part, :, q] = 1.0
    return (jnp.asarray(r16.reshape(S5_CHUNK * S5_GROUP, S5_CHUNK * LANES), BF16),
            jnp.asarray(r64.reshape(2 * S5_STATE, 2 * gb * S5_STATE), BF16))


def _s5_scan_kernel(ul_ref, uc_ref, kc_ref, ec_ref, cl_ref, r16_ref, r64_ref, a_ref, yl_ref, yc_ref,
                    h_scr, s_scr, hs_scr, wts_scr, wy_scr, yt_scr):
    d = pl.program_id(1)
    t = pl.program_id(2)
    L = S5_CHUNK
    nstate = a_ref.shape[-1]
    half = nstate // 2
    nt = L * LANES
    blk = lambda i: slice(i * LANES, (i + 1) * LANES)

    def build_weights():
        row_gi = lax.broadcasted_iota(jnp.int32, (LANES, LANES), 0) // S5_GROUP
        lane_go = lax.broadcasted_iota(jnp.int32, (LANES, LANES), 1) // S5_GROUP
        toep = _dot(kc_ref[0, 0], r16_ref[...])
        zero = jnp.zeros((LANES, LANES), BF16)
        for lag in range(L):
            piece = jnp.where(row_gi == lane_go, toep[:, blk(lag)], 0.0).astype(BF16)
            for s in range(L - lag):
                wts_scr[blk(s), blk(s + lag)] = piece
        for tt in range(0, L, 2):
            wts_scr[blk(tt + 1), blk(tt)] = zero
        row_gi_w = lax.broadcasted_iota(jnp.int32, (LANES, nstate), 0) // S5_GROUP
        lane_gp = (lax.broadcasted_iota(jnp.int32, (LANES, nstate), 1) % half) // S5_STATE
        for s in range(L):
            e = _dot(ec_ref[0, 0, s], r64_ref[...])
            wts_scr[blk(s), nt:] = jnp.where(row_gi_w == lane_gp, e, 0.0).astype(BF16)
        row_gp = (lax.broadcasted_iota(jnp.int32, (nstate, LANES), 0) % half) // S5_STATE
        lane_go_t = lax.broadcasted_iota(jnp.int32, (nstate, LANES), 1) // S5_GROUP
        for tt in range(L):
            w = _dot(cl_ref[0, 0], r16_ref[:, blk(tt)])
            wy_scr[:, blk(tt)] = jnp.where(row_gp == lane_go_t, w, 0.0).astype(BF16)

    def process(x_ref, y_ref):
        bsz, tok, _ = x_ref.shape
        nc = tok // L
        m = bsz * nc
        flip = lambda i, n: i + d * (n - 1 - 2 * i)
        u = jnp.concatenate(
            [x_ref[:, pl.ds(flip(s, L), nc, stride=L), :].reshape(m, LANES) for s in range(L)], axis=1).astype(BF16)
        s_all = _dot(u, wts_scr[:, nt:])
        nq = nstate // LANES
        nh = nq // 2
        ncp = nc + SUBLANES
        for q in range(nq):
            for b in range(bsz):
                s_scr[q, b * ncp:b * ncp + nc, :] = s_all[b * nc:(b + 1) * nc, blk(q)]
        pair = lambda n: slice(2 * n * LANES, 2 * (n + 1) * LANES)
        for n in range(L // 2):
            yt_scr[0:m, pair(n)] = _dot(u[:, :pair(n).stop], wts_scr[:pair(n).stop, pair(n)])
        a = a_ref[0, 0]
        a_re = [jnp.broadcast_to(a[:, blk(q)], (bsz, LANES)) for q in range(nh)]
        a_im = [jnp.broadcast_to(a[:, blk(nh + q)], (bsz, LANES)) for q in range(nh)]
        h_re = [h_scr[q] for q in range(nh)]
        h_im = [h_scr[nh + q] for q in range(nh)]
        for c in range(nc):
            rows = pl.ds(flip(c, nc), bsz, stride=ncp)
            for q in range(nh):
                hs_scr.at[q][rows, :] = h_re[q]
                hs_scr.at[nh + q][rows, :] = h_im[q]
                s_re = s_scr.at[q][rows, :]
                s_im = s_scr.at[nh + q][rows, :]
                h_re[q], h_im[q] = (a_re[q] * h_re[q] - a_im[q] * h_im[q] + s_re,
                                    a_re[q] * h_im[q] + a_im[q] * h_re[q] + s_im)
        for q in range(nh):
            h_scr[q] = h_re[q]
            h_scr[nh + q] = h_im[q]
        hs = jnp.concatenate(
            [jnp.concatenate([hs_scr[q, b * ncp:b * ncp + nc, :] for b in range(bsz)], axis=0) for q in range(nq)],
            axis=1).astype(BF16)
        for n in range(L // 2):
            y = yt_scr[0:m, pair(n)] + _dot(hs, wy_scr[:, pair(n)])
            for tt in (2 * n, 2 * n + 1):
                y_ref[:, pl.ds(flip(tt, L), nc, stride=L), :] = (
                    y[:, blk(tt - 2 * n)].reshape(bsz, nc, LANES))

    @pl.when(t == 0)
    def _():
        build_weights()
        h_scr[...] = jnp.zeros_like(h_scr)
        process(uc_ref, yc_ref)

    @pl.when(t > 0)
    def _():
        process(ul_ref, yl_ref)


def _s5_scan(u3, kc, ec, cl, a, layer, seq, ctx_len):
    bsz, _, d = u3.shape
    tl = 1024
    ntl = seq // tl
    nb = d // LANES
    n_dir = 2
    ctx_blk = seq // ctx_len
    nstate = a.shape[-1]
    nt = S5_CHUNK * LANES
    r16, r64 = _replicators()

    def lat_idx(dd, t):
        i = jnp.maximum(t - 1, 0)
        return jnp.where(dd == 0, i, ntl - 1 - i)

    per_block = lambda arr: pl.BlockSpec((None, 1, 1) + arr.shape[3:],
                                         lambda j, dd, t: (layer, dd, j) + (0,) * (arr.ndim - 3))
    const = lambda arr: pl.BlockSpec(arr.shape, lambda j, dd, t: (0,) * arr.ndim)
    return pl.pallas_call(
        _s5_scan_kernel,
        out_shape=(jax.ShapeDtypeStruct((n_dir, bsz, seq, d), F32),
                   jax.ShapeDtypeStruct((n_dir, bsz, ctx_len, d), F32)),
        grid=(nb, n_dir, ntl + 1),
        in_specs=[
            pl.BlockSpec((bsz, tl, LANES), lambda j, dd, t: (0, lat_idx(dd, t), j)),
            pl.BlockSpec((bsz, ctx_len, LANES), lambda j, dd, t: (0, ctx_blk, j)),
            per_block(kc), per_block(ec), per_block(cl), const(r16), const(r64), per_block(a),
        ],
        out_specs=(
            pl.BlockSpec((None, bsz, tl, LANES), lambda j, dd, t: (dd, 0, lat_idx(dd, t), j)),
            pl.BlockSpec((None, bsz, ctx_len, LANES), lambda j, dd, t: (dd, 0, 0, j)),
        ),
        scratch_shapes=[
            pltpu.VMEM((nstate // LANES, bsz, LANES), F32),
            pltpu.VMEM((nstate // LANES, bsz * (tl // S5_CHUNK + SUBLANES), LANES), F32),
            pltpu.VMEM((nstate // LANES, bsz * (tl // S5_CHUNK + SUBLANES), LANES), F32),
            pltpu.VMEM((nt, nt + nstate), BF16),
            pltpu.VMEM((nstate, nt), BF16),
            pltpu.VMEM((bsz * tl // S5_CHUNK, nt), F32),
        ],
        compiler_params=_cparams(("arbitrary", "arbitrary", "arbitrary")),
        name="s5_scan",
    )(u3, u3, kc, ec, cl, r16, r64, a)


def _gelu_tanh(x):
    return 0.5 * x * (1.0 + jnp.tanh(math.sqrt(2.0 / math.pi) * (x + 0.044715 * (x * x * x))))


def _s5_glu_kernel(h_ref, yf_ref, yr_ref, mod_ref, dvec_ref, w_ref, b_ref, *rest):
    o_ref = rest[-1]
    h = h_ref[...]
    mod = mod_ref[0]
    y = dvec_ref[...] * _normmod(h, mod[GS_M:GS_M + 1], mod[SH_M:SH_M + 1]) + (yf_ref[...] + yr_ref[...])
    z = _gelu_tanh(y)
    gate = 1.0 / (1.0 + jnp.exp(-(_dot(z.astype(BF16), w_ref[...]) + b_ref[...])))
    o_ref[...] = h + mod[G_M:G_M + 1] * (z * gate)


def _s5_glu(lat_src, ctx_src, y_lat, y_ctx, mod4, dvec, glu_w, glu_b, seq):
    bsz, tps, _, d = mod4.shape
    ctx_len = y_ctx.shape[2]
    tot = seq + ctx_len
    wb = (dvec.reshape(1, d), glu_w.astype(BF16), glu_b.reshape(1, d))
    vec2 = lambda *_: (0, 0)
    wb_specs = [pl.BlockSpec((1, d), vec2), pl.BlockSpec((d, d), vec2), pl.BlockSpec((1, d), vec2)]
    lat_arr, _ = lat_src
    tm = ROW_BLOCK
    part = pl.pallas_call(
        _s5_glu_kernel,
        out_shape=jax.ShapeDtypeStruct((bsz, tot, d), F32),
        grid=(bsz, seq // tm),
        in_specs=[pl.BlockSpec((None, tm, d), lambda b, m: (b, m, 0)),
                  pl.BlockSpec((None, None, tm, d), lambda b, m: (0, b, m, 0)),
                  pl.BlockSpec((None, None, tm, d), lambda b, m: (1, b, m, 0)),
                  pl.BlockSpec((None, 1, MOD_ROWS, d), lambda b, m: (b, 0, 0, 0))] + wb_specs,
        out_specs=pl.BlockSpec((None, tm, d), lambda b, m: (b, m, 0)),
        compiler_params=_cparams(("arbitrary", "arbitrary")),
        name="s5_glu_latent",
    )(lat_arr, y_lat, y_lat, mod4, *wb)
    ctx_arr, ctx_blk = ctx_src
    out_blk = seq // ctx_len
    return pl.pallas_call(
        _s5_glu_kernel,
        out_shape=jax.ShapeDtypeStruct((bsz, tot, d), F32),
        grid=(bsz,),
        in_specs=[pl.BlockSpec((None, ctx_len, d), lambda b: (b, ctx_blk, 0)),
                  pl.BlockSpec((None, None, ctx_len, d), lambda b: (0, b, 0, 0)),
                  pl.BlockSpec((None, None, ctx_len, d), lambda b: (1, b, 0, 0)),
                  pl.BlockSpec((None, 1, MOD_ROWS, d), lambda b: (b, tps - 1, 0, 0))] + wb_specs + [
                  pl.BlockSpec(memory_space=pl.ANY)],
        out_specs=pl.BlockSpec((None, ctx_len, d), lambda b: (b, out_blk, 0)),
        input_output_aliases={7: 0},
        compiler_params=_cparams(("arbitrary",)),
        name="s5_glu_context",
    )(ctx_arr, y_ctx, y_ctx, mod4, *wb, part)


def _pool_matrices(width, tile):
    mats = np.zeros((len(POOL_WINDOWS), tile, tile), np.float32)
    pos = np.arange(width)
    for g, win in enumerate(POOL_WINDOWS):
        lo = np.clip(pos - win // 2, 0, width)
        hi = np.clip(pos + win // 2, 0, width)
        for base in range(0, tile, width):
            for j in range(width):
                mats[g, base + j, base + lo[j]:base + hi[j]] = 1.0 / float(hi[j] - lo[j])
    return mats


def _pool_kernel(tps, h_ref, mod_ref, a_ref, w_ref, b_ref, sc_ref, o_ref):
    gw = w_ref.shape[-1]
    nsub = h_ref.shape[0] // TILE
    for j in range(nsub):
        rows = slice(j * TILE, (j + 1) * TILE)
        mod = mod_ref[j]
        is_ctx = (pl.program_id(0) * nsub + j) % tps == tps - 1
        h = h_ref[rows, :]
        u = _normmod(h, mod[GS_M:GS_M + 1], mod[SH_M:SH_M + 1])
        ub = u.astype(BF16)
        for g in range(w_ref.shape[0]):
            cols = slice(g * gw, (g + 1) * gw)
            avg = jnp.where(is_ctx, a_ref[1, g], a_ref[0, g])
            res = _dot(avg, ub[:, cols]) - u[:, cols]
            mixed = _dot(res.astype(BF16), w_ref[g]) + b_ref[:, cols]
            o_ref[rows, cols] = h[:, cols] + mod[G_M:G_M + 1, cols] * (mixed * sc_ref[:, cols])


def _pool(h, mod, pool_w, pool_b, pool_scale, seq):
    r, d = h.shape
    tps = seq // TILE + 1
    ng, gw, _ = pool_w.shape
    mats = jnp.asarray(np.stack([_pool_matrices(GRID_W, TILE), _pool_matrices(TILE, TILE)]), BF16)
    row = pl.BlockSpec((ROW_BLOCK, d), lambda i: (i, 0))
    vec = pl.BlockSpec((1, d), lambda i: (0, 0))
    return pl.pallas_call(
        functools.partial(_pool_kernel, tps),
        out_shape=jax.ShapeDtypeStruct((r, d), F32),
        grid=(r // ROW_BLOCK,),
        in_specs=[row, pl.BlockSpec((ROW_BLOCK // TILE, MOD_ROWS, d), lambda i: (i, 0, 0)),
                  pl.BlockSpec((2, ng, TILE, TILE), lambda i: (0, 0, 0, 0)),
                  pl.BlockSpec((ng, gw, gw), lambda i: (0, 0, 0)), vec, vec],
        out_specs=row,
        compiler_params=_cparams(("arbitrary",)),
        name="pool_mixer",
    )(h, mod, mats, pool_w.astype(BF16), pool_b.reshape(1, d), pool_scale.reshape(1, d))


def _dft_tables(n):
    idx = np.arange(n)
    ang = 2.0 * np.pi * ((idx[:, None] * idx[None, :]) % n) / n
    scale = 1.0 / math.sqrt(n)
    return (np.cos(ang) * scale).astype(np.float32), (np.sin(ang) * scale).astype(np.float32)


def _fnet_fold_kernel(cc_ref, sc_ref, w_ref, o_ref):
    w = w_ref[...]
    d = w.shape[-1]
    o_ref[:, :d] = _dot3(cc_ref[...], w).astype(o_ref.dtype)
    o_ref[:, d:] = (-_dot3(sc_ref[...], w)).astype(o_ref.dtype)


def _fnet_fold(fnet_w):
    d = fnet_w.shape[0]
    gw = d // FNET_GROUPS
    cc, sc = _dft_tables(gw)
    sq = pl.BlockSpec((gw, gw), lambda i: (0, 0))
    return pl.pallas_call(
        _fnet_fold_kernel,
        out_shape=jax.ShapeDtypeStruct((d, 2 * d), BF16),
        grid=(FNET_GROUPS,),
        in_specs=[sq, sq, pl.BlockSpec((gw, d), lambda i: (i, 0))],
        out_specs=pl.BlockSpec((gw, 2 * d), lambda i: (i, 0)),
        compiler_params=_cparams(("arbitrary",)),
        name="fnet_fold",
    )(jnp.asarray(cc), jnp.asarray(sc), fnet_w)


def _fnet_proj_kernel(h_ref, mod_ref, g_ref, v_ref):
    for j in range(h_ref.shape[0] // TILE):
        rows = slice(j * TILE, (j + 1) * TILE)
        mod = mod_ref[j]
        u = _normmod(h_ref[rows, :], mod[GS_M:GS_M + 1], mod[SH_M:SH_M + 1])
        v_ref[rows, :] = _dot(u.astype(BF16), g_ref[...]).astype(v_ref.dtype)


def _fnet_proj(h, mod, g):
    r, d = h.shape
    sub = ROW_BLOCK // TILE
    return pl.pallas_call(
        _fnet_proj_kernel,
        out_shape=jax.ShapeDtypeStruct((r, 2 * d), BF16),
        grid=(r // ROW_BLOCK,),
        in_specs=[pl.BlockSpec((ROW_BLOCK, d), lambda i: (i, 0)),
                  pl.BlockSpec((sub, MOD_ROWS, d), lambda i: (i, 0, 0)),
                  pl.BlockSpec((d, 2 * d), lambda i: (0, 0))],
        out_specs=pl.BlockSpec((ROW_BLOCK, 2 * d), lambda i: (i, 0)),
        compiler_params=_cparams(("arbitrary",)),
        name="fnet_proj",
    )(h, mod, g)


def _dft_matrix(n):
    n1 = int(round(math.sqrt(n)))
    assert n1 * n1 == n
    hi = jnp.arange(n1, dtype=jnp.int32)[:, None]
    k = jnp.arange(n, dtype=jnp.int32)[None, :]
    ang_a = (2.0 * math.pi / n1) * ((hi * k) % n1).astype(F32)
    ang_b = (2.0 * math.pi / n) * ((hi * k) % n).astype(F32)
    ca, sa, cb, sb = jnp.cos(ang_a), jnp.sin(ang_a), jnp.cos(ang_b), jnp.sin(ang_b)
    scale = 1.0 / math.sqrt(n)
    c = (ca[:, None] * cb[None] - sa[:, None] * sb[None]).reshape(n, n) * scale
    s = (sa[:, None] * cb[None] + ca[:, None] * sb[None]).reshape(n, n) * scale
    return jnp.concatenate([c, s], axis=1).astype(BF16)


def _fnet_lat_kernel(f_ref, v_ref, h_ref, mod_ref, b_ref, o_ref, acc_ref):
    k = pl.program_id(2)

    @pl.when(k == 0)
    def _():
        acc_ref[...] = jnp.zeros_like(acc_ref)

    acc_ref[...] += _dot(f_ref[...], v_ref[0])

    @pl.when(k == pl.num_programs(2) - 1)
    def _():
        for j in range(h_ref.shape[1] // TILE):
            rows = slice(j * TILE, (j + 1) * TILE)
            o_ref[0, rows, :] = h_ref[0, rows, :] + mod_ref[0, j, G_M:G_M + 1] * (acc_ref[rows, :] + b_ref[...])


def _fnet_lat(dft, v3, h3, mod4, fnet_b, seq):
    bsz, tot, d = h3.shape
    tm, tk = ROW_BLOCK, min(2048, seq)
    kh = seq // tk
    sub = tm // TILE
    return pl.pallas_call(
        _fnet_lat_kernel,
        out_shape=jax.ShapeDtypeStruct((bsz, tot, d), F32),
        grid=(bsz, seq // tm, 2 * kh),
        in_specs=[pl.BlockSpec((tm, tk), lambda b, m, k: (m, k)),
                  pl.BlockSpec((1, tk, d), lambda b, m, k: (b, k % kh, k // kh)),
                  pl.BlockSpec((1, tm, d), lambda b, m, k: (b, m, 0)),
                  pl.BlockSpec((1, sub, MOD_ROWS, d), lambda b, m, k: (b, m, 0, 0)),
                  pl.BlockSpec((1, d), lambda b, m, k: (0, 0))],
        out_specs=pl.BlockSpec((1, tm, d), lambda b, m, k: (b, m, 0)),
        scratch_shapes=[pltpu.VMEM((tm, d), F32)],
        compiler_params=_cparams(("arbitrary", "arbitrary", "arbitrary")),
        name="fnet_dft_latent",
    )(dft, v3, h3, mod4, fnet_b.reshape(1, d))


def _fnet_ctx_kernel(f_ref, v_ref, h_ref, mod_ref, b_ref, prev_ref, o_ref):
    del prev_ref
    n = f_ref.shape[0]
    d = h_ref.shape[-1]
    acc = _dot(f_ref[:, :n], v_ref[0, :, :d]) + _dot(f_ref[:, n:], v_ref[0, :, d:])
    o_ref[0] = h_ref[0] + mod_ref[0, 0, G_M:G_M + 1] * (acc + b_ref[...])


def _fnet_ctx(dftc, v3, h3, mod4, fnet_b, partial, seq):
    bsz, tot, d = h3.shape
    n = tot - seq
    blk = seq // n
    return pl.pallas_call(
        _fnet_ctx_kernel,
        out_shape=jax.ShapeDtypeStruct((bsz, tot, d), F32),
        grid=(bsz,),
        in_specs=[pl.BlockSpec((n, 2 * n), lambda b: (0, 0)),
                  pl.BlockSpec((1, n, 2 * d), lambda b: (b, blk, 0)),
                  pl.BlockSpec((1, n, d), lambda b: (b, blk, 0)),
                  pl.BlockSpec((1, 1, MOD_ROWS, d), lambda b: (b, blk, 0, 0)),
                  pl.BlockSpec((1, d), lambda b: (0, 0)),
                  pl.BlockSpec(memory_space=pl.ANY)],
        out_specs=pl.BlockSpec((1, n, d), lambda b: (b, blk, 0)),
        input_output_aliases={5: 0},
        compiler_params=_cparams(("arbitrary",)),
        name="fnet_dft_context",
    )(dftc, v3, h3, mod4, fnet_b.reshape(1, d), partial)


def _silu(x):
    return x * (1.0 / (1.0 + jnp.exp(-x)))


def _ffn_kernel(h_ref, mod_ref, wg_ref, wu_ref, wd_ref, o_ref, u_scr, acc_ref):
    f = pl.program_id(1)
    sub = h_ref.shape[0] // TILE

    @pl.when(f == 0)
    def _():
        for j in range(sub):
            rows = slice(j * TILE, (j + 1) * TILE)
            mod = mod_ref[j]
            u_scr[rows, :] = _normmod(h_ref[rows, :], mod[GS_F:GS_F + 1], mod[SH_F:SH_F + 1]).astype(BF16)
        acc_ref[...] = jnp.zeros_like(acc_ref)

    u = u_scr[...]
    a = _silu(_dot(u, wg_ref[...])) * _dot(u, wu_ref[...])
    acc_ref[...] += _dot(a.astype(BF16), wd_ref[...])

    @pl.when(f == pl.num_programs(1) - 1)
    def _():
        for j in range(sub):
            rows = slice(j * TILE, (j + 1) * TILE)
            o_ref[rows, :] = h_ref[rows, :] + mod_ref[j, G_F:G_F + 1] * acc_ref[rows, :]


def _ffn(h, mod, w_gate, w_up, w_down):
    r, d = h.shape
    ff = w_gate.shape[1]
    tm, tf = ROW_BLOCK, 256
    sub = tm // TILE
    return pl.pallas_call(
        _ffn_kernel,
        out_shape=jax.ShapeDtypeStruct((r, d), F32),
        grid=(r // tm, ff // tf),
        in_specs=[pl.BlockSpec((tm, d), lambda i, f: (i, 0)),
                  pl.BlockSpec((sub, MOD_ROWS, d), lambda i, f: (i, 0, 0)),
                  pl.BlockSpec((d, tf), lambda i, f: (0, f)),
                  pl.BlockSpec((d, tf), lambda i, f: (0, f)),
                  pl.BlockSpec((tf, d), lambda i, f: (f, 0))],
        out_specs=pl.BlockSpec((tm, d), lambda i, f: (i, 0)),
        scratch_shapes=[pltpu.VMEM((tm, d), BF16), pltpu.VMEM((tm, d), F32)],
        compiler_params=_cparams(("arbitrary", "arbitrary")),
        name="ffn_swiglu",
    )(h, mod, w_gate.astype(BF16), w_up.astype(BF16), w_down.astype(BF16))


MOE_TILE = 2048
MOE_CHUNK = 128


def _router_kernel(tiles_per_seq, route_ctx, h_ref, mod_ref, r_ref, u_ref, cw_ref, slot_ref, slott_ref, cnt_ref):
    nsub = h_ref.shape[0] // TILE
    lane = lax.broadcasted_iota(jnp.int32, (TILE, LANES), 1).astype(F32)
    earlier = jnp.where(lax.broadcasted_iota(jnp.int32, (TILE, TILE), 1)
                        < lax.broadcasted_iota(jnp.int32, (TILE, TILE), 0), 1.0, 0.0).astype(BF16)
    neg = jnp.float32(-jnp.inf)
    count = jnp.zeros((1, LANES), F32)
    for j in range(nsub):
        rows = slice(j * TILE, (j + 1) * TILE)
        mod = mod_ref[j]
        u = _normmod(h_ref[rows, :], mod[GS_F:GS_F + 1], mod[SH_F:SH_F + 1])
        u_ref[rows, :] = u.astype(BF16)
        logits = _dot3(u, r_ref[...])
        logits = jnp.where(lane < N_EXPERTS, logits, neg)
        m1 = jnp.max(logits, axis=-1, keepdims=True)
        i1 = jnp.min(jnp.where(logits == m1, lane, float(LANES)), axis=-1, keepdims=True)
        rest = jnp.where(lane == i1, neg, logits)
        m2 = jnp.max(rest, axis=-1, keepdims=True)
        i2 = jnp.min(jnp.where(rest == m2, lane, float(LANES)), axis=-1, keepdims=True)
        e2 = jnp.exp(m2 - m1)
        w1 = 1.0 / (1.0 + e2)
        w2 = e2 / (1.0 + e2)
        cw = jnp.where(lane == i1, w1, jnp.where(lane == i2, w2, 0.0))
        sel = jnp.where(lane == i1, 1.0, jnp.where(lane == i2, 1.0, 0.0))
        if not route_ctx:
            is_ctx = (pl.program_id(0) * nsub + j) % tiles_per_seq == tiles_per_seq - 1
            cw = jnp.where(is_ctx, 0.0, cw)
            sel = jnp.where(is_ctx, 0.0, sel)
        cw_ref[rows, :] = cw
        slot = jnp.where(sel > 0.0, _dot(earlier, sel.astype(BF16)) + count, -1.0)
        slot_ref[rows, :] = slot
        slott_ref[:, rows] = slot.T[:N_EXPERTS, :]
        count = count + jnp.sum(sel, axis=0, keepdims=True)
    cnt_ref[0] = jnp.broadcast_to(count, (N_EXPERTS, LANES))


def _router(h, mod, router, tiles_per_seq, route_ctx):
    r, d = h.shape
    rp = jnp.zeros((d, LANES), F32).at[:, :N_EXPERTS].set(router)
    sub = MOE_TILE // TILE
    row = lambda w: pl.BlockSpec((MOE_TILE, w), lambda i: (i, 0))
    return pl.pallas_call(
        functools.partial(_router_kernel, tiles_per_seq, route_ctx),
        out_shape=(jax.ShapeDtypeStruct((r, d), BF16), jax.ShapeDtypeStruct((r, LANES), F32),
                   jax.ShapeDtypeStruct((r, LANES), F32), jax.ShapeDtypeStruct((N_EXPERTS, r), F32),
                   jax.ShapeDtypeStruct((r // MOE_TILE, N_EXPERTS, LANES), F32)),
        grid=(r // MOE_TILE,),
        in_specs=[row(d),
                  pl.BlockSpec((sub, MOD_ROWS, d), lambda i: (i, 0, 0)),
                  pl.BlockSpec((d, LANES), lambda i: (0, 0))],
        out_specs=(row(d), row(LANES), row(LANES),
                   pl.BlockSpec((N_EXPERTS, MOE_TILE), lambda i: (0, i)),
                   pl.BlockSpec((1, N_EXPERTS, LANES), lambda i: (i, 0, 0))),
        compiler_params=_cparams(("arbitrary",)),
        name="moe_router",
    )(h, mod, rp)


def _moe_kernel(nch_ref, h_ref, u_ref, cw_ref, slot_ref, slott_ref, mod_ref, wg_ref, wu_ref, wd_ref, o_ref,
                xs_scr, y_scr):
    i = pl.program_id(0)
    e = pl.program_id(1)
    f = pl.program_id(2)
    last_f = pl.num_programs(2) - 1
    tile = u_ref.shape[0]
    nch = nch_ref[i * N_EXPERTS + e]
    block = 4 * MOE_CHUNK

    @pl.when((e == 0) & (f == 0))
    def _():
        o_ref[...] = jnp.zeros_like(o_ref)

    nblk = lax.shift_right_logical(nch, 2)
    tail2 = pl.multiple_of(nblk * block, MOE_CHUNK)
    tail1 = pl.multiple_of(tail2 + (nch & 2) * MOE_CHUNK, MOE_CHUNK)

    def run_block(r0, rn):
        rows = pl.ds(r0, rn)

        @pl.when(f == 0)
        def _():
            rid = (lax.broadcasted_iota(jnp.int32, (rn, 1), 0) + r0).astype(F32)
            pick = jnp.where(slott_ref[pl.ds(e, 1), :] == rid, 1.0, 0.0).astype(BF16)
            xs_scr[rows, :] = _dot(pick, u_ref[...]).astype(BF16)

        xs = xs_scr[rows, :]
        a = _silu(_dot(xs, wg_ref[0])) * _dot(xs, wu_ref[0])
        part = _dot(a.astype(BF16), wd_ref[0])

        @pl.when(f == 0)
        def _():
            y_scr[rows, :] = part

        @pl.when(f > 0)
        def _():
            y_scr[rows, :] += part

    def loop_blocks(fn):
        def body(b, carry):
            fn(pl.multiple_of(b * block, block), block)
            return carry
        lax.fori_loop(0, nblk, body, 0)
        pl.when((nch & 2) != 0)(lambda: fn(tail2, 2 * MOE_CHUNK))
        pl.when((nch & 1) != 0)(lambda: fn(tail1, MOE_CHUNK))

    loop_blocks(run_block)

    @pl.when((f == last_f) & (nch > 0))
    def _():
        mine = lax.broadcasted_iota(jnp.int32, (tile, LANES), 1) == e
        slot_col = jnp.sum(jnp.where(mine, slot_ref[...], 0.0), axis=-1, keepdims=True)
        cw_col = jnp.sum(jnp.where(mine, cw_ref[...], 0.0), axis=-1, keepdims=True)

        def scatter_block(r0, rn):
            cid = (lax.broadcasted_iota(jnp.int32, (1, rn), 1) + r0).astype(F32)
            put = jnp.where(slot_col == cid, 1.0, 0.0).astype(BF16)
            o_ref[...] += cw_col * _dot(put, y_scr[pl.ds(r0, rn), :].astype(BF16))

        loop_blocks(scatter_block)

    @pl.when((e == pl.num_programs(1) - 1) & (f == last_f))
    def _():
        for j in range(tile // TILE):
            rows = slice(j * TILE, (j + 1) * TILE)
            o_ref[rows, :] = h_ref[rows, :] + mod_ref[j, G_F:G_F + 1] * o_ref[rows, :]


def _moe(h, u, cw, slot, slott, counts, mod, w_gate, w_up, w_down, layer):
    r, d = h.shape
    _, ne, _, ff = w_gate.shape
    tm, tf = MOE_TILE, 512
    sub = tm // TILE
    nch = ((counts[:, 0, :ne].astype(jnp.int32) + (MOE_CHUNK - 1)) // MOE_CHUNK).reshape(-1)
    once = pl.Buffered(1)
    row = lambda w: pl.BlockSpec((tm, w), lambda i, e, f, n: (i, 0), pipeline_mode=once)
    return pl.pallas_call(
        _moe_kernel,
        out_shape=jax.ShapeDtypeStruct((r, d), F32),
        grid_spec=pltpu.PrefetchScalarGridSpec(
            num_scalar_prefetch=1,
            grid=(r // tm, ne, ff // tf),
            in_specs=[row(d), row(d), row(LANES), row(LANES),
                      pl.BlockSpec((ne, tm), lambda i, e, f, n: (0, i), pipeline_mode=once),
                      pl.BlockSpec((sub, MOD_ROWS, d), lambda i, e, f, n: (i, 0, 0)),
                      pl.BlockSpec((None, 1, d, tf), lambda i, e, f, n: (layer, e, 0, f)),
                      pl.BlockSpec((None, 1, d, tf), lambda i, e, f, n: (layer, e, 0, f)),
                      pl.BlockSpec((None, 1, tf, d), lambda i, e, f, n: (layer, e, f, 0))],
            out_specs=pl.BlockSpec((tm, d), lambda i, e, f, n: (i, 0), pipeline_mode=once),
            scratch_shapes=[pltpu.VMEM((tm, d), BF16), pltpu.VMEM((tm, d), F32)]),
        compiler_params=_cparams(("arbitrary", "arbitrary", "arbitrary")),
        name="moe_experts",
    )(nch, h, u, cw, slot, slott, mod, w_gate, w_up, w_down)


def _final_kernel(h_ref, g_ref, o_ref):
    h = h_ref[0]
    inv = lax.rsqrt(jnp.mean(h * h, axis=-1, keepdims=True) + EPS)
    o_ref[0] = h * inv * g_ref[...]


def _final_norm(h3, gain, seq):
    bsz, _, d = h3.shape
    return pl.pallas_call(
        _final_kernel,
        out_shape=jax.ShapeDtypeStruct((bsz, seq, d), F32),
        grid=(bsz, seq // ROW_BLOCK),
        in_specs=[pl.BlockSpec((1, ROW_BLOCK, d), lambda b, t: (b, t, 0)),
                  pl.BlockSpec((1, d), lambda b, t: (0, 0))],
        out_specs=pl.BlockSpec((1, ROW_BLOCK, d), lambda b, t: (b, t, 0)),
        compiler_params=_cparams(("arbitrary", "arbitrary")),
        name="final_norm",
    )(h3, gain.reshape(1, d))


def _tile_mods(mods, gain_mix, gain_ffn, bsz, lat_tiles):
    sh_m, sc_m, g_m, sh_f, sc_f, g_f = (mods[:, i] for i in range(N_MOD))
    rows = jnp.stack([gain_mix * (1.0 + sc_m), sh_m, g_m, gain_ffn * (1.0 + sc_f), sh_f, g_f,
                      jnp.zeros_like(g_f), jnp.zeros_like(g_f)], axis=1)
    lat = jnp.broadcast_to(rows[:bsz, None], (bsz, lat_tiles) + rows.shape[1:])
    ctx = jnp.broadcast_to(rows[bsz:, None], (bsz, 1) + rows.shape[1:])
    return jnp.concatenate([lat, ctx], axis=1).reshape(bsz * (lat_tiles + 1), MOD_ROWS, rows.shape[-1])


def kernel(x, c, ctx, c_ctx, ada_w, ada_b, norm_mix, norm_ffn, norm_final, s5_lambda_re, s5_lambda_im, s5_log_step, s5_b_re, s5_b_im, s5_c_re, s5_c_im, s5_d, s5_glu_w, s5_glu_b, pool_w, pool_b, pool_scale, fnet_w, fnet_b, ffn_w_gate, ffn_w_up, ffn_w_down, moe_router, moe_w_gate, moe_w_up, moe_w_down):
    bsz, seq, d = x.shape
    ctx_len = ctx.shape[1]
    depth = ada_w.shape[0]
    assert ctx_len == TILE and seq % ROW_BLOCK == 0 and d % LANES == 0
    tot = seq + ctx_len
    lat_tiles = seq // TILE
    tps = lat_tiles + 1
    r = bsz * tot
    assert r % ROW_BLOCK == 0

    cond = jnp.zeros((16, d), F32).at[:bsz].set(c).at[bsz].set(c_ctx)
    mods_all = _ada_all(cond, ada_w, ada_b)[:, :bsz + 1].reshape(depth, bsz + 1, N_MOD, d)

    moe_wg, moe_wu, moe_wd = (w.astype(BF16) for w in (moe_w_gate, moe_w_up, moe_w_down))
    mods_tiled = jax.vmap(lambda m, gm, gf: _tile_mods(m, gm, gf, bsz, lat_tiles))(mods_all, norm_mix, norm_ffn)
    s5_tables = jax.vmap(_s5_params)(s5_lambda_re, s5_lambda_im, s5_log_step, s5_b_re, s5_b_im, s5_c_re, s5_c_im)
    h = None
    for i in range(depth):
        kind, j = i % 3, i // 3
        mod = mods_tiled[i]
        if kind == 0:
            mod4 = mod.reshape(bsz, tps, MOD_ROWS, d)
            if h is None:
                lat_src, ctx_src = (x, 0), (ctx, 0)
                u3 = _s5_pre_split(x, ctx, mod4)
            else:
                h3 = h.reshape(bsz, tot, d)
                lat_src, ctx_src = (h3, 0), (h3, seq // ctx_len)
                u3 = _s5_pre((h,), mod, tps).reshape(bsz, tot, d)
            y_lat, y_ctx = _s5_scan(u3, *s5_tables, j, seq, ctx_len)
            h = _s5_glu(lat_src, ctx_src, y_lat, y_ctx, mod4,
                        s5_d[j], s5_glu_w[j], s5_glu_b[j], seq).reshape(r, d)
        elif kind == 1:
            h = _pool(h, mod, pool_w[j], pool_b[j], pool_scale[j], seq)
        else:
            g = _fnet_fold(fnet_w[j])
            v3 = _fnet_proj(h, mod, g).reshape(bsz, tot, 2 * d)
            h3 = h.reshape(bsz, tot, d)
            mod4 = mod.reshape(bsz, tps, MOD_ROWS, d)
            part = _fnet_lat(_dft_matrix(seq), v3, h3, mod4, fnet_b[j], seq)
            cc, sc = _dft_tables(ctx_len)
            dftc = jnp.asarray(np.concatenate([cc, sc], axis=1), BF16)
            h = _fnet_ctx(dftc, v3, h3, mod4, fnet_b[j], part, seq).reshape(r, d)
        kk = i // 2
        if i % 2 == 0:
            h = _ffn(h, mod, ffn_w_gate[kk], ffn_w_up[kk], ffn_w_down[kk])
        else:
            ub, cw, slot, slott, counts = _router(h, mod, moe_router[kk], tps, route_ctx=i < depth - 1)
            h = _moe(h, ub, cw, slot, slott, counts, mod, moe_wg, moe_wu, moe_wd, kk)
    return _final_norm(h.reshape(bsz, tot, d), norm_final, seq)
```
